```python
import math
import numpy as np
import jax
import jax.numpy as jnp
from jax import lax

D_MODEL = 4096
BATCH = 2
SEQ = 4096
DEPTH = 4

HEAD_DIM = 128
N_HEAD_SLOTS = D_MODEL // HEAD_DIM
DIFF_HEADS = N_HEAD_SLOTS // 4
DIFF_V_DIM = 2 * HEAD_DIM
MOBA_HEADS = N_HEAD_SLOTS // 2
NSA_Q_HEADS = N_HEAD_SLOTS
NSA_KV_HEADS = 4
NSA_GROUP = NSA_Q_HEADS // NSA_KV_HEADS
ROT_DIM = HEAD_DIM // 4
ROPE_THETA = 500000.0
D_FF = 7 * D_MODEL // 4
RMS_EPS = 1e-6
NEG = -1e30
BIG = 1e30
DENSE_Q_BLOCK = 128
MOBA_BLOCK = 256
MOBA_TOPK = 3
MOBA_Q_CHUNK = 16
NSA_CMP_LEN = 32
NSA_CMP_STRIDE = 16
NSA_CMP_HIDDEN = 256
NSA_SEL_BLOCK = 64
NSA_SEL_TOPN = 16
NSA_WINDOW = 512
NSA_Q_CHUNK = 32
N_EVEN = (DEPTH + 1) // 2
N_ODD = DEPTH // 2
EVEN_IN = 2 * DIFF_HEADS * 2 * HEAD_DIM + DIFF_HEADS * DIFF_V_DIM + 3 * MOBA_HEADS * HEAD_DIM
EVEN_OUT = DIFF_HEADS * DIFF_V_DIM + MOBA_HEADS * HEAD_DIM
ODD_IN = NSA_Q_HEADS * HEAD_DIM + 6 * NSA_KV_HEADS * HEAD_DIM + 3 * NSA_Q_HEADS
ODD_OUT = NSA_Q_HEADS * HEAD_DIM

kernel_name = "hybrid_diff_moba_nsa_macaron"


def rms_norm(x, g):
    xf = x.astype(jnp.float32)
    y = xf * lax.rsqrt(jnp.mean(xf * xf, axis=-1, keepdims=True) + RMS_EPS)
    return (y * g.astype(jnp.float32)).astype(x.dtype)


def rope_tables(seq):
    pos = jnp.arange(seq, dtype=jnp.float32)
    inv = ROPE_THETA ** (-jnp.arange(0, ROT_DIM, 2, dtype=jnp.float32) / ROT_DIM)
    ang = pos[:, None] * inv[None, :]
    return jnp.cos(ang), jnp.sin(ang)


def partial_rope(x, cos, sin):
    half = ROT_DIM // 2
    c = cos.astype(x.dtype)
    s = sin.astype(x.dtype)
    x1 = x[..., :half]
    x2 = x[..., half:ROT_DIM]
    return jnp.concatenate([x1 * c - x2 * s, x2 * c + x1 * s, x[..., ROT_DIM:]], axis=-1)


def masked_softmax(scores, mask, axis=-1):
    return jax.nn.softmax(jnp.where(mask, scores.astype(jnp.float32), NEG), axis=axis)


def swiglu(h, w_gate, w_up, w_down):
    return (jax.nn.silu(h @ w_gate) * (h @ w_up)) @ w_down


def gather_blocks(blocks, ids):
    return jax.vmap(jax.vmap(lambda bl, ix: bl[ix]))(blocks, ids)


def diff_attention(q, k, v, lam, lam_init, subln_g):
    b, h, _, s, d = q.shape
    kpos = jnp.arange(s)
    scale = HEAD_DIM ** -0.5

    def block(i):
        start = i * DENSE_Q_BLOCK
        qb = lax.dynamic_slice_in_dim(q, start, DENSE_Q_BLOCK, axis=3)
        sc = jnp.einsum('bhcqd,bhckd->bhcqk', qb, k) * scale
        qpos = start + jnp.arange(DENSE_Q_BLOCK)
        p = masked_softmax(sc, kpos[None, :] <= qpos[:, None])
        a = p[:, :, 0] - lam * p[:, :, 1]
        return jnp.einsum('bhqk,bhkd->bhqd', a.astype(v.dtype), v)

    o = lax.map(block, jnp.arange(s // DENSE_Q_BLOCK))
    o = jnp.moveaxis(o, 0, 2).reshape(b, h, s, DIFF_V_DIM)
    return rms_norm(o, subln_g) * (1.0 - lam_init)


def moba_attention(q, k, v):
    b, h, s, d = q.shape
    nb = -(-s // MOBA_BLOCK)
    pad = nb * MOBA_BLOCK - s
    kb = jnp.pad(k, ((0, 0), (0, 0), (0, pad), (0, 0))).reshape(b, h, nb, MOBA_BLOCK, d)
    vb = jnp.pad(v, ((0, 0), (0, 0), (0, pad), (0, 0))).reshape(b, h, nb, MOBA_BLOCK, d)
    kmean = jnp.mean(kb.astype(jnp.float32), axis=3)
    qpos = jnp.arange(s)
    qblk = qpos // MOBA_BLOCK
    gate = jnp.einsum('bhsd,bhnd->bhsn', q.astype(jnp.float32), kmean)
    past = jnp.arange(nb)[None, :] < qblk[:, None]
    gate = jnp.where(past, gate, NEG)
    n_sel = min(MOBA_TOPK, nb)
    _, top_idx = lax.top_k(gate, n_sel)
    own = jnp.broadcast_to(qblk[None, None, :, None], (b, h, s, 1)).astype(top_idx.dtype)
    idx = jnp.concatenate([top_idx, own], axis=-1)
    valid = jnp.concatenate([jnp.arange(n_sel)[None, :] < qblk[:, None],
                             jnp.ones((s, 1), dtype=bool)], axis=-1)
    scale = HEAD_DIM ** -0.5

    def chunk(i):
        start = i * MOBA_Q_CHUNK
        qc = lax.dynamic_slice_in_dim(q, start, MOBA_Q_CHUNK, axis=2)
        ic = lax.dynamic_slice_in_dim(idx, start, MOBA_Q_CHUNK, axis=2)
        vc = lax.dynamic_slice_in_dim(valid, start, MOBA_Q_CHUNK, axis=0)
        kg = gather_blocks(kb, ic)
        vg = gather_blocks(vb, ic)
        kpos = ic[..., None] * MOBA_BLOCK + jnp.arange(MOBA_BLOCK)
        qp = start + jnp.arange(MOBA_Q_CHUNK)
        mask = vc[None, None, :, :, None] & (kpos <= qp[None, None, :, None, None])
        sc = jnp.einsum('bhqd,bhqnkd->bhqnk', qc, kg) * scale
        p = masked_softmax(sc.reshape(b, h, MOBA_Q_CHUNK, -1), mask.reshape(b, h, MOBA_Q_CHUNK, -1))
        p = p.reshape(sc.shape).astype(v.dtype)
        return jnp.einsum('bhqnk,bhqnkd->bhqd', p, vg)

    o = lax.map(chunk, jnp.arange(s // MOBA_Q_CHUNK))
    return jnp.moveaxis(o, 0, 2).reshape(b, h, s, d)


def nsa_attention(q, kc, vc, ks, vs, kw, vw, gates, cmp_pe, cmp_w1, cmp_w2):
    b, g, z, s, d = q.shape
    scale = HEAD_DIM ** -0.5
    qpos = jnp.arange(s)

    n_cmp = (s - NSA_CMP_LEN) // NSA_CMP_STRIDE + 1
    win = np.arange(n_cmp)[:, None] * NSA_CMP_STRIDE + np.arange(NSA_CMP_LEN)[None, :]

    def compress(t, pe, w1, w2):
        blocks = t[:, :, win] + pe
        flat = blocks.reshape(b, g, n_cmp, NSA_CMP_LEN * d)
        return jax.nn.silu(flat @ w1) @ w2

    k_cmp = compress(kc, cmp_pe[0], cmp_w1[0], cmp_w2[0])
    v_cmp = compress(vc, cmp_pe[1], cmp_w1[1], cmp_w2[1])
    cmp_end = jnp.asarray(np.arange(n_cmp) * NSA_CMP_STRIDE + NSA_CMP_LEN - 1)
    cmp_mask = cmp_end[None, :] <= qpos[:, None]
    sc = jnp.einsum('bgzsd,bgcd->bgzsc', q, k_cmp) * scale
    p_cmp = masked_softmax(sc, cmp_mask) * cmp_mask
    o_cmp = jnp.einsum('bgzsc,bgcd->bgzsd', p_cmp.astype(vc.dtype), v_cmp)

    ns = s // NSA_SEL_BLOCK
    c_lo = np.arange(n_cmp) * NSA_CMP_STRIDE
    j_lo = np.arange(ns) * NSA_SEL_BLOCK
    cover = ((c_lo[:, None] < j_lo[None, :] + NSA_SEL_BLOCK) &
             (c_lo[:, None] + NSA_CMP_LEN > j_lo[None, :])).astype(np.float32)
    imp = jnp.einsum('bgzsc,cj->bgsj', p_cmp, jnp.asarray(cover))
    qblk = qpos // NSA_SEL_BLOCK
    jb = jnp.arange(ns)[None, :]
    forced = (jb == 0) | (jb == qblk[:, None]) | (jb == qblk[:, None] - 1)
    allowed = jb <= qblk[:, None]
    imp = jnp.where(forced, BIG, jnp.where(allowed, imp, NEG))
    n_sel = min(NSA_SEL_TOPN, ns)
    _, sel_idx = lax.top_k(imp, n_sel)
    ksb = ks.reshape(b, g, ns, NSA_SEL_BLOCK, d)
    vsb = vs.reshape(b, g, ns, NSA_SEL_BLOCK, d)

    def sel_chunk(i):
        start = i * NSA_Q_CHUNK
        qc = lax.dynamic_slice_in_dim(q, start, NSA_Q_CHUNK, axis=3)
        ic = lax.dynamic_slice_in_dim(sel_idx, start, NSA_Q_CHUNK, axis=2)
        kg = gather_blocks(ksb, ic)
        vg = gather_blocks(vsb, ic)
        kpos = ic[..., None] * NSA_SEL_BLOCK + jnp.arange(NSA_SEL_BLOCK)
        qp = start + jnp.arange(NSA_Q_CHUNK)
        mask = (kpos <= qp[None, None, :, None, None])[:, :, None]
        scs = jnp.einsum('bgzqd,bgqnkd->bgzqnk', qc, kg) * scale
        p = masked_softmax(scs.reshape(b, g, z, NSA_Q_CHUNK, -1), mask.reshape(b, g, 1, NSA_Q_CHUNK, -1))
        return jnp.einsum('bgzqnk,bgqnkd->bgzqd', p.reshape(scs.shape).astype(vs.dtype), vg)

    o_sel = lax.map(sel_chunk, jnp.arange(s // NSA_Q_CHUNK))
    o_sel = jnp.moveaxis(o_sel, 0, 3).reshape(b, g, z, s, d)

    span = DENSE_Q_BLOCK + NSA_WINDOW
    kwp = jnp.pad(kw, ((0, 0), (0, 0), (NSA_WINDOW, 0), (0, 0)))
    vwp = jnp.pad(vw, ((0, 0), (0, 0), (NSA_WINDOW, 0), (0, 0)))

    def win_block(i):
        start = i * DENSE_Q_BLOCK
        qb = lax.dynamic_slice_in_dim(q, start, DENSE_Q_BLOCK, axis=3)
        kb = lax.dynamic_slice_in_dim(kwp, start, span, axis=2)
        vb = lax.dynamic_slice_in_dim(vwp, start, span, axis=2)
        kp = start - NSA_WINDOW + jnp.arange(span)
        qp = start + jnp.arange(DENSE_Q_BLOCK)
        dist = qp[:, None] - kp[None, :]
        mask = (kp[None, :] >= 0) & (dist >= 0) & (dist < NSA_WINDOW)
        scw = jnp.einsum('bgzqd,bgkd->bgzqk', qb, kb) * scale
        p = masked_softmax(scw, mask)
        return jnp.einsum('bgzqk,bgkd->bgzqd', p.astype(vw.dtype), vb)

    o_win = lax.map(win_block, jnp.arange(s // DENSE_Q_BLOCK))
    o_win = jnp.moveaxis(o_win, 0, 3).reshape(b, g, z, s, d)

    gt = jax.nn.sigmoid(gates.astype(jnp.float32)).astype(q.dtype)
    return gt[..., 0:1] * o_cmp + gt[..., 1:2] * o_sel + gt[..., 2:3] * o_win


def even_mixer(h, w_in, w_out, lam_params, subln_g, lam_init, cos, sin):
    b, s, _ = h.shape
    proj = h @ w_in
    sizes = [DIFF_HEADS * 2 * HEAD_DIM] * 2 + [DIFF_HEADS * DIFF_V_DIM] + [MOBA_HEADS * HEAD_DIM] * 3
    dq, dk, dv, mq, mk, mv = jnp.split(proj, np.cumsum(sizes)[:-1].tolist(), axis=-1)

    def diff_qk(t):
        return partial_rope(t.reshape(b, s, DIFF_HEADS, 2, HEAD_DIM).transpose(0, 2, 3, 1, 4), cos, sin)

    def heads(t, n, d):
        return t.reshape(b, s, n, d).transpose(0, 2, 1, 3)

    lp = lam_params.astype(jnp.float32)
    lam = jnp.exp(jnp.sum(lp[0] * lp[1])) - jnp.exp(jnp.sum(lp[2] * lp[3])) + lam_init
    o_diff = diff_attention(diff_qk(dq), diff_qk(dk), heads(dv, DIFF_HEADS, DIFF_V_DIM), lam, lam_init, subln_g)
    o_moba = moba_attention(partial_rope(heads(mq, MOBA_HEADS, HEAD_DIM), cos, sin),
                            partial_rope(heads(mk, MOBA_HEADS, HEAD_DIM), cos, sin),
                            heads(mv, MOBA_HEADS, HEAD_DIM))
    o = jnp.concatenate([o_diff.transpose(0, 2, 1, 3).reshape(b, s, -1),
                         o_moba.transpose(0, 2, 1, 3).reshape(b, s, -1)], axis=-1)
    return o @ w_out


def odd_mixer(h, w_in, w_out, cmp_pe, cmp_w1, cmp_w2, cos, sin):
    b, s, _ = h.shape
    proj = h @ w_in
    kvd = NSA_KV_HEADS * HEAD_DIM
    sizes = [NSA_Q_HEADS * HEAD_DIM] + [kvd] * 6 + [3 * NSA_Q_HEADS]
    q, kc, vc, ks, vs, kw, vw, gates = jnp.split(proj, np.cumsum(sizes)[:-1].tolist(), axis=-1)
    q = partial_rope(q.reshape(b, s, NSA_KV_HEADS, NSA_GROUP, HEAD_DIM).transpose(0, 2, 3, 1, 4), cos, sin)

    def kv(t):
        return t.reshape(b, s, NSA_KV_HEADS, HEAD_DIM).transpose(0, 2, 1, 3)

    gates = gates.reshape(b, s, NSA_KV_HEADS, NSA_GROUP, 3).transpose(0, 2, 3, 1, 4)
    o = nsa_attention(q, partial_rope(kv(kc), cos, sin), kv(vc),
                      partial_rope(kv(ks), cos, sin), kv(vs),
                      partial_rope(kv(kw), cos, sin), kv(vw),
                      gates, cmp_pe, cmp_w1, cmp_w2)
    o = o.transpose(0, 3, 1, 2, 4).reshape(b, s, ODD_OUT)
    return o @ w_out


def setup_inputs(seed: int = 0) -> dict:
    key = jax.random.key(seed)
    ks = jax.random.split(key, 17)
    f32 = jnp.float32

    def w(k, shape, fan_in):
        return jax.random.normal(k, shape, f32) * (fan_in ** -0.5)

    return {
        "x": jax.random.normal(ks[0], (BATCH, SEQ, D_MODEL), f32),
        "norm_g": 1.0 + 0.05 * jax.random.normal(ks[1], (DEPTH, 6, D_MODEL), f32),
        "ffn_a_gate": w(ks[2], (DEPTH, D_MODEL, D_FF), D_MODEL),
        "ffn_a_up": w(ks[3], (DEPTH, D_MODEL, D_FF), D_MODEL),
        "ffn_a_down": w(ks[4], (DEPTH, D_FF, D_MODEL), D_FF),
        "ffn_b_gate": w(ks[5], (DEPTH, D_MODEL, D_FF), D_MODEL),
        "ffn_b_up": w(ks[6], (DEPTH, D_MODEL, D_FF), D_MODEL),
        "ffn_b_down": w(ks[7], (DEPTH, D_FF, D_MODEL), D_FF),
        "ev_w_in": w(ks[8], (N_EVEN, D_MODEL, EVEN_IN), D_MODEL),
        "ev_w_out": w(ks[9], (N_EVEN, EVEN_OUT, D_MODEL), EVEN_OUT),
        "ev_lambda": 0.1 * jax.random.normal(ks[10], (N_EVEN, 4, HEAD_DIM), f32),
        "ev_subln_g": 1.0 + 0.05 * jax.random.normal(ks[11], (N_EVEN, DIFF_V_DIM), f32),
        "od_w_in": w(ks[12], (N_ODD, D_MODEL, ODD_IN), D_MODEL),
        "od_w_out": w(ks[13], (N_ODD, ODD_OUT, D_MODEL), ODD_OUT),
        "od_cmp_pe": 0.2 * jax.random.normal(ks[14], (N_ODD, 2, NSA_CMP_LEN, HEAD_DIM), f32),
        "od_cmp_w1": w(ks[15], (N_ODD, 2, NSA_CMP_LEN * HEAD_DIM, NSA_CMP_HIDDEN), NSA_CMP_LEN * HEAD_DIM),
        "od_cmp_w2": w(ks[16], (N_ODD, 2, NSA_CMP_HIDDEN, HEAD_DIM), NSA_CMP_HIDDEN),
    }


def reference(x, norm_g, ffn_a_gate, ffn_a_up, ffn_a_down, ffn_b_gate, ffn_b_up, ffn_b_down,
              ev_w_in, ev_w_out, ev_lambda, ev_subln_g,
              od_w_in, od_w_out, od_cmp_pe, od_cmp_w1, od_cmp_w2):
    cos, sin = rope_tables(x.shape[1])
    h = x
    for layer in range(DEPTH):
        g = norm_g[layer]
        h = h + 0.5 * rms_norm(swiglu(rms_norm(h, g[0]), ffn_a_gate[layer], ffn_a_up[layer], ffn_a_down[layer]), g[1])
        m_in = rms_norm(h, g[2])
        if layer % 2 == 0:
            e = layer // 2
            lam_init = 0.8 - 0.6 * math.exp(-0.3 * layer)
            m = even_mixer(m_in, ev_w_in[e], ev_w_out[e], ev_lambda[e], ev_subln_g[e], lam_init, cos, sin)
        else:
            o = layer // 2
            m = odd_mixer(m_in, od_w_in[o], od_w_out[o], od_cmp_pe[o], od_cmp_w1[o], od_cmp_w2[o], cos, sin)
        h = h + rms_norm(m, g[3])
        h = h + 0.5 * rms_norm(swiglu(rms_norm(h, g[4]), ffn_b_gate[layer], ffn_b_up[layer], ffn_b_down[layer]), g[5])
    return h
```

```python
import functools
import math

import numpy as np
import jax
import jax.numpy as jnp
from jax import lax
from jax.experimental import pallas as pl
from jax.experimental.pallas import tpu as pltpu

F32 = jnp.float32
BF16 = jnp.bfloat16

HEAD_DIM = 128
ROT_DIM = HEAD_DIM // 4
ROPE_THETA = 500000.0
RMS_EPS = 1e-6
NEG = -1e30
BIG = 1e30
ATTN_SCALE = HEAD_DIM ** -0.5

DIFF_HEADS = 8
DIFF_V_DIM = 2 * HEAD_DIM
MOBA_HEADS = 16
MOBA_BLOCK = 256
MOBA_TOPK = 3
NSA_Q_HEADS = 32
NSA_KV_HEADS = 4
NSA_GROUP = NSA_Q_HEADS // NSA_KV_HEADS
NSA_CMP_LEN = 32
NSA_CMP_STRIDE = 16
NSA_SEL_BLOCK = 64
NSA_SEL_TOPN = 16
NSA_WINDOW = 512
NSA_GATE_PAD = 128

LANES = 128
VMEM_LIMIT_BYTES = 56 * 1024 * 1024


def _params(*semantics):
    return pltpu.CompilerParams(dimension_semantics=semantics, vmem_limit_bytes=VMEM_LIMIT_BYTES)


def _dot(a, b):
    return jnp.dot(a, b, preferred_element_type=F32)


def _dot_nt(a, b):
    return lax.dot_general(a, b, (((1,), (1,)), ((), ())), preferred_element_type=F32)


def _split_bf16(x):
    hi = x.astype(BF16)
    lo = (x - hi.astype(F32)).astype(BF16)
    return hi, lo


def _rms(x, g):
    return x * lax.rsqrt(jnp.mean(x * x, axis=-1, keepdims=True) + RMS_EPS) * g


def _rmsnorm_kernel(x_ref, g_ref, o_ref):
    o_ref[...] = _rms(x_ref[...], g_ref[...]).astype(o_ref.dtype)


def rmsnorm(x, g, tm=256):
    m, d = x.shape
    tm = min(tm, m)
    return pl.pallas_call(
        _rmsnorm_kernel,
        grid=(m // tm,),
        in_specs=[pl.BlockSpec((tm, d), lambda i: (i, 0)), pl.BlockSpec((1, d), lambda i: (0, 0))],
        out_specs=pl.BlockSpec((tm, d), lambda i: (i, 0)),
        out_shape=jax.ShapeDtypeStruct((m, d), BF16),
        compiler_params=_params("parallel"),
    )(x, g.reshape(1, d))


def _residual_kernel(h_ref, m_ref, gpost_ref, gnext_ref, h_out_ref, hn_out_ref, *, coef):
    y = _rms(m_ref[...], gpost_ref[...])
    if coef != 1.0:
        y = coef * y
    h = h_ref[...] + y
    h_out_ref[...] = h
    hn_out_ref[...] = _rms(h, gnext_ref[...]).astype(hn_out_ref.dtype)


def _residual_last_kernel(h_ref, m_ref, gpost_ref, h_out_ref, *, coef):
    y = _rms(m_ref[...], gpost_ref[...])
    if coef != 1.0:
        y = coef * y
    h_out_ref[...] = h_ref[...] + y


def residual_norm(h, mix, g_post, g_next, coef, tm=256):
    m, d = h.shape
    tm = min(tm, m)
    row = pl.BlockSpec((tm, d), lambda i: (i, 0))
    vec = pl.BlockSpec((1, d), lambda i: (0, 0))
    if g_next is None:
        return pl.pallas_call(
            functools.partial(_residual_last_kernel, coef=coef),
            grid=(m // tm,),
            in_specs=[row, row, vec],
            out_specs=row,
            out_shape=jax.ShapeDtypeStruct((m, d), F32),
            compiler_params=_params("parallel"),
        )(h, mix, g_post.reshape(1, d)), None
    return pl.pallas_call(
        functools.partial(_residual_kernel, coef=coef),
        grid=(m // tm,),
        in_specs=[row, row, vec, vec],
        out_specs=[row, row],
        out_shape=[jax.ShapeDtypeStruct((m, d), F32), jax.ShapeDtypeStruct((m, d), BF16)],
        compiler_params=_params("parallel"),
    )(h, mix, g_post.reshape(1, d), g_next.reshape(1, d))


def _mm_kernel(x_ref, w_ref, o_ref):
    o_ref[...] = _dot(x_ref[...], w_ref[...]).astype(o_ref.dtype)


def _mm_ksplit_kernel(x_ref, w_ref, o_ref):
    k = pl.program_id(2)

    @pl.when(k == 0)
    def _():
        o_ref[...] = _dot(x_ref[...], w_ref[...])

    @pl.when(k > 0)
    def _():
        o_ref[...] += _dot(x_ref[...], w_ref[...])


def matmul(x, w, out_dtype, tm=1024, tn=1024, k_splits=1):
    m, kd = x.shape
    n = w.shape[1]
    tm, tn = min(tm, m), min(tn, n)
    if k_splits == 1:
        return pl.pallas_call(
            _mm_kernel,
            grid=(m // tm, n // tn),
            in_specs=[pl.BlockSpec((tm, kd), lambda i, j: (i, 0)), pl.BlockSpec((kd, tn), lambda i, j: (0, j))],
            out_specs=pl.BlockSpec((tm, tn), lambda i, j: (i, j)),
            out_shape=jax.ShapeDtypeStruct((m, n), out_dtype),
            compiler_params=_params("parallel", "parallel"),
        )(x, w)
    assert out_dtype == F32 and kd % k_splits == 0
    tk = kd // k_splits
    return pl.pallas_call(
        _mm_ksplit_kernel,
        grid=(m // tm, n // tn, k_splits),
        in_specs=[pl.BlockSpec((tm, tk), lambda i, j, k: (i, k)), pl.BlockSpec((tk, tn), lambda i, j, k: (k, j))],
        out_specs=pl.BlockSpec((tm, tn), lambda i, j, k: (i, j)),
        out_shape=jax.ShapeDtypeStruct((m, n), F32),
        compiler_params=_params("parallel", "parallel", "arbitrary"),
    )(x, w)


def _swiglu_kernel(x_ref, wg_ref, wu_ref, o_ref):
    x = x_ref[...]
    g = _dot(x, wg_ref[...])
    u = _dot(x, wu_ref[...])
    o_ref[...] = (g * jax.nn.sigmoid(g) * u).astype(o_ref.dtype)


def swiglu_up(x, w_gate, w_up, tm=1024, tn=512):
    m, kd = x.shape
    n = w_gate.shape[1]
    tm, tn = min(tm, m), min(tn, n)
    wspec = pl.BlockSpec((kd, tn), lambda i, j: (0, j))
    return pl.pallas_call(
        _swiglu_kernel,
        grid=(m // tm, n // tn),
        in_specs=[pl.BlockSpec((tm, kd), lambda i, j: (i, 0)), wspec, wspec],
        out_specs=pl.BlockSpec((tm, tn), lambda i, j: (i, j)),
        out_shape=jax.ShapeDtypeStruct((m, n), BF16),
        compiler_params=_params("parallel", "parallel"),
    )(x, w_gate, w_up)


def rope_tables(seq):
    half = ROT_DIM // 2
    pos = jnp.arange(seq, dtype=F32)
    inv = ROPE_THETA ** (-jnp.arange(0, ROT_DIM, 2, dtype=F32) / ROT_DIM)
    ang = pos[:, None] * inv[None, :]
    cos, sin = jnp.cos(ang), jnp.sin(ang)
    zeros = jnp.zeros((seq, HEAD_DIM - ROT_DIM), F32)
    zhalf = jnp.zeros((seq, half), F32)
    c = jnp.concatenate([cos, cos, zeros + 1.0], axis=-1)
    s1 = jnp.concatenate([zhalf, sin, zeros], axis=-1)
    s2 = jnp.concatenate([-sin, zhalf, zeros], axis=-1)
    return c, s1, s2


def _proj_rope_kernel(x_ref, w_ref, c_ref, s1_ref, s2_ref, o_ref, *, rope_pred):
    acc = _dot(x_ref[...], w_ref[...])
    is_rope = rope_pred(pl.program_id(1))
    half = ROT_DIM // 2

    @pl.when(is_rope)
    def _():
        c, s1, s2 = c_ref[...], s1_ref[...], s2_ref[...]
        for h in range(acc.shape[1] // HEAD_DIM):
            xh = acc[:, h * HEAD_DIM:(h + 1) * HEAD_DIM]
            r = xh * c + pltpu.roll(xh, half, 1) * s1 + pltpu.roll(xh, HEAD_DIM - half, 1) * s2
            o_ref[:, h * HEAD_DIM:(h + 1) * HEAD_DIM] = r.astype(o_ref.dtype)

    @pl.when(jnp.logical_not(is_rope))
    def _():
        o_ref[...] = acc.astype(o_ref.dtype)


def proj_rope(x, w, tables, seq, rope_pred, tm=1024, tn=1024):
    m, kd = x.shape
    n = w.shape[1]
    tm, tn = min(tm, seq), min(tn, n)
    assert seq % tm == 0 and m % tm == 0 and n % tn == 0
    pos_blocks = seq // tm
    tspec = pl.BlockSpec((tm, HEAD_DIM), lambda i, j: (i % pos_blocks, 0))
    return pl.pallas_call(
        functools.partial(_proj_rope_kernel, rope_pred=rope_pred),
        grid=(m // tm, n // tn),
        in_specs=[pl.BlockSpec((tm, kd), lambda i, j: (i, 0)), pl.BlockSpec((kd, tn), lambda i, j: (0, j)),
                  tspec, tspec, tspec],
        out_specs=pl.BlockSpec((tm, tn), lambda i, j: (i, j)),
        out_shape=jax.ShapeDtypeStruct((m, n), BF16),
        compiler_params=_params("parallel", "parallel"),
    )(x, w, *tables)


def _flash_init(slot, s, v, m_ref, l_ref, acc_ref):
    m = jnp.max(s, axis=-1, keepdims=True)
    p = jnp.exp(s - m)
    m_ref[slot] = m
    l_ref[slot] = jnp.sum(p, axis=-1, keepdims=True)
    acc_ref[slot] = _dot(p.astype(v.dtype), v)


def _flash_update(slot, s, v, m_ref, l_ref, acc_ref):
    m_old = m_ref[slot]
    m_new = jnp.maximum(m_old, jnp.max(s, axis=-1, keepdims=True))
    alpha = jnp.exp(m_old - m_new)
    p = jnp.exp(s - m_new)
    m_ref[slot] = m_new
    l_ref[slot] = alpha * l_ref[slot] + jnp.sum(p, axis=-1, keepdims=True)
    acc_ref[slot] = alpha * acc_ref[slot] + _dot(p.astype(v.dtype), v)


def _flash_result(slot, l_ref, acc_ref):
    return acc_ref[slot] * (1.0 / l_ref[slot])


def _rank_before(vals, n_cols):
    lane = lax.broadcasted_iota(jnp.int32, vals.shape, 1)
    rank = jnp.zeros(vals.shape, jnp.int32)
    for j2 in range(n_cols):
        col = vals[:, j2:j2 + 1]
        first = jnp.where(col > vals, 1, jnp.where(col == vals, (lane > j2).astype(jnp.int32), 0))
        rank = rank + first
    return rank


def _diff_kernel(lam_ref, g_ref, q_ref, k_ref, v_ref, o_ref, m_ref, l_ref, acc_ref, *, tq, lam_init):
    i = pl.program_id(2)
    lp = lam_ref[...]
    lam = (jnp.exp(jnp.sum(lp[0:1] * lp[1:2], axis=-1, keepdims=True))
           - jnp.exp(jnp.sum(lp[2:3] * lp[3:4], axis=-1, keepdims=True)) + lam_init)
    rows = lax.broadcasted_iota(jnp.int32, (tq, tq), 0)
    cols = lax.broadcasted_iota(jnp.int32, (tq, tq), 1)

    def scores(c, start):
        q = q_ref[0, :, c * HEAD_DIM:(c + 1) * HEAD_DIM]
        k = k_ref[0, pl.ds(start, tq), c * HEAD_DIM:(c + 1) * HEAD_DIM]
        return _dot_nt(q, k) * ATTN_SCALE

    own = pl.multiple_of(i * tq, tq)
    v_own = v_ref[0, pl.ds(own, tq), :]
    for c in range(2):
        s = jnp.where(cols <= rows, scores(c, own), NEG)
        _flash_init(c, s, v_own, m_ref, l_ref, acc_ref)

    def body(j, carry):
        start = pl.multiple_of(j * tq, tq)
        v = v_ref[0, pl.ds(start, tq), :]
        for c in range(2):
            _flash_update(c, scores(c, start), v, m_ref, l_ref, acc_ref)
        return carry

    lax.fori_loop(0, i, body, 0)
    o = _flash_result(0, l_ref, acc_ref) - lam * _flash_result(1, l_ref, acc_ref)
    o_ref[0] = (_rms(o, g_ref[...]) * (1.0 - lam_init)).astype(o_ref.dtype)


def diff_attention(proj, lam_params, subln_g, lam_init, n_heads, q_col, k_col, v_col, tq=256):
    b, s, _ = proj.shape
    tq = min(tq, s)
    w = 2 * HEAD_DIM
    qo, ko, vo = q_col // w, k_col // w, v_col // w
    return pl.pallas_call(
        functools.partial(_diff_kernel, tq=tq, lam_init=lam_init),
        grid=(b, n_heads, s // tq),
        in_specs=[pl.BlockSpec((4, HEAD_DIM), lambda bi, h, i: (0, 0)),
                  pl.BlockSpec((1, w), lambda bi, h, i: (0, 0)),
                  pl.BlockSpec((1, tq, w), lambda bi, h, i: (bi, i, qo + h)),
                  pl.BlockSpec((1, s, w), lambda bi, h, i: (bi, 0, ko + h)),
                  pl.BlockSpec((1, s, w), lambda bi, h, i: (bi, 0, vo + h))],
        out_specs=pl.BlockSpec((1, tq, w), lambda bi, h, i: (bi, i, h)),
        out_shape=jax.ShapeDtypeStruct((b, s, n_heads * w), BF16),
        scratch_shapes=[pltpu.VMEM((2, tq, 1), F32), pltpu.VMEM((2, tq, 1), F32), pltpu.VMEM((2, tq, w), F32)],
        compiler_params=_params("parallel", "parallel", "arbitrary"),
    )(lam_params, subln_g.reshape(1, w), proj, proj, proj)


def _moba_kernel(q_ref, k_ref, v_ref, o_ref, kmean_ref, m_ref, l_ref, acc_ref, *, nb):
    i = pl.program_id(2)
    blk = MOBA_BLOCK

    @pl.when(i == 0)
    def _():
        kmean_ref[...] = jnp.zeros(kmean_ref.shape, F32)
        for n in range(nb):
            kb = k_ref[0, n * blk:(n + 1) * blk, :].astype(F32)
            kmean_ref[n:n + 1, :] = jnp.mean(kb, axis=0, keepdims=True)

    q = q_ref[0]
    km_hi, km_lo = _split_bf16(kmean_ref[...])
    gate = _dot_nt(q, km_hi) + _dot_nt(q, km_lo)
    lane = lax.broadcasted_iota(jnp.int32, gate.shape, 1)
    past = lane < i
    rank = _rank_before(jnp.where(past, gate, NEG), nb)
    block_bias = jnp.where(past & (rank < MOBA_TOPK), 0.0, NEG)

    rows = lax.broadcasted_iota(jnp.int32, (blk, blk), 0)
    cols = lax.broadcasted_iota(jnp.int32, (blk, blk), 1)
    own = pl.multiple_of(i * blk, blk)
    s = _dot_nt(q, k_ref[0, pl.ds(own, blk), :]) * ATTN_SCALE
    _flash_init(0, jnp.where(cols <= rows, s, NEG), v_ref[0, pl.ds(own, blk), :], m_ref, l_ref, acc_ref)

    for n in range(nb - 1):
        @pl.when(n < i)
        def _():
            sn = _dot_nt(q, k_ref[0, n * blk:(n + 1) * blk, :]) * ATTN_SCALE + block_bias[:, n:n + 1]
            _flash_update(0, sn, v_ref[0, n * blk:(n + 1) * blk, :], m_ref, l_ref, acc_ref)

    o_ref[0] = _flash_result(0, l_ref, acc_ref).astype(o_ref.dtype)


def moba_attention(proj, n_heads, q_col, k_col, v_col):
    b, s, _ = proj.shape
    blk = MOBA_BLOCK
    assert s % blk == 0 and s // blk <= LANES
    nb = s // blk
    qo, ko, vo = q_col // HEAD_DIM, k_col // HEAD_DIM, v_col // HEAD_DIM
    return pl.pallas_call(
        functools.partial(_moba_kernel, nb=nb),
        grid=(b, n_heads, nb),
        in_specs=[pl.BlockSpec((1, blk, HEAD_DIM), lambda bi, h, i: (bi, i, qo + h)),
                  pl.BlockSpec((1, s, HEAD_DIM), lambda bi, h, i: (bi, 0, ko + h)),
                  pl.BlockSpec((1, s, HEAD_DIM), lambda bi, h, i: (bi, 0, vo + h))],
        out_specs=pl.BlockSpec((1, blk, HEAD_DIM), lambda bi, h, i: (bi, i, h)),
        out_shape=jax.ShapeDtypeStruct((b, s, n_heads * HEAD_DIM), BF16),
        scratch_shapes=[pltpu.VMEM((LANES, HEAD_DIM), F32), pltpu.VMEM((1, blk, 1), F32),
                        pltpu.VMEM((1, blk, 1), F32), pltpu.VMEM((1, blk, HEAD_DIM), F32)],
        compiler_params=_params("parallel", "parallel", "arbitrary"),
    )(proj, proj, proj)


def _compress_kernel(t_ref, pe_ref, w1_ref, w2_ref, o_ref):
    t = t_ref[0, 0].astype(F32)
    half = t.shape[1]
    top = (t + pe_ref[0, :, :half]).astype(BF16)
    bot = (t + pe_ref[0, :, half:]).astype(BF16)
    a = _dot(top, w1_ref[0, :half, :])
    bm = _dot(bot, w1_ref[0, half:, :])
    hidden = a + pltpu.roll(bm, bm.shape[0] - 1, 0)
    act = hidden * jax.nn.sigmoid(hidden)
    o_ref[0, 0] = _dot(act.astype(BF16), w2_ref[0]).astype(o_ref.dtype)


def nsa_compress(t16, pe_flat, w1, w2):
    b, g2, nc, width = t16.shape
    g = g2 // 2
    hid = w1.shape[-1]
    return pl.pallas_call(
        _compress_kernel,
        grid=(b, g2),
        in_specs=[pl.BlockSpec((1, 1, nc, width), lambda bi, j: (bi, j, 0, 0)),
                  pl.BlockSpec((1, 1, 2 * width), lambda bi, j: (j // g, 0, 0)),
                  pl.BlockSpec((1, 2 * width, hid), lambda bi, j: (j // g, 0, 0)),
                  pl.BlockSpec((1, hid, HEAD_DIM), lambda bi, j: (j // g, 0, 0))],
        out_specs=pl.BlockSpec((1, 1, nc, HEAD_DIM), lambda bi, j: (bi, j, 0, 0)),
        out_shape=jax.ShapeDtypeStruct((b, g2, nc, HEAD_DIM), BF16),
        compiler_params=_params("parallel", "parallel"),
    )(t16, pe_flat, w1, w2)


def _nsa_kernel(q_ref, ks_ref, vs_ref, kw_ref, vw_ref, kc_ref, vc_ref, gt_ref, cover_ref, expand_ref,
                o_ref, bias_ref, m_ref, l_ref, acc_ref, *, tq, seq, n_sel):
    i = pl.program_id(2)
    z_heads = NSA_GROUP
    nc = kc_ref.shape[2]
    q = jnp.concatenate([q_ref[0, :, z * HEAD_DIM:(z + 1) * HEAD_DIM] for z in range(z_heads)], axis=0)
    qpos = i * tq + lax.broadcasted_iota(jnp.int32, (tq, 1), 0)

    cmp_end = lax.broadcasted_iota(jnp.int32, (tq, nc), 1) * NSA_CMP_STRIDE + (NSA_CMP_LEN - 1)
    cmp_ok_t = jnp.where(cmp_end <= qpos, 1.0, 0.0)
    cmp_ok = jnp.concatenate([cmp_ok_t] * z_heads, axis=0)
    sc = jnp.where(cmp_ok > 0.5, _dot_nt(q, kc_ref[0, 0]) * ATTN_SCALE, NEG)
    pc = jnp.exp(sc - jnp.max(sc, axis=-1, keepdims=True))
    pc = pc * (1.0 / jnp.sum(pc, axis=-1, keepdims=True)) * cmp_ok
    o_cmp = _dot(pc.astype(BF16), vc_ref[0, 0])

    p_sum = pc[0:tq]
    for z in range(1, z_heads):
        p_sum = p_sum + pc[z * tq:(z + 1) * tq]
    p_hi, p_lo = _split_bf16(p_sum)
    imp = _dot(p_hi, cover_ref[...]) + _dot(p_lo, cover_ref[...])
    lane = lax.broadcasted_iota(jnp.int32, imp.shape, 1)
    qblk = qpos // NSA_SEL_BLOCK
    forced = (lane == 0) | (lane == qblk) | (lane == qblk - 1)
    allowed = lane <= qblk
    imp = jnp.where(forced, BIG, jnp.where(allowed, imp, NEG))
    rank = _rank_before(imp, n_sel)
    chosen = jnp.where(allowed & (rank < NSA_SEL_TOPN), 1.0, 0.0).astype(BF16)
    chunk = 512 if seq % 512 == 0 else seq
    for c0 in range(0, seq, chunk):
        @pl.when(c0 < (i + 1) * tq)
        def _():
            vis = _dot(chosen, expand_ref[:, c0:c0 + chunk])
            kpos = c0 + lax.broadcasted_iota(jnp.int32, (tq, chunk), 1)
            bias_ref[:, c0:c0 + chunk] = jnp.where((vis > 0.5) & (kpos <= qpos), 0.0, NEG)

    def sel_scores(start):
        b1 = bias_ref[:, pl.ds(start, tq)]
        return _dot_nt(q, ks_ref[0, pl.ds(start, tq), :]) * ATTN_SCALE + jnp.concatenate([b1] * z_heads, axis=0)

    _flash_init(0, sel_scores(0), vs_ref[0, 0:tq, :], m_ref, l_ref, acc_ref)

    def sel_body(j, carry):
        start = pl.multiple_of(j * tq, tq)
        _flash_update(0, sel_scores(start), vs_ref[0, pl.ds(start, tq), :], m_ref, l_ref, acc_ref)
        return carry

    lax.fori_loop(1, i + 1, sel_body, 0)

    r_idx = lax.broadcasted_iota(jnp.int32, (tq, tq), 0)
    c_idx = lax.broadcasted_iota(jnp.int32, (tq, tq), 1)

    def win_scores(t):
        start = pl.multiple_of((i - t) * tq, tq)
        dist = r_idx - c_idx + t * tq
        b1 = jnp.where((dist >= 0) & (dist < NSA_WINDOW), 0.0, NEG)
        s = _dot_nt(q, kw_ref[0, pl.ds(start, tq), :]) * ATTN_SCALE + jnp.concatenate([b1] * z_heads, axis=0)
        return s, vw_ref[0, pl.ds(start, tq), :]

    s0, v0 = win_scores(0)
    _flash_init(1, s0, v0, m_ref, l_ref, acc_ref)

    def win_body(t, carry):
        st, vt = win_scores(t)
        _flash_update(1, st, vt, m_ref, l_ref, acc_ref)
        return carry

    lax.fori_loop(1, jnp.minimum(i, NSA_WINDOW // tq) + 1, win_body, 0)

    o_sel = _flash_result(0, l_ref, acc_ref)
    o_win = _flash_result(1, l_ref, acc_ref)
    gt = jax.nn.sigmoid(gt_ref[0])
    for z in range(z_heads):
        rs = slice(z * tq, (z + 1) * tq)
        o = (gt[:, 3 * z:3 * z + 1] * o_cmp[rs] + gt[:, 3 * z + 1:3 * z + 2] * o_sel[rs]
             + gt[:, 3 * z + 2:3 * z + 3] * o_win[rs])
        o_ref[0, :, z * HEAD_DIM:(z + 1) * HEAD_DIM] = o.astype(o_ref.dtype)


def nsa_attention(proj, gates, cmp_kv, n_groups, q_col, ks_col, vs_col, kw_col, vw_col, tq=128):
    b, s, _ = proj.shape
    tq = min(tq, s)
    assert s % tq == 0 and NSA_WINDOW % tq == 0 and tq % NSA_SEL_BLOCK == 0
    nc = cmp_kv.shape[2]
    n_sel = s // NSA_SEL_BLOCK
    assert nc * NSA_CMP_STRIDE == s and n_sel <= LANES
    c_lo = np.arange(nc) * NSA_CMP_STRIDE
    j_lo = np.arange(LANES) * NSA_SEL_BLOCK
    cover = ((c_lo[:, None] < j_lo[None, :] + NSA_SEL_BLOCK) & (c_lo[:, None] + NSA_CMP_LEN > j_lo[None, :])
             & (np.arange(LANES)[None, :] < n_sel) & (c_lo[:, None] + NSA_CMP_LEN <= s))
    expand = (np.arange(s)[None, :] // NSA_SEL_BLOCK == np.arange(LANES)[:, None])
    gw = NSA_GROUP * HEAD_DIM
    qo = q_col // gw
    kso, vso, kwo, vwo = (c // HEAD_DIM for c in (ks_col, vs_col, kw_col, vw_col))
    kv = lambda off: pl.BlockSpec((1, s, HEAD_DIM), lambda bi, g, i: (bi, 0, off + g))
    return pl.pallas_call(
        functools.partial(_nsa_kernel, tq=tq, seq=s, n_sel=n_sel),
        grid=(b, n_groups, s // tq),
        in_specs=[pl.BlockSpec((1, tq, gw), lambda bi, g, i: (bi, i, qo + g)),
                  kv(kso), kv(vso), kv(kwo), kv(vwo),
                  pl.BlockSpec((1, 1, nc, HEAD_DIM), lambda bi, g, i: (bi, g, 0, 0)),
                  pl.BlockSpec((1, 1, nc, HEAD_DIM), lambda bi, g, i: (bi, n_groups + g, 0, 0)),
                  pl.BlockSpec((1, tq, NSA_GATE_PAD), lambda bi, g, i: (bi, i, g)),
                  pl.BlockSpec((nc, LANES), lambda bi, g, i: (0, 0)),
                  pl.BlockSpec((LANES, s), lambda bi, g, i: (0, 0))],
        out_specs=pl.BlockSpec((1, tq, gw), lambda bi, g, i: (bi, i, g)),
        out_shape=jax.ShapeDtypeStruct((b, s, n_groups * gw), BF16),
        scratch_shapes=[pltpu.VMEM((tq, s), F32), pltpu.VMEM((2, NSA_GROUP * tq, 1), F32),
                        pltpu.VMEM((2, NSA_GROUP * tq, 1), F32), pltpu.VMEM((2, NSA_GROUP * tq, HEAD_DIM), F32)],
        compiler_params=_params("parallel", "parallel", "arbitrary"),
    )(proj, proj, proj, proj, proj, cmp_kv, cmp_kv, gates, jnp.asarray(cover, BF16), jnp.asarray(expand, BF16))


def _even_rope_pred(tn, d_model):
    seg_tiles = (d_model // 2) // tn
    return lambda j: (j // seg_tiles) % 3 != 2


def even_mixer(hn, w_in, w_out, lam_params, subln_g, lam_init, tables, batch, seq):
    d_model = hn.shape[1]
    seg = d_model // 2
    tn = min(1024, seg)
    proj = proj_rope(hn, w_in.astype(BF16), tables, seq, _even_rope_pred(tn, d_model), tn=tn)
    proj = proj.reshape(batch, seq, -1)
    o_diff = diff_attention(proj, lam_params, subln_g, lam_init, seg // (2 * HEAD_DIM), 0, seg, 2 * seg)
    o_moba = moba_attention(proj, seg // HEAD_DIM, 3 * seg, 4 * seg, 5 * seg)
    o = jnp.concatenate([o_diff, o_moba], axis=-1).reshape(batch * seq, d_model)
    return matmul(o, w_out.astype(BF16), F32)


def _odd_rope_pred(tn, d_model):
    slab_tiles = (d_model // 8) // tn
    q_tiles = d_model // tn
    return lambda j: (j < q_tiles) | (((j - q_tiles) // slab_tiles) % 2 == 0)


def odd_mixer(hn, w_in, w_out, cmp_pe, cmp_w1, cmp_w2, tables, batch, seq):
    d_model = hn.shape[1]
    g = d_model // (NSA_GROUP * HEAD_DIM)
    kvd = g * HEAD_DIM
    n_main = d_model + 6 * kvd
    tn = min(512, kvd)
    w_in = w_in.astype(BF16)
    proj = proj_rope(hn, w_in[:, :n_main], tables, seq, _odd_rope_pred(tn, d_model), tn=tn)
    proj = proj.reshape(batch, seq, n_main)
    n_gate = 3 * NSA_GROUP
    w_gate = w_in[:, n_main:].reshape(d_model, g, n_gate)
    w_gate = jnp.pad(w_gate, ((0, 0), (0, 0), (0, NSA_GATE_PAD - n_gate))).reshape(d_model, g * NSA_GATE_PAD)
    gates = matmul(hn, w_gate, F32).reshape(batch, seq, g * NSA_GATE_PAD)
    nc = seq // NSA_CMP_STRIDE
    t16 = proj[:, :, d_model:d_model + 2 * kvd].reshape(batch, seq, 2 * g, HEAD_DIM)
    t16 = t16.transpose(0, 2, 1, 3).reshape(batch, 2 * g, nc, NSA_CMP_STRIDE * HEAD_DIM)
    cmp_kv = nsa_compress(t16, cmp_pe.reshape(2, 1, NSA_CMP_LEN * HEAD_DIM), cmp_w1.astype(BF16),
                          cmp_w2.astype(BF16))
    o = nsa_attention(proj, gates, cmp_kv, g, 0, d_model + 2 * kvd, d_model + 3 * kvd, d_model + 4 * kvd,
                      d_model + 5 * kvd)
    return matmul(o.reshape(batch * seq, d_model), w_out.astype(BF16), F32)


def ffn(hn, w_gate, w_up, w_down):
    a = swiglu_up(hn, w_gate.astype(BF16), w_up.astype(BF16))
    return matmul(a, w_down.astype(BF16), F32, k_splits=2)


def kernel(x, norm_g, ffn_a_gate, ffn_a_up, ffn_a_down, ffn_b_gate, ffn_b_up, ffn_b_down, ev_w_in, ev_w_out, ev_lambda, ev_subln_g, od_w_in, od_w_out, od_cmp_pe, od_cmp_w1, od_cmp_w2):
    batch, seq, d_model = x.shape
    depth = norm_g.shape[0]
    tables = rope_tables(seq)
    h = x.reshape(batch * seq, d_model)
    hn = rmsnorm(h, norm_g[0, 0])
    for layer in range(depth):
        g = norm_g[layer]
        m = ffn(hn, ffn_a_gate[layer], ffn_a_up[layer], ffn_a_down[layer])
        h, hn = residual_norm(h, m, g[1], g[2], 0.5)
        if layer % 2 == 0:
            e = layer // 2
            lam_init = 0.8 - 0.6 * math.exp(-0.3 * layer)
            m = even_mixer(hn, ev_w_in[e], ev_w_out[e], ev_lambda[e], ev_subln_g[e], lam_init, tables, batch, seq)
        else:
            o = layer // 2
            m = odd_mixer(hn, od_w_in[o], od_w_out[o], od_cmp_pe[o], od_cmp_w1[o], od_cmp_w2[o], tables, batch, seq)
        h, hn = residual_norm(h, m, g[3], g[4], 1.0)
        m = ffn(hn, ffn_b_gate[layer], ffn_b_up[layer], ffn_b_down[layer])
        g_next = norm_g[layer + 1, 0] if layer + 1 < depth else None
        h, hn = residual_norm(h, m, g[5], g_next, 0.5)
    return h.reshape(batch, seq, d_model)
```

```python
import functools
import math

import numpy as np
import jax
import jax.numpy as jnp
from jax import lax
from jax.experimental import pallas as pl
from jax.experimental.pallas import tpu as pltpu

F32 = jnp.float32
BF16 = jnp.bfloat16

HEAD_DIM = 128
ROT_DIM = HEAD_DIM // 4
ROPE_THETA = 500000.0
RMS_EPS = 1e-6
NEG = -1e30
BIG = 1e30
Q_SCALE = HEAD_DIM ** -0.5 * math.log2(math.e)

DIFF_V_DIM = 2 * HEAD_DIM
MOBA_BLOCK = 256
MOBA_TOPK = 3
NSA_GROUP = 8
NSA_CMP_LEN = 32
NSA_CMP_STRIDE = 16
NSA_SEL_BLOCK = 64
NSA_SEL_TOPN = 16
NSA_WINDOW = 512
NSA_GATE_PAD = 128

LANES = 128
SUBLANES = 8
VMEM_LIMIT_BYTES = 56 * 1024 * 1024


def _params(*semantics):
    return pltpu.CompilerParams(dimension_semantics=semantics, vmem_limit_bytes=VMEM_LIMIT_BYTES)


def _dot(a, b):
    return jnp.dot(a, b, preferred_element_type=F32)


def _dot_nt(a, b):
    return lax.dot_general(a, b, (((1,), (1,)), ((), ())), preferred_element_type=F32)


def _dot_tn(a, b):
    return lax.dot_general(a, b, (((0,), (0,)), ((), ())), preferred_element_type=F32)


def _split_bf16(x):
    hi = x.astype(BF16)
    lo = (x - hi.astype(F32)).astype(BF16)
    return hi, lo


def _rms(x, g):
    return x * lax.rsqrt(jnp.mean(x * x, axis=-1, keepdims=True) + RMS_EPS) * g


def _rmsnorm_kernel(x_ref, g_ref, o_ref):
    o_ref[...] = _rms(x_ref[...], g_ref[...]).astype(o_ref.dtype)


def rmsnorm(x, g, tm=256):
    m, d = x.shape
    tm = min(tm, m)
    return pl.pallas_call(
        _rmsnorm_kernel,
        grid=(m // tm,),
        in_specs=[pl.BlockSpec((tm, d), lambda i: (i, 0)), pl.BlockSpec((1, d), lambda i: (0, 0))],
        out_specs=pl.BlockSpec((tm, d), lambda i: (i, 0)),
        out_shape=jax.ShapeDtypeStruct((m, d), BF16),
        compiler_params=_params("parallel"),
    )(x, g.reshape(1, d))


def _residual_kernel(h_ref, m_ref, gpost_ref, gnext_ref, h_out_ref, hn_out_ref, *, coef):
    y = _rms(m_ref[...], gpost_ref[...])
    if coef != 1.0:
        y = coef * y
    h = h_ref[...] + y
    h_out_ref[...] = h
    hn_out_ref[...] = _rms(h, gnext_ref[...]).astype(hn_out_ref.dtype)


def _residual_last_kernel(h_ref, m_ref, gpost_ref, h_out_ref, *, coef):
    y = _rms(m_ref[...], gpost_ref[...])
    if coef != 1.0:
        y = coef * y
    h_out_ref[...] = h_ref[...] + y


def residual_norm(h, mix, g_post, g_next, coef, tm=256):
    m, d = h.shape
    tm = min(tm, m)
    row = pl.BlockSpec((tm, d), lambda i: (i, 0))
    vec = pl.BlockSpec((1, d), lambda i: (0, 0))
    if g_next is None:
        return pl.pallas_call(
            functools.partial(_residual_last_kernel, coef=coef),
            grid=(m // tm,),
            in_specs=[row, row, vec],
            out_specs=row,
            out_shape=jax.ShapeDtypeStruct((m, d), F32),
            compiler_params=_params("parallel"),
        )(h, mix, g_post.reshape(1, d)), None
    return pl.pallas_call(
        functools.partial(_residual_kernel, coef=coef),
        grid=(m // tm,),
        in_specs=[row, row, vec, vec],
        out_specs=[row, row],
        out_shape=[jax.ShapeDtypeStruct((m, d), F32), jax.ShapeDtypeStruct((m, d), BF16)],
        compiler_params=_params("parallel"),
    )(h, mix, g_post.reshape(1, d), g_next.reshape(1, d))


def _mm_kernel(x_ref, w_ref, o_ref):
    o_ref[...] = _dot(x_ref[...], w_ref[...].astype(BF16)).astype(o_ref.dtype)


def _mm_ksplit_kernel(x_ref, w_ref, o_ref):
    k = pl.program_id(2)

    @pl.when(k == 0)
    def _():
        o_ref[...] = _dot(x_ref[...], w_ref[...].astype(BF16))

    @pl.when(k > 0)
    def _():
        o_ref[...] += _dot(x_ref[...], w_ref[...].astype(BF16))


def matmul(x, w, layer, out_dtype, tm=1024, tn=512, k_splits=1):
    m, kd = x.shape
    n = w.shape[2]
    tm, tn = min(tm, m), min(tn, n)
    if k_splits == 1:
        return pl.pallas_call(
            _mm_kernel,
            grid=(m // tm, n // tn),
            in_specs=[pl.BlockSpec((tm, kd), lambda i, j: (i, 0)),
                      pl.BlockSpec((None, kd, tn), lambda i, j: (layer, 0, j))],
            out_specs=pl.BlockSpec((tm, tn), lambda i, j: (i, j)),
            out_shape=jax.ShapeDtypeStruct((m, n), out_dtype),
            compiler_params=_params("parallel", "parallel"),
        )(x, w)
    assert out_dtype == F32 and kd % k_splits == 0
    tk = kd // k_splits
    return pl.pallas_call(
        _mm_ksplit_kernel,
        grid=(m // tm, n // tn, k_splits),
        in_specs=[pl.BlockSpec((tm, tk), lambda i, j, k: (i, k)),
                  pl.BlockSpec((None, tk, tn), lambda i, j, k: (layer, k, j))],
        out_specs=pl.BlockSpec((tm, tn), lambda i, j, k: (i, j)),
        out_shape=jax.ShapeDtypeStruct((m, n), F32),
        compiler_params=_params("parallel", "parallel", "arbitrary"),
    )(x, w)


def _mm_nt_kernel(w_ref, x_ref, o_ref):
    o_ref[...] = _dot_nt(w_ref[...].astype(BF16), x_ref[...])


def matmul_nt(w_t, x, tm=1024):
    n, kd = w_t.shape
    m = x.shape[0]
    tm = min(tm, m)
    return pl.pallas_call(
        _mm_nt_kernel,
        grid=(m // tm,),
        in_specs=[pl.BlockSpec((n, kd), lambda i: (0, 0)), pl.BlockSpec((tm, kd), lambda i: (i, 0))],
        out_specs=pl.BlockSpec((n, tm), lambda i: (0, i)),
        out_shape=jax.ShapeDtypeStruct((n, m), F32),
        compiler_params=_params("parallel"),
    )(w_t, x)


def _swiglu_kernel(x_ref, wg_ref, wu_ref, o_ref):
    x = x_ref[...]
    g = _dot(x, wg_ref[...].astype(BF16))
    u = _dot(x, wu_ref[...].astype(BF16))
    o_ref[...] = (g * jax.nn.sigmoid(g) * u).astype(o_ref.dtype)


def swiglu_up(x, w_gate, w_up, layer, tm=1024, tn=256):
    m, kd = x.shape
    n = w_gate.shape[2]
    tm, tn = min(tm, m), min(tn, n)
    wspec = pl.BlockSpec((None, kd, tn), lambda i, j: (layer, 0, j))
    return pl.pallas_call(
        _swiglu_kernel,
        grid=(m // tm, n // tn),
        in_specs=[pl.BlockSpec((tm, kd), lambda i, j: (i, 0)), wspec, wspec],
        out_specs=pl.BlockSpec((tm, tn), lambda i, j: (i, j)),
        out_shape=jax.ShapeDtypeStruct((m, n), BF16),
        compiler_params=_params("parallel", "parallel"),
    )(x, w_gate, w_up)


def rope_tables(seq):
    half = ROT_DIM // 2
    pos = jnp.arange(seq, dtype=F32)
    inv = ROPE_THETA ** (-jnp.arange(0, ROT_DIM, 2, dtype=F32) / ROT_DIM)
    ang = pos[:, None] * inv[None, :]
    cos, sin = jnp.cos(ang), jnp.sin(ang)
    zeros = jnp.zeros((seq, HEAD_DIM - ROT_DIM), F32)
    zhalf = jnp.zeros((seq, half), F32)
    c = jnp.concatenate([cos, cos, zeros + 1.0], axis=-1)
    s1 = jnp.concatenate([zhalf, sin, zeros], axis=-1)
    s2 = jnp.concatenate([-sin, zhalf, zeros], axis=-1)
    return c, s1, s2


PLAIN, ROPE, ROPE_SCALED = 0, 1, 2


def _proj_rope_kernel(x_ref, w_ref, c_ref, s1_ref, s2_ref, o_ref, *, tile_mode):
    acc = _dot(x_ref[...], w_ref[...].astype(BF16))
    mode = tile_mode(pl.program_id(1))
    half = ROT_DIM // 2

    def rope_store(scale):
        c, s1, s2 = c_ref[...], s1_ref[...], s2_ref[...]
        for h in range(acc.shape[1] // HEAD_DIM):
            xh = acc[:, h * HEAD_DIM:(h + 1) * HEAD_DIM]
            r = xh * c + pltpu.roll(xh, half, 1) * s1 + pltpu.roll(xh, HEAD_DIM - half, 1) * s2
            if scale != 1.0:
                r = r * scale
            o_ref[:, h * HEAD_DIM:(h + 1) * HEAD_DIM] = r.astype(o_ref.dtype)

    @pl.when(mode == ROPE_SCALED)
    def _():
        rope_store(Q_SCALE)

    @pl.when(mode == ROPE)
    def _():
        rope_store(1.0)

    @pl.when(mode == PLAIN)
    def _():
        o_ref[...] = acc.astype(o_ref.dtype)


def proj_rope(x, w, layer, n, tables, seq, tile_mode, tm=1024, tn=512):
    m, kd = x.shape
    tm, tn = min(tm, seq), min(tn, n)
    assert seq % tm == 0 and m % tm == 0 and n % tn == 0
    pos_blocks = seq // tm
    tspec = pl.BlockSpec((tm, HEAD_DIM), lambda i, j: (i % pos_blocks, 0))
    return pl.pallas_call(
        functools.partial(_proj_rope_kernel, tile_mode=tile_mode),
        grid=(m // tm, n // tn),
        in_specs=[pl.BlockSpec((tm, kd), lambda i, j: (i, 0)),
                  pl.BlockSpec((None, kd, tn), lambda i, j: (layer, 0, j)),
                  tspec, tspec, tspec],
        out_specs=pl.BlockSpec((tm, tn), lambda i, j: (i, j)),
        out_shape=jax.ShapeDtypeStruct((m, n), BF16),
        compiler_params=_params("parallel", "parallel"),
    )(x, w, *tables)


def _flash_init(slot, s_t, v, m_ref, l_ref, acc_ref):
    m = jnp.max(s_t, axis=0, keepdims=True)
    p = jnp.exp2(s_t - m)
    m_ref[slot] = m
    l_ref[slot] = jnp.sum(p, axis=0, keepdims=True)
    acc_ref[slot] = _dot_tn(v, p.astype(v.dtype))


def _flash_update(slot, s_t, v, m_ref, l_ref, acc_ref):
    m_old = m_ref[slot]
    m_new = jnp.maximum(m_old, jnp.max(s_t, axis=0, keepdims=True))
    alpha = jnp.exp2(m_old - m_new)
    p = jnp.exp2(s_t - m_new)
    m_ref[slot] = m_new
    l_ref[slot] = alpha * l_ref[slot] + jnp.sum(p, axis=0, keepdims=True)
    acc_ref[slot] = alpha * acc_ref[slot] + _dot_tn(v, p.astype(v.dtype))


def _flash_result(slot, l_ref, acc_ref):
    return acc_ref[slot] * (1.0 / l_ref[slot])


def _softmax_pv(s_t, v):
    p = jnp.exp2(s_t - jnp.max(s_t, axis=0, keepdims=True))
    p = p * (1.0 / jnp.sum(p, axis=0, keepdims=True))
    return _dot_tn(v, p.astype(v.dtype)), p


def _pad_rows(n_rows):
    return -(-n_rows // SUBLANES) * SUBLANES


def _rank_before(v, n_rows):
    row = lax.broadcasted_iota(jnp.int32, v.shape, 0)
    rank = jnp.zeros(v.shape, jnp.int32)
    for j2 in range(n_rows):
        other = v[j2:j2 + 1, :]
        rank = rank + jnp.where(other > v, 1, jnp.where(other == v, (row > j2).astype(jnp.int32), 0))
    return rank


def _diff_kernel(lam_ref, g_ref, q_ref, k_ref, v_ref, o_ref, m_ref, l_ref, acc_ref, *, tq, hp, lam_init):
    i = pl.program_id(2)
    w = DIFF_V_DIM
    lp = lam_ref[...]
    lam = (jnp.exp(jnp.sum(lp[0:1] * lp[1:2], axis=-1, keepdims=True))
           - jnp.exp(jnp.sum(lp[2:3] * lp[3:4], axis=-1, keepdims=True)) + lam_init)
    key = lax.broadcasted_iota(jnp.int32, (tq, tq), 0)
    qry = lax.broadcasted_iota(jnp.int32, (tq, tq), 1)

    def scores(c, start):
        cols = slice(c * HEAD_DIM, (c + 1) * HEAD_DIM)
        return _dot_nt(k_ref[0, pl.ds(start, tq), cols], q_ref[0, :, cols])

    own = pl.multiple_of(i * tq, tq)
    for c in range(2 * hp):
        v_own = v_ref[0, pl.ds(own, tq), (c // 2) * w:(c // 2 + 1) * w]
        _flash_init(c, jnp.where(key <= qry, scores(c, own), NEG), v_own, m_ref, l_ref, acc_ref)

    def body(j, carry):
        start = pl.multiple_of(j * tq, tq)
        for c in range(2 * hp):
            v = v_ref[0, pl.ds(start, tq), (c // 2) * w:(c // 2 + 1) * w]
            _flash_update(c, scores(c, start), v, m_ref, l_ref, acc_ref)
        return carry

    lax.fori_loop(0, i, body, 0)
    for h in range(hp):
        o = (_flash_result(2 * h, l_ref, acc_ref) - lam * _flash_result(2 * h + 1, l_ref, acc_ref)).T
        o_ref[0, :, h * w:(h + 1) * w] = (_rms(o, g_ref[...]) * (1.0 - lam_init)).astype(o_ref.dtype)


def diff_attention(proj, lam_params, subln_g, lam_init, n_heads, q_col, k_col, v_col, tq=256, hp=2):
    b, s, _ = proj.shape
    tq = min(tq, s)
    w = DIFF_V_DIM
    assert n_heads % hp == 0
    gw = hp * w
    qo, ko, vo = q_col // gw, k_col // gw, v_col // gw
    return pl.pallas_call(
        functools.partial(_diff_kernel, tq=tq, hp=hp, lam_init=lam_init),
        grid=(b, n_heads // hp, s // tq),
        in_specs=[pl.BlockSpec((4, HEAD_DIM), lambda bi, h, i: (0, 0)),
                  pl.BlockSpec((1, w), lambda bi, h, i: (0, 0)),
                  pl.BlockSpec((1, tq, gw), lambda bi, h, i: (bi, i, qo + h)),
                  pl.BlockSpec((1, s, gw), lambda bi, h, i: (bi, 0, ko + h)),
                  pl.BlockSpec((1, s, gw), lambda bi, h, i: (bi, 0, vo + h))],
        out_specs=pl.BlockSpec((1, tq, gw), lambda bi, h, i: (bi, i, h)),
        out_shape=jax.ShapeDtypeStruct((b, s, n_heads * w), BF16),
        scratch_shapes=[pltpu.VMEM((2 * hp, 1, tq), F32), pltpu.VMEM((2 * hp, 1, tq), F32),
                        pltpu.VMEM((2 * hp, w, tq), F32)],
        compiler_params=_params("parallel", "parallel", "arbitrary"),
    )(lam_params, subln_g.reshape(1, w), proj, proj, proj)


def _moba_kernel(q_ref, k_ref, v_ref, o_ref, kmean_ref, bias_ref, m_ref, l_ref, acc_ref, *, nb, hp):
    i = pl.program_id(2)
    blk = MOBA_BLOCK
    head = lambda h: slice(h * HEAD_DIM, (h + 1) * HEAD_DIM)

    @pl.when(i == 0)
    def _():
        kmean_ref[...] = jnp.zeros(kmean_ref.shape, F32)
        for h in range(hp):
            for n in range(nb):
                kb = k_ref[0, n * blk:(n + 1) * blk, head(h)].astype(F32)
                kmean_ref[h, n:n + 1, :] = jnp.mean(kb, axis=0, keepdims=True)

    key = lax.broadcasted_iota(jnp.int32, (blk, blk), 0)
    qry = lax.broadcasted_iota(jnp.int32, (blk, blk), 1)
    own = pl.multiple_of(i * blk, blk)
    for h in range(hp):
        q = q_ref[0, :, head(h)]
        km_hi, km_lo = _split_bf16(kmean_ref[h])
        gate = (_dot_nt(km_hi, q) + _dot_nt(km_lo, q))[:_pad_rows(nb)]
        past = lax.broadcasted_iota(jnp.int32, gate.shape, 0) < i
        rank = _rank_before(jnp.where(past, gate, NEG), nb)
        bias_ref[h] = jnp.where(past & (rank < MOBA_TOPK), 0.0, NEG)
        s = _dot_nt(k_ref[0, pl.ds(own, blk), head(h)], q)
        _flash_init(h, jnp.where(key <= qry, s, NEG), v_ref[0, pl.ds(own, blk), head(h)], m_ref, l_ref, acc_ref)

    def body(n, carry):
        start = pl.multiple_of(n * blk, blk)
        for h in range(hp):
            sn = _dot_nt(k_ref[0, pl.ds(start, blk), head(h)], q_ref[0, :, head(h)]) + bias_ref[h, pl.ds(n, 1), :]
            _flash_update(h, sn, v_ref[0, pl.ds(start, blk), head(h)], m_ref, l_ref, acc_ref)
        return carry

    lax.fori_loop(0, i, body, 0)
    for h in range(hp):
        o_ref[0, :, head(h)] = _flash_result(h, l_ref, acc_ref).T.astype(o_ref.dtype)


def moba_attention(proj, n_heads, q_col, k_col, v_col, hp=4):
    b, s, _ = proj.shape
    blk = MOBA_BLOCK
    hp = min(hp, n_heads)
    assert s % blk == 0 and s // blk <= LANES and n_heads % hp == 0
    nb = s // blk
    gw = hp * HEAD_DIM
    qo, ko, vo = q_col // gw, k_col // gw, v_col // gw
    return pl.pallas_call(
        functools.partial(_moba_kernel, nb=nb, hp=hp),
        grid=(b, n_heads // hp, nb),
        in_specs=[pl.BlockSpec((1, blk, gw), lambda bi, h, i: (bi, i, qo + h)),
                  pl.BlockSpec((1, s, gw), lambda bi, h, i: (bi, 0, ko + h)),
                  pl.BlockSpec((1, s, gw), lambda bi, h, i: (bi, 0, vo + h))],
        out_specs=pl.BlockSpec((1, blk, gw), lambda bi, h, i: (bi, i, h)),
        out_shape=jax.ShapeDtypeStruct((b, s, n_heads * HEAD_DIM), BF16),
        scratch_shapes=[pltpu.VMEM((hp, LANES, HEAD_DIM), F32), pltpu.VMEM((hp, _pad_rows(nb), blk), F32),
                        pltpu.VMEM((hp, 1, blk), F32), pltpu.VMEM((hp, 1, blk), F32),
                        pltpu.VMEM((hp, HEAD_DIM, blk), F32)],
        compiler_params=_params("parallel", "parallel", "arbitrary"),
    )(proj, proj, proj)


def _compress_kernel(t_ref, pe_ref, w1_ref, w2_ref, o_ref):
    t = t_ref[0, 0].astype(F32)
    half = t.shape[1]
    top = (t + pe_ref[0, :, :half]).astype(BF16)
    bot = (t + pe_ref[0, :, half:]).astype(BF16)
    a = _dot(top, w1_ref[0, :half, :].astype(BF16))
    bm = _dot(bot, w1_ref[0, half:, :].astype(BF16))
    hidden = a + pltpu.roll(bm, bm.shape[0] - 1, 0)
    act = hidden * jax.nn.sigmoid(hidden)
    o_ref[0, 0] = _dot(act.astype(BF16), w2_ref[0].astype(BF16)).astype(o_ref.dtype)


def nsa_compress(t16, pe_flat, w1, w2):
    b, g2, nc, width = t16.shape
    g = g2 // 2
    hid = w1.shape[-1]
    return pl.pallas_call(
        _compress_kernel,
        grid=(b, g2),
        in_specs=[pl.BlockSpec((1, 1, nc, width), lambda bi, j: (bi, j, 0, 0)),
                  pl.BlockSpec((1, 1, 2 * width), lambda bi, j: (j // g, 0, 0)),
                  pl.BlockSpec((1, 2 * width, hid), lambda bi, j: (j // g, 0, 0)),
                  pl.BlockSpec((1, hid, HEAD_DIM), lambda bi, j: (j // g, 0, 0))],
        out_specs=pl.BlockSpec((1, 1, nc, HEAD_DIM), lambda bi, j: (bi, j, 0, 0)),
        out_shape=jax.ShapeDtypeStruct((b, g2, nc, HEAD_DIM), BF16),
        compiler_params=_params("parallel", "parallel"),
    )(t16, pe_flat, w1, w2)


def _lanes_x(a, n):
    return jnp.concatenate([a] * n, axis=1)


def _nsa_kernel(q_ref, ks_ref, vs_ref, kw_ref, vw_ref, kc_ref, vc_ref, gt_ref, cover_ref, expand_ref,
                o_ref, bias_ref, m_ref, l_ref, acc_ref, *, tq, tk, seq, n_sel):
    i = pl.program_id(2)
    z_heads = NSA_GROUP
    nc = kc_ref.shape[2]
    q = jnp.concatenate([q_ref[0, :, z * HEAD_DIM:(z + 1) * HEAD_DIM] for z in range(z_heads)], axis=0)
    qpos = i * tq + lax.broadcasted_iota(jnp.int32, (1, tq), 1)

    cmp_end = lax.broadcasted_iota(jnp.int32, (nc, tq), 0) * NSA_CMP_STRIDE + (NSA_CMP_LEN - 1)
    cmp_ok = _lanes_x(jnp.where(cmp_end <= qpos, 1.0, 0.0), z_heads)
    sc = jnp.where(cmp_ok > 0.5, _dot_nt(kc_ref[0, 0], q), NEG)
    pc = jnp.exp2(sc - jnp.max(sc, axis=0, keepdims=True))
    pc = pc * (1.0 / jnp.sum(pc, axis=0, keepdims=True)) * cmp_ok
    o_cmp = _dot_tn(vc_ref[0, 0], pc.astype(BF16))

    p_sum = pc[:, 0:tq]
    for z in range(1, z_heads):
        p_sum = p_sum + pc[:, z * tq:(z + 1) * tq]
    p_hi, p_lo = _split_bf16(p_sum)
    imp = (_dot(cover_ref[...], p_hi) + _dot(cover_ref[...], p_lo))[:_pad_rows(n_sel)]
    row = lax.broadcasted_iota(jnp.int32, imp.shape, 0)
    qblk = qpos // NSA_SEL_BLOCK
    forced = (row == 0) | (row == qblk) | (row == qblk - 1)
    allowed = row <= qblk
    imp = jnp.where(forced, BIG, jnp.where(allowed, imp, NEG))
    rank = _rank_before(imp, n_sel)
    chosen = jnp.where(allowed & (rank < NSA_SEL_TOPN), 1.0, 0.0).astype(BF16)
    chunk = 512 if seq % 512 == 0 else seq
    for c0 in range(0, seq, chunk):
        @pl.when(c0 < (i + 1) * tq)
        def _():
            vis = _dot(expand_ref[c0:c0 + chunk, :], chosen)
            kpos = c0 + lax.broadcasted_iota(jnp.int32, (chunk, tq), 0)
            bias_ref[c0:c0 + chunk, :] = jnp.where((vis > 0.5) & (kpos <= qpos), 0.0, NEG)

    def sel_scores(start):
        return _dot_nt(ks_ref[0, pl.ds(start, tk), :], q) + _lanes_x(bias_ref[pl.ds(start, tk), :], z_heads)

    _flash_init(0, sel_scores(0), vs_ref[0, 0:tk, :], m_ref, l_ref, acc_ref)

    def sel_body(j, carry):
        start = pl.multiple_of(j * tk, tk)
        _flash_update(0, sel_scores(start), vs_ref[0, pl.ds(start, tk), :], m_ref, l_ref, acc_ref)
        return carry

    lax.fori_loop(1, ((i + 1) * tq + tk - 1) // tk, sel_body, 0)
    o_sel = _flash_result(0, l_ref, acc_ref)

    span = NSA_WINDOW + tq
    w_start = pl.multiple_of(jnp.maximum(i * tq - NSA_WINDOW, 0), tq)
    dist = qpos - (w_start + lax.broadcasted_iota(jnp.int32, (span, tq), 0))
    w_bias = jnp.where((dist >= 0) & (dist < NSA_WINDOW), 0.0, NEG)
    sw = _dot_nt(kw_ref[0, pl.ds(w_start, span), :], q) + _lanes_x(w_bias, z_heads)
    o_win, _ = _softmax_pv(sw, vw_ref[0, pl.ds(w_start, span), :])

    gt = jax.nn.sigmoid(gt_ref[...])
    for z in range(z_heads):
        ls = slice(z * tq, (z + 1) * tq)
        o = (gt[3 * z:3 * z + 1] * o_cmp[:, ls] + gt[3 * z + 1:3 * z + 2] * o_sel[:, ls]
             + gt[3 * z + 2:3 * z + 3] * o_win[:, ls])
        o_ref[0, :, z * HEAD_DIM:(z + 1) * HEAD_DIM] = o.T.astype(o_ref.dtype)


def nsa_attention(proj, gates_t, cmp_kv, n_groups, q_col, ks_col, vs_col, kw_col, vw_col, tq=128, tk=256):
    b, s, _ = proj.shape
    tq, tk = min(tq, s), min(tk, s)
    assert s % tk == 0 and tk % tq == 0 and NSA_WINDOW % tq == 0 and tq % NSA_SEL_BLOCK == 0
    assert s >= NSA_WINDOW + tq
    nc = cmp_kv.shape[2]
    n_sel = s // NSA_SEL_BLOCK
    assert nc * NSA_CMP_STRIDE == s and n_sel <= LANES
    c_lo = np.arange(nc) * NSA_CMP_STRIDE
    j_lo = np.arange(LANES) * NSA_SEL_BLOCK
    cover_t = ((c_lo[None, :] < j_lo[:, None] + NSA_SEL_BLOCK) & (c_lo[None, :] + NSA_CMP_LEN > j_lo[:, None])
               & (np.arange(LANES)[:, None] < n_sel) & (c_lo[None, :] + NSA_CMP_LEN <= s))
    n_pad = _pad_rows(n_sel)
    expand_t = (np.arange(s)[:, None] // NSA_SEL_BLOCK == np.arange(n_pad)[None, :])
    gw = NSA_GROUP * HEAD_DIM
    qo = q_col // gw
    kso, vso, kwo, vwo = (c // HEAD_DIM for c in (ks_col, vs_col, kw_col, vw_col))
    kv = lambda off: pl.BlockSpec((1, s, HEAD_DIM), lambda bi, g, i: (bi, 0, off + g))
    q_tiles = s // tq
    rows = NSA_GROUP * tq
    return pl.pallas_call(
        functools.partial(_nsa_kernel, tq=tq, tk=tk, seq=s, n_sel=n_sel),
        grid=(b, n_groups, q_tiles),
        in_specs=[pl.BlockSpec((1, tq, gw), lambda bi, g, i: (bi, i, qo + g)),
                  kv(kso), kv(vso), kv(kwo), kv(vwo),
                  pl.BlockSpec((1, 1, nc, HEAD_DIM), lambda bi, g, i: (bi, g, 0, 0)),
                  pl.BlockSpec((1, 1, nc, HEAD_DIM), lambda bi, g, i: (bi, n_groups + g, 0, 0)),
                  pl.BlockSpec((NSA_GATE_PAD, tq), lambda bi, g, i: (g, bi * q_tiles + i)),
                  pl.BlockSpec((LANES, nc), lambda bi, g, i: (0, 0)),
                  pl.BlockSpec((s, n_pad), lambda bi, g, i: (0, 0))],
        out_specs=pl.BlockSpec((1, tq, gw), lambda bi, g, i: (bi, i, g)),
        out_shape=jax.ShapeDtypeStruct((b, s, n_groups * gw), BF16),
        scratch_shapes=[pltpu.VMEM((s, tq), F32), pltpu.VMEM((1, 1, rows), F32),
                        pltpu.VMEM((1, 1, rows), F32), pltpu.VMEM((1, HEAD_DIM, rows), F32)],
        compiler_params=_params("parallel", "parallel", "arbitrary"),
    )(proj, proj, proj, proj, proj, cmp_kv, cmp_kv, gates_t, jnp.asarray(cover_t, BF16),
      jnp.asarray(expand_t, BF16))


def _even_tile_mode(tn, d_model):
    seg_tiles = (d_model // 2) // tn
    return lambda j: ROPE_SCALED - (j // seg_tiles) % 3


def even_mixer(hn, w_in, w_out, e, lam_params, subln_g, lam_init, tables, batch, seq):
    d_model = hn.shape[1]
    seg = d_model // 2
    tn = min(512, seg)
    proj = proj_rope(hn, w_in, e, 6 * seg, tables, seq, _even_tile_mode(tn, d_model), tn=tn)
    proj = proj.reshape(batch, seq, -1)
    o_diff = diff_attention(proj, lam_params, subln_g, lam_init, seg // DIFF_V_DIM, 0, seg, 2 * seg)
    o_moba = moba_attention(proj, seg // HEAD_DIM, 3 * seg, 4 * seg, 5 * seg)
    o = jnp.concatenate([o_diff, o_moba], axis=-1).reshape(batch * seq, d_model)
    return matmul(o, w_out, e, F32)


def _odd_tile_mode(tn, d_model):
    slab_tiles = (d_model // 8) // tn
    q_tiles = d_model // tn
    return lambda j: jnp.where(j < q_tiles, ROPE_SCALED, ROPE - ((j - q_tiles) // slab_tiles) % 2)


def odd_mixer(hn, w_in, w_out, o_idx, cmp_pe, cmp_w1, cmp_w2, tables, batch, seq):
    d_model = hn.shape[1]
    g = d_model // (NSA_GROUP * HEAD_DIM)
    kvd = g * HEAD_DIM
    n_main = d_model + 6 * kvd
    tn = min(512, kvd)
    proj = proj_rope(hn, w_in, o_idx, n_main, tables, seq, _odd_tile_mode(tn, d_model), tn=tn)
    proj = proj.reshape(batch, seq, n_main)
    n_gate = 3 * NSA_GROUP
    w_gate_t = w_in[o_idx, :, n_main:].T.reshape(g, n_gate, d_model)
    w_gate_t = jnp.pad(w_gate_t, ((0, 0), (0, NSA_GATE_PAD - n_gate), (0, 0))).reshape(g * NSA_GATE_PAD, d_model)
    gates_t = matmul_nt(w_gate_t, hn)
    nc = seq // NSA_CMP_STRIDE
    t16 = proj[:, :, d_model:d_model + 2 * kvd].reshape(batch, seq, 2 * g, HEAD_DIM)
    t16 = t16.transpose(0, 2, 1, 3).reshape(batch, 2 * g, nc, NSA_CMP_STRIDE * HEAD_DIM)
    cmp_kv = nsa_compress(t16, cmp_pe.reshape(2, 1, NSA_CMP_LEN * HEAD_DIM), cmp_w1, cmp_w2)
    o = nsa_attention(proj, gates_t, cmp_kv, g, 0, d_model + 2 * kvd, d_model + 3 * kvd, d_model + 4 * kvd,
                      d_model + 5 * kvd)
    return matmul(o.reshape(batch * seq, d_model), w_out, o_idx, F32)


def ffn(hn, w_gate, w_up, w_down, layer):
    return matmul(swiglu_up(hn, w_gate, w_up, layer), w_down, layer, F32, k_splits=2)


def kernel(x, norm_g, ffn_a_gate, ffn_a_up, ffn_a_down, ffn_b_gate, ffn_b_up, ffn_b_down, ev_w_in, ev_w_out, ev_lambda, ev_subln_g, od_w_in, od_w_out, od_cmp_pe, od_cmp_w1, od_cmp_w2):
    batch, seq, d_model = x.shape
    depth = norm_g.shape[0]
    tables = rope_tables(seq)
    h = x.reshape(batch * seq, d_model)
    hn = rmsnorm(h, norm_g[0, 0])
    for layer in range(depth):
        g = norm_g[layer]
        m = ffn(hn, ffn_a_gate, ffn_a_up, ffn_a_down, layer)
        h, hn = residual_norm(h, m, g[1], g[2], 0.5)
        if layer % 2 == 0:
            e = layer // 2
            lam_init = 0.8 - 0.6 * math.exp(-0.3 * layer)
            m = even_mixer(hn, ev_w_in, ev_w_out, e, ev_lambda[e], ev_subln_g[e], lam_init, tables, batch, seq)
        else:
            o = layer // 2
            m = odd_mixer(hn, od_w_in, od_w_out, o, od_cmp_pe[o], od_cmp_w1[o], od_cmp_w2[o], tables, batch, seq)
        h, hn = residual_norm(h, m, g[3], g[4], 1.0)
        m = ffn(hn, ffn_b_gate, ffn_b_up, ffn_b_down, layer)
        g_next = norm_g[layer + 1, 0] if layer + 1 < depth else None
        h, hn = residual_norm(h, m, g[5], g_next, 0.5)
    return h.reshape(batch, seq, d_model)
```

```python
import functools
import math

import numpy as np
import jax
import jax.numpy as jnp
from jax import lax
from jax.experimental import pallas as pl
from jax.experimental.pallas import tpu as pltpu

F32 = jnp.float32
BF16 = jnp.bfloat16

HEAD_DIM = 128
ROT_DIM = HEAD_DIM // 4
ROPE_THETA = 500000.0
RMS_EPS = 1e-6
NEG = -1e30
BIG = 1e30
Q_SCALE = HEAD_DIM ** -0.5 * math.log2(math.e)

DIFF_V_DIM = 2 * HEAD_DIM
MOBA_BLOCK = 256
MOBA_TOPK = 3
NSA_GROUP = 8
NSA_CMP_LEN = 32
NSA_CMP_STRIDE = 16
NSA_SEL_BLOCK = 64
NSA_SEL_TOPN = 16
NSA_WINDOW = 512
NSA_GATE_PAD = 128

LANES = 128
SUBLANES = 8
VMEM_LIMIT_BYTES = 56 * 1024 * 1024


def _params(*semantics):
    return pltpu.CompilerParams(dimension_semantics=semantics, vmem_limit_bytes=VMEM_LIMIT_BYTES)


def _dot(a, b):
    return jnp.dot(a, b, preferred_element_type=F32)


def _dot_nt(a, b):
    return lax.dot_general(a, b, (((1,), (1,)), ((), ())), preferred_element_type=F32)


def _dot_tn(a, b):
    return lax.dot_general(a, b, (((0,), (0,)), ((), ())), preferred_element_type=F32)


def _split_bf16(x):
    hi = x.astype(BF16)
    lo = (x - hi.astype(F32)).astype(BF16)
    return hi, lo


def _rms(x, g):
    return x * lax.rsqrt(jnp.mean(x * x, axis=-1, keepdims=True) + RMS_EPS) * g


def _rmsnorm_kernel(x_ref, g_ref, o_ref):
    o_ref[...] = _rms(x_ref[...], g_ref[...]).astype(o_ref.dtype)


def rmsnorm(x, g, tm=256):
    m, d = x.shape
    tm = min(tm, m)
    return pl.pallas_call(
        _rmsnorm_kernel,
        grid=(m // tm,),
        in_specs=[pl.BlockSpec((tm, d), lambda i: (i, 0)), pl.BlockSpec((1, d), lambda i: (0, 0))],
        out_specs=pl.BlockSpec((tm, d), lambda i: (i, 0)),
        out_shape=jax.ShapeDtypeStruct((m, d), BF16),
        compiler_params=_params("parallel"),
    )(x, g.reshape(1, d))


def _residual_kernel(h_ref, m_ref, gpost_ref, gnext_ref, h_out_ref, hn_out_ref, *, coef):
    y = _rms(m_ref[...], gpost_ref[...])
    if coef != 1.0:
        y = coef * y
    h = h_ref[...] + y
    h_out_ref[...] = h
    hn_out_ref[...] = _rms(h, gnext_ref[...]).astype(hn_out_ref.dtype)


def _residual_last_kernel(h_ref, m_ref, gpost_ref, h_out_ref, *, coef):
    y = _rms(m_ref[...], gpost_ref[...])
    if coef != 1.0:
        y = coef * y
    h_out_ref[...] = h_ref[...] + y


def residual_norm(h, mix, g_post, g_next, coef, tm=256):
    m, d = h.shape
    tm = min(tm, m)
    row = pl.BlockSpec((tm, d), lambda i: (i, 0))
    vec = pl.BlockSpec((1, d), lambda i: (0, 0))
    if g_next is None:
        return pl.pallas_call(
            functools.partial(_residual_last_kernel, coef=coef),
            grid=(m // tm,),
            in_specs=[row, row, vec],
            out_specs=row,
            out_shape=jax.ShapeDtypeStruct((m, d), F32),
            compiler_params=_params("parallel"),
        )(h, mix, g_post.reshape(1, d)), None
    return pl.pallas_call(
        functools.partial(_residual_kernel, coef=coef),
        grid=(m // tm,),
        in_specs=[row, row, vec, vec],
        out_specs=[row, row],
        out_shape=[jax.ShapeDtypeStruct((m, d), F32), jax.ShapeDtypeStruct((m, d), BF16)],
        compiler_params=_params("parallel"),
    )(h, mix, g_post.reshape(1, d), g_next.reshape(1, d))


def _bf16(w):
    return w if w.dtype == BF16 else w.astype(BF16)


def _mm_kernel(x_ref, w_ref, o_ref):
    o_ref[...] = _dot(x_ref[...], _bf16(w_ref[...])).astype(o_ref.dtype)


def _mm_ksplit_kernel(x_ref, w_ref, o_ref):
    k = pl.program_id(2)

    @pl.when(k == 0)
    def _():
        o_ref[...] = _dot(x_ref[...], _bf16(w_ref[...]))

    @pl.when(k > 0)
    def _():
        o_ref[...] += _dot(x_ref[...], _bf16(w_ref[...]))


def matmul(x, w, layer, out_dtype, tm=1024, tn=512, k_splits=1):
    m, kd = x.shape
    n = w.shape[2]
    tm, tn = min(tm, m), min(tn, n)
    if k_splits == 1:
        return pl.pallas_call(
            _mm_kernel,
            grid=(m // tm, n // tn),
            in_specs=[pl.BlockSpec((tm, kd), lambda i, j: (i, 0)),
                      pl.BlockSpec((None, kd, tn), lambda i, j: (layer, 0, j))],
            out_specs=pl.BlockSpec((tm, tn), lambda i, j: (i, j)),
            out_shape=jax.ShapeDtypeStruct((m, n), out_dtype),
            compiler_params=_params("parallel", "parallel"),
        )(x, w)
    assert out_dtype == F32 and kd % k_splits == 0
    tk = kd // k_splits
    return pl.pallas_call(
        _mm_ksplit_kernel,
        grid=(m // tm, n // tn, k_splits),
        in_specs=[pl.BlockSpec((tm, tk), lambda i, j, k: (i, k)),
                  pl.BlockSpec((None, tk, tn), lambda i, j, k: (layer, k, j))],
        out_specs=pl.BlockSpec((tm, tn), lambda i, j, k: (i, j)),
        out_shape=jax.ShapeDtypeStruct((m, n), F32),
        compiler_params=_params("parallel", "parallel", "arbitrary"),
    )(x, w)


def _mm_nt_kernel(w_ref, x_ref, o_ref):
    o_ref[...] = _dot_nt(w_ref[...].astype(BF16), x_ref[...])


def matmul_nt(w_t, x, tm=1024):
    n, kd = w_t.shape
    m = x.shape[0]
    tm = min(tm, m)
    return pl.pallas_call(
        _mm_nt_kernel,
        grid=(m // tm,),
        in_specs=[pl.BlockSpec((n, kd), lambda i: (0, 0)), pl.BlockSpec((tm, kd), lambda i: (i, 0))],
        out_specs=pl.BlockSpec((n, tm), lambda i: (0, i)),
        out_shape=jax.ShapeDtypeStruct((n, m), F32),
        compiler_params=_params("parallel"),
    )(w_t, x)


def _swiglu_kernel(x_ref, wg_ref, wu_ref, o_ref):
    x = x_ref[...]
    g = _dot(x, wg_ref[...].astype(BF16))
    u = _dot(x, wu_ref[...].astype(BF16))
    o_ref[...] = (g * jax.nn.sigmoid(g) * u).astype(o_ref.dtype)


def swiglu_up(x, w_gate, w_up, layer, tm=1024, tn=256):
    m, kd = x.shape
    n = w_gate.shape[2]
    tm, tn = min(tm, m), min(tn, n)
    wspec = pl.BlockSpec((None, kd, tn), lambda i, j: (layer, 0, j))
    return pl.pallas_call(
        _swiglu_kernel,
        grid=(m // tm, n // tn),
        in_specs=[pl.BlockSpec((tm, kd), lambda i, j: (i, 0)), wspec, wspec],
        out_specs=pl.BlockSpec((tm, tn), lambda i, j: (i, j)),
        out_shape=jax.ShapeDtypeStruct((m, n), BF16),
        compiler_params=_params("parallel", "parallel"),
    )(x, w_gate, w_up)


def rope_tables(seq):
    half = ROT_DIM // 2
    pos = jnp.arange(seq, dtype=F32)
    inv = ROPE_THETA ** (-jnp.arange(0, ROT_DIM, 2, dtype=F32) / ROT_DIM)
    ang = pos[:, None] * inv[None, :]
    cos, sin = jnp.cos(ang), jnp.sin(ang)
    zeros = jnp.zeros((seq, HEAD_DIM - ROT_DIM), F32)
    zhalf = jnp.zeros((seq, half), F32)
    c = jnp.concatenate([cos, cos, zeros + 1.0], axis=-1)
    s1 = jnp.concatenate([zhalf, sin, zeros], axis=-1)
    s2 = jnp.concatenate([-sin, zhalf, zeros], axis=-1)
    return c, s1, s2


PLAIN, ROPE, ROPE_SCALED = 0, 1, 2


def _proj_rope_kernel(x_ref, w_ref, c_ref, s1_ref, s2_ref, o_ref, *, tile_mode):
    acc = _dot(x_ref[...], w_ref[...].astype(BF16))
    mode = tile_mode(pl.program_id(1))
    half = ROT_DIM // 2

    def rope_store(scale):
        c, s1, s2 = c_ref[...], s1_ref[...], s2_ref[...]
        for h in range(acc.shape[1] // HEAD_DIM):
            xh = acc[:, h * HEAD_DIM:(h + 1) * HEAD_DIM]
            r = xh * c + pltpu.roll(xh, half, 1) * s1 + pltpu.roll(xh, HEAD_DIM - half, 1) * s2
            if scale != 1.0:
                r = r * scale
            o_ref[:, h * HEAD_DIM:(h + 1) * HEAD_DIM] = r.astype(o_ref.dtype)

    @pl.when(mode == ROPE_SCALED)
    def _():
        rope_store(Q_SCALE)

    @pl.when(mode == ROPE)
    def _():
        rope_store(1.0)

    @pl.when(mode == PLAIN)
    def _():
        o_ref[...] = acc.astype(o_ref.dtype)


def proj_rope(x, w, layer, n, tables, seq, tile_mode, tm=1024, tn=512):
    m, kd = x.shape
    tm, tn = min(tm, seq), min(tn, n)
    assert seq % tm == 0 and m % tm == 0 and n % tn == 0
    pos_blocks = seq // tm
    tspec = pl.BlockSpec((tm, HEAD_DIM), lambda i, j: (i % pos_blocks, 0))
    return pl.pallas_call(
        functools.partial(_proj_rope_kernel, tile_mode=tile_mode),
        grid=(m // tm, n // tn),
        in_specs=[pl.BlockSpec((tm, kd), lambda i, j: (i, 0)),
                  pl.BlockSpec((None, kd, tn), lambda i, j: (layer, 0, j)),
                  tspec, tspec, tspec],
        out_specs=pl.BlockSpec((tm, tn), lambda i, j: (i, j)),
        out_shape=jax.ShapeDtypeStruct((m, n), BF16),
        compiler_params=_params("parallel", "parallel"),
    )(x, w, *tables)


def _flash_init_many(slots, s_list, v_list, m_ref, l_ref, acc_ref):
    m = [jnp.max(s, axis=0, keepdims=True) for s in s_list]
    p = [jnp.exp2(s - mm) for s, mm in zip(s_list, m)]
    pv = [_dot_tn(v, pp.astype(v.dtype)) for v, pp in zip(v_list, p)]
    for idx, c in enumerate(slots):
        m_ref[c] = m[idx]
        l_ref[c] = jnp.sum(p[idx], axis=0, keepdims=True)
        acc_ref[c] = pv[idx]


def _flash_update_many(slots, s_list, v_list, m_ref, l_ref, acc_ref):
    m_old = [m_ref[c] for c in slots]
    l_old = [l_ref[c] for c in slots]
    m_new = [jnp.maximum(mo, jnp.max(s, axis=0, keepdims=True)) for mo, s in zip(m_old, s_list)]
    p = [jnp.exp2(s - mn) for s, mn in zip(s_list, m_new)]
    pv = [_dot_tn(v, pp.astype(v.dtype)) for v, pp in zip(v_list, p)]
    for idx, c in enumerate(slots):
        alpha = jnp.exp2(m_old[idx] - m_new[idx])
        m_ref[c] = m_new[idx]
        l_ref[c] = alpha * l_old[idx] + jnp.sum(p[idx], axis=0, keepdims=True)
        acc_ref[c] = alpha * acc_ref[c] + pv[idx]


def _flash_result(slot, l_ref, acc_ref):
    return acc_ref[slot] * (1.0 / l_ref[slot])


def _softmax_pv(s_t, v):
    p = jnp.exp2(s_t - jnp.max(s_t, axis=0, keepdims=True))
    p = p * (1.0 / jnp.sum(p, axis=0, keepdims=True))
    return _dot_tn(v, p.astype(v.dtype)), p


def _pad_rows(n_rows):
    return -(-n_rows // SUBLANES) * SUBLANES


def _rank_before(v, n_rows):
    row = lax.broadcasted_iota(jnp.int32, v.shape, 0)
    rank = jnp.zeros(v.shape, jnp.int32)
    for j2 in range(n_rows):
        other = v[j2:j2 + 1, :]
        rank = rank + jnp.where(other > v, 1, jnp.where(other == v, (row > j2).astype(jnp.int32), 0))
    return rank


def _diff_kernel(lam_ref, g_ref, q_ref, k_ref, v_ref, o_ref, m_ref, l_ref, acc_ref, *, tq, hp, lam_init):
    i = pl.program_id(2)
    w = DIFF_V_DIM
    lp = lam_ref[...]
    lam = (jnp.exp(jnp.sum(lp[0:1] * lp[1:2], axis=-1, keepdims=True))
           - jnp.exp(jnp.sum(lp[2:3] * lp[3:4], axis=-1, keepdims=True)) + lam_init)
    key = lax.broadcasted_iota(jnp.int32, (tq, tq), 0)
    qry = lax.broadcasted_iota(jnp.int32, (tq, tq), 1)

    def scores(c, start):
        cols = slice(c * HEAD_DIM, (c + 1) * HEAD_DIM)
        return _dot_nt(k_ref[0, pl.ds(start, tq), cols], q_ref[0, :, cols])

    own = pl.multiple_of(i * tq, tq)
    s_own = [jnp.where(key <= qry, scores(c, own), NEG) for c in range(2 * hp)]
    v_own = [v_ref[0, pl.ds(own, tq), (c // 2) * w:(c // 2 + 1) * w] for c in range(2 * hp)]
    _flash_init_many(range(2 * hp), s_own, v_own, m_ref, l_ref, acc_ref)

    def body(j, carry):
        start = pl.multiple_of(j * tq, tq)
        s = [scores(c, start) for c in range(2 * hp)]
        v = [v_ref[0, pl.ds(start, tq), (c // 2) * w:(c // 2 + 1) * w] for c in range(2 * hp)]
        _flash_update_many(range(2 * hp), s, v, m_ref, l_ref, acc_ref)
        return carry

    lax.fori_loop(0, i, body, 0)
    for h in range(hp):
        o = (_flash_result(2 * h, l_ref, acc_ref) - lam * _flash_result(2 * h + 1, l_ref, acc_ref)).T
        o_ref[0, :, h * w:(h + 1) * w] = (_rms(o, g_ref[...]) * (1.0 - lam_init)).astype(o_ref.dtype)


def diff_attention(proj, lam_params, subln_g, lam_init, n_heads, q_col, k_col, v_col, tq=256, hp=2):
    b, s, _ = proj.shape
    tq = min(tq, s)
    w = DIFF_V_DIM
    assert n_heads % hp == 0
    gw = hp * w
    qo, ko, vo = q_col // gw, k_col // gw, v_col // gw
    return pl.pallas_call(
        functools.partial(_diff_kernel, tq=tq, hp=hp, lam_init=lam_init),
        grid=(b, n_heads // hp, s // tq),
        in_specs=[pl.BlockSpec((4, HEAD_DIM), lambda bi, h, i: (0, 0)),
                  pl.BlockSpec((1, w), lambda bi, h, i: (0, 0)),
                  pl.BlockSpec((1, tq, gw), lambda bi, h, i: (bi, i, qo + h)),
                  pl.BlockSpec((1, s, gw), lambda bi, h, i: (bi, 0, ko + h)),
                  pl.BlockSpec((1, s, gw), lambda bi, h, i: (bi, 0, vo + h))],
        out_specs=pl.BlockSpec((1, tq, gw), lambda bi, h, i: (bi, i, h)),
        out_shape=jax.ShapeDtypeStruct((b, s, n_heads * w), BF16),
        scratch_shapes=[pltpu.VMEM((2 * hp, 1, tq), F32), pltpu.VMEM((2 * hp, 1, tq), F32),
                        pltpu.VMEM((2 * hp, w, tq), F32)],
        compiler_params=_params("parallel", "parallel", "arbitrary"),
    )(lam_params, subln_g.reshape(1, w), proj, proj, proj)


def _moba_kernel(q_ref, k_ref, v_ref, o_ref, kmean_ref, bias_ref, m_ref, l_ref, acc_ref, *, nb, hp):
    i = pl.program_id(2)
    blk = MOBA_BLOCK
    head = lambda h: slice(h * HEAD_DIM, (h + 1) * HEAD_DIM)

    @pl.when(i == 0)
    def _():
        kmean_ref[...] = jnp.zeros(kmean_ref.shape, F32)
        for h in range(hp):
            for n in range(nb):
                kb = k_ref[0, n * blk:(n + 1) * blk, head(h)].astype(F32)
                kmean_ref[h, n:n + 1, :] = jnp.mean(kb, axis=0, keepdims=True)

    key = lax.broadcasted_iota(jnp.int32, (blk, blk), 0)
    qry = lax.broadcasted_iota(jnp.int32, (blk, blk), 1)
    own = pl.multiple_of(i * blk, blk)
    qs = [q_ref[0, :, head(h)] for h in range(hp)]
    s_own = [jnp.where(key <= qry, _dot_nt(k_ref[0, pl.ds(own, blk), head(h)], qs[h]), NEG) for h in range(hp)]
    v_own = [v_ref[0, pl.ds(own, blk), head(h)] for h in range(hp)]
    biases = []
    for h in range(hp):
        km_hi, km_lo = _split_bf16(kmean_ref[h])
        gate = (_dot_nt(km_hi, qs[h]) + _dot_nt(km_lo, qs[h]))[:_pad_rows(nb)]
        past = lax.broadcasted_iota(jnp.int32, gate.shape, 0) < i
        rank = _rank_before(jnp.where(past, gate, NEG), nb)
        biases.append(jnp.where(past & (rank < MOBA_TOPK), 0.0, NEG))
    for h in range(hp):
        bias_ref[h] = biases[h]
    _flash_init_many(range(hp), s_own, v_own, m_ref, l_ref, acc_ref)

    def body(n, carry):
        start = pl.multiple_of(n * blk, blk)
        sn = [_dot_nt(k_ref[0, pl.ds(start, blk), head(h)], q_ref[0, :, head(h)]) + bias_ref[h, pl.ds(n, 1), :]
              for h in range(hp)]
        vn = [v_ref[0, pl.ds(start, blk), head(h)] for h in range(hp)]
        _flash_update_many(range(hp), sn, vn, m_ref, l_ref, acc_ref)
        return carry

    lax.fori_loop(0, i, body, 0)
    for h in range(hp):
        o_ref[0, :, head(h)] = _flash_result(h, l_ref, acc_ref).T.astype(o_ref.dtype)


def moba_attention(proj, n_heads, q_col, k_col, v_col, hp=4):
    b, s, _ = proj.shape
    blk = MOBA_BLOCK
    hp = min(hp, n_heads)
    assert s % blk == 0 and s // blk <= LANES and n_heads % hp == 0
    nb = s // blk
    gw = hp * HEAD_DIM
    qo, ko, vo = q_col // gw, k_col // gw, v_col // gw
    return pl.pallas_call(
        functools.partial(_moba_kernel, nb=nb, hp=hp),
        grid=(b, n_heads // hp, nb),
        in_specs=[pl.BlockSpec((1, blk, gw), lambda bi, h, i: (bi, i, qo + h)),
                  pl.BlockSpec((1, s, gw), lambda bi, h, i: (bi, 0, ko + h)),
                  pl.BlockSpec((1, s, gw), lambda bi, h, i: (bi, 0, vo + h))],
        out_specs=pl.BlockSpec((1, blk, gw), lambda bi, h, i: (bi, i, h)),
        out_shape=jax.ShapeDtypeStruct((b, s, n_heads * HEAD_DIM), BF16),
        scratch_shapes=[pltpu.VMEM((hp, LANES, HEAD_DIM), F32), pltpu.VMEM((hp, _pad_rows(nb), blk), F32),
                        pltpu.VMEM((hp, 1, blk), F32), pltpu.VMEM((hp, 1, blk), F32),
                        pltpu.VMEM((hp, HEAD_DIM, blk), F32)],
        compiler_params=_params("parallel", "parallel", "arbitrary"),
    )(proj, proj, proj)


def _compress_kernel(t_ref, pe_ref, w1_ref, w2_ref, o_ref):
    t = t_ref[0, 0].astype(F32)
    half = t.shape[1]
    top = (t + pe_ref[0, :, :half]).astype(BF16)
    bot = (t + pe_ref[0, :, half:]).astype(BF16)
    a = _dot(top, w1_ref[0, :half, :].astype(BF16))
    bm = _dot(bot, w1_ref[0, half:, :].astype(BF16))
    hidden = a + pltpu.roll(bm, bm.shape[0] - 1, 0)
    act = hidden * jax.nn.sigmoid(hidden)
    o_ref[0, 0] = _dot(act.astype(BF16), w2_ref[0].astype(BF16)).astype(o_ref.dtype)


def nsa_compress(t16, pe_flat, w1, w2):
    b, g2, nc, width = t16.shape
    g = g2 // 2
    hid = w1.shape[-1]
    return pl.pallas_call(
        _compress_kernel,
        grid=(b, g2),
        in_specs=[pl.BlockSpec((1, 1, nc, width), lambda bi, j: (bi, j, 0, 0)),
                  pl.BlockSpec((1, 1, 2 * width), lambda bi, j: (j // g, 0, 0)),
                  pl.BlockSpec((1, 2 * width, hid), lambda bi, j: (j // g, 0, 0)),
                  pl.BlockSpec((1, hid, HEAD_DIM), lambda bi, j: (j // g, 0, 0))],
        out_specs=pl.BlockSpec((1, 1, nc, HEAD_DIM), lambda bi, j: (bi, j, 0, 0)),
        out_shape=jax.ShapeDtypeStruct((b, g2, nc, HEAD_DIM), BF16),
        compiler_params=_params("parallel", "parallel"),
    )(t16, pe_flat, w1, w2)


def _lanes_x(a, n):
    return jnp.concatenate([a] * n, axis=1)


def _nsa_kernel(q_ref, ks_ref, vs_ref, kw_ref, vw_ref, kc_ref, vc_ref, gt_ref, cover_ref, expand_ref,
                o_ref, part_ref, m_ref, l_ref, acc_ref, *, tq, tk, n_sel):
    i = pl.program_id(2)
    z_heads = NSA_GROUP
    nc = kc_ref.shape[2]
    q = jnp.concatenate([q_ref[0, :, z * HEAD_DIM:(z + 1) * HEAD_DIM] for z in range(z_heads)], axis=0)
    qpos = i * tq + lax.broadcasted_iota(jnp.int32, (1, tq), 1)

    cmp_end = lax.broadcasted_iota(jnp.int32, (nc, tq), 0) * NSA_CMP_STRIDE + (NSA_CMP_LEN - 1)
    cmp_ok = _lanes_x(jnp.where(cmp_end <= qpos, 1.0, 0.0), z_heads)
    sc = jnp.where(cmp_ok > 0.5, _dot_nt(kc_ref[0, 0], q), NEG)
    pc = jnp.exp2(sc - jnp.max(sc, axis=0, keepdims=True))
    pc = pc * (1.0 / jnp.sum(pc, axis=0, keepdims=True)) * cmp_ok
    o_cmp = _dot_tn(vc_ref[0, 0], pc.astype(BF16))

    p_sum = pc[:, 0:tq]
    for z in range(1, z_heads):
        p_sum = p_sum + pc[:, z * tq:(z + 1) * tq]
    p_hi, p_lo = _split_bf16(p_sum)
    imp = (_dot(cover_ref[...], p_hi) + _dot(cover_ref[...], p_lo))[:_pad_rows(n_sel)]
    row = lax.broadcasted_iota(jnp.int32, imp.shape, 0)
    qblk = qpos // NSA_SEL_BLOCK
    forced = (row == 0) | (row == qblk) | (row == qblk - 1)
    allowed = row <= qblk
    imp = jnp.where(forced, BIG, jnp.where(allowed, imp, NEG))
    rank = _rank_before(imp, n_sel)
    chosen = jnp.where(allowed & (rank < NSA_SEL_TOPN), 1.0, 0.0).astype(BF16)
    span = NSA_WINDOW + tq
    w_start = pl.multiple_of(jnp.maximum(i * tq - NSA_WINDOW, 0), tq)
    dist = qpos - (w_start + lax.broadcasted_iota(jnp.int32, (span, tq), 0))
    w_bias = jnp.where((dist >= 0) & (dist < NSA_WINDOW), 0.0, NEG)
    sw = _dot_nt(kw_ref[0, pl.ds(w_start, span), :], q) + _lanes_x(w_bias, z_heads)
    o_win, _ = _softmax_pv(sw, vw_ref[0, pl.ds(w_start, span), :])
    gt = jax.nn.sigmoid(gt_ref[...])
    part_ref[...] = jnp.concatenate(
        [gt[3 * z:3 * z + 1] * o_cmp[:, z * tq:(z + 1) * tq] + gt[3 * z + 2:3 * z + 3] * o_win[:, z * tq:(z + 1) * tq]
         for z in range(z_heads)], axis=1)

    def sel_tile(start):
        vis = _dot(expand_ref[pl.ds(start, tk), :], chosen)
        kpos = start + lax.broadcasted_iota(jnp.int32, (tk, tq), 0)
        bias = jnp.where((vis > 0.5) & (kpos <= qpos), 0.0, NEG)
        return (_dot_nt(ks_ref[0, pl.ds(start, tk), :], q) + _lanes_x(bias, z_heads),
                vs_ref[0, pl.ds(start, tk), :])

    first = [sel_tile(c * tk) for c in range(2)]
    _flash_init_many(range(2), [s for s, _ in first], [v for _, v in first], m_ref, l_ref, acc_ref)

    def sel_body(t, carry):
        tiles = [sel_tile(pl.multiple_of((2 * t + c) * tk, tk)) for c in range(2)]
        _flash_update_many(range(2), [s for s, _ in tiles], [v for _, v in tiles], m_ref, l_ref, acc_ref)
        return carry

    lax.fori_loop(1, ((i + 1) * tq + 2 * tk - 1) // (2 * tk), sel_body, 0)
    m_all = jnp.maximum(m_ref[0], m_ref[1])
    w0, w1 = jnp.exp2(m_ref[0] - m_all), jnp.exp2(m_ref[1] - m_all)
    o_sel = (w0 * acc_ref[0] + w1 * acc_ref[1]) * (1.0 / (w0 * l_ref[0] + w1 * l_ref[1]))

    for z in range(z_heads):
        ls = slice(z * tq, (z + 1) * tq)
        o = part_ref[:, ls] + jax.nn.sigmoid(gt_ref[3 * z + 1:3 * z + 2, :]) * o_sel[:, ls]
        o_ref[0, :, z * HEAD_DIM:(z + 1) * HEAD_DIM] = o.T.astype(o_ref.dtype)


def nsa_attention(proj, gates_t, cmp_kv, n_groups, q_col, ks_col, vs_col, kw_col, vw_col, tq=128, tk=256):
    b, s, _ = proj.shape
    tq, tk = min(tq, s), min(tk, s)
    assert s % (2 * tk) == 0 and tk % tq == 0 and NSA_WINDOW % tq == 0 and tq % NSA_SEL_BLOCK == 0
    assert s >= NSA_WINDOW + tq
    nc = cmp_kv.shape[2]
    n_sel = s // NSA_SEL_BLOCK
    assert nc * NSA_CMP_STRIDE == s and n_sel <= LANES
    c_lo = np.arange(nc) * NSA_CMP_STRIDE
    j_lo = np.arange(LANES) * NSA_SEL_BLOCK
    cover_t = ((c_lo[None, :] < j_lo[:, None] + NSA_SEL_BLOCK) & (c_lo[None, :] + NSA_CMP_LEN > j_lo[:, None])
               & (np.arange(LANES)[:, None] < n_sel) & (c_lo[None, :] + NSA_CMP_LEN <= s))
    n_pad = _pad_rows(n_sel)
    expand_t = (np.arange(s)[:, None] // NSA_SEL_BLOCK == np.arange(n_pad)[None, :])
    gw = NSA_GROUP * HEAD_DIM
    qo = q_col // gw
    kso, vso, kwo, vwo = (c // HEAD_DIM for c in (ks_col, vs_col, kw_col, vw_col))
    kv = lambda off: pl.BlockSpec((1, s, HEAD_DIM), lambda bi, g, i: (bi, 0, off + g))
    q_tiles = s // tq
    rows = NSA_GROUP * tq
    return pl.pallas_call(
        functools.partial(_nsa_kernel, tq=tq, tk=tk, n_sel=n_sel),
        grid=(b, n_groups, q_tiles),
        in_specs=[pl.BlockSpec((1, tq, gw), lambda bi, g, i: (bi, i, qo + g)),
                  kv(kso), kv(vso), kv(kwo), kv(vwo),
                  pl.BlockSpec((1, 1, nc, HEAD_DIM), lambda bi, g, i: (bi, g, 0, 0)),
                  pl.BlockSpec((1, 1, nc, HEAD_DIM), lambda bi, g, i: (bi, n_groups + g, 0, 0)),
                  pl.BlockSpec((NSA_GATE_PAD, tq), lambda bi, g, i: (g, bi * q_tiles + i)),
                  pl.BlockSpec((LANES, nc), lambda bi, g, i: (0, 0)),
                  pl.BlockSpec((s, n_pad), lambda bi, g, i: (0, 0))],
        out_specs=pl.BlockSpec((1, tq, gw), lambda bi, g, i: (bi, i, g)),
        out_shape=jax.ShapeDtypeStruct((b, s, n_groups * gw), BF16),
        scratch_shapes=[pltpu.VMEM((HEAD_DIM, rows), F32), pltpu.VMEM((2, 1, rows), F32),
                        pltpu.VMEM((2, 1, rows), F32), pltpu.VMEM((2, HEAD_DIM, rows), F32)],
        compiler_params=_params("parallel", "parallel", "arbitrary"),
    )(proj, proj, proj, proj, proj, cmp_kv, cmp_kv, gates_t, jnp.asarray(cover_t, BF16),
      jnp.asarray(expand_t, BF16))


def _even_tile_mode(tn, d_model):
    seg_tiles = (d_model // 2) // tn
    return lambda j: ROPE_SCALED - (j // seg_tiles) % 3


def even_mixer(hn, w_in, w_out, e, lam_params, subln_g, lam_init, tables, batch, seq):
    d_model = hn.shape[1]
    seg = d_model // 2
    tn = min(512, seg)
    proj = proj_rope(hn, w_in, e, 6 * seg, tables, seq, _even_tile_mode(tn, d_model), tn=tn)
    proj = proj.reshape(batch, seq, -1)
    o_diff = diff_attention(proj, lam_params, subln_g, lam_init, seg // DIFF_V_DIM, 0, seg, 2 * seg)
    o_moba = moba_attention(proj, seg // HEAD_DIM, 3 * seg, 4 * seg, 5 * seg)
    o = jnp.concatenate([o_diff, o_moba], axis=-1).reshape(batch * seq, d_model)
    return matmul(o, w_out, e, F32)


def _odd_tile_mode(tn, d_model):
    slab_tiles = (d_model // 8) // tn
    q_tiles = d_model // tn
    return lambda j: jnp.where(j < q_tiles, ROPE_SCALED, ROPE - ((j - q_tiles) // slab_tiles) % 2)


def odd_mixer(hn, w_in, w_out, o_idx, cmp_pe, cmp_w1, cmp_w2, tables, batch, seq):
    d_model = hn.shape[1]
    g = d_model // (NSA_GROUP * HEAD_DIM)
    kvd = g * HEAD_DIM
    n_main = d_model + 6 * kvd
    tn = min(512, kvd)
    proj = proj_rope(hn, w_in, o_idx, n_main, tables, seq, _odd_tile_mode(tn, d_model), tn=tn)
    proj = proj.reshape(batch, seq, n_main)
    n_gate = 3 * NSA_GROUP
    w_gate_t = w_in[o_idx, :, n_main:].T.reshape(g, n_gate, d_model)
    w_gate_t = jnp.pad(w_gate_t, ((0, 0), (0, NSA_GATE_PAD - n_gate), (0, 0))).reshape(g * NSA_GATE_PAD, d_model)
    gates_t = matmul_nt(w_gate_t, hn)
    nc = seq // NSA_CMP_STRIDE
    t16 = proj[:, :, d_model:d_model + 2 * kvd].reshape(batch, seq, 2 * g, HEAD_DIM)
    t16 = t16.transpose(0, 2, 1, 3).reshape(batch, 2 * g, nc, NSA_CMP_STRIDE * HEAD_DIM)
    cmp_kv = nsa_compress(t16, cmp_pe.reshape(2, 1, NSA_CMP_LEN * HEAD_DIM), cmp_w1, cmp_w2)
    o = nsa_attention(proj, gates_t, cmp_kv, g, 0, d_model + 2 * kvd, d_model + 3 * kvd, d_model + 4 * kvd,
                      d_model + 5 * kvd)
    return matmul(o.reshape(batch * seq, d_model), w_out, o_idx, F32)


def ffn(hn, w_gate, w_up, w_down_bf16, layer):
    return matmul(swiglu_up(hn, w_gate, w_up, layer), w_down_bf16, layer, F32, tn=1024, k_splits=2)


def kernel(x, norm_g, ffn_a_gate, ffn_a_up, ffn_a_down, ffn_b_gate, ffn_b_up, ffn_b_down, ev_w_in, ev_w_out, ev_lambda, ev_subln_g, od_w_in, od_w_out, od_cmp_pe, od_cmp_w1, od_cmp_w2):
    batch, seq, d_model = x.shape
    depth = norm_g.shape[0]
    tables = rope_tables(seq)
    ffn_a_down, ffn_b_down = ffn_a_down.astype(BF16), ffn_b_down.astype(BF16)
    h = x.reshape(batch * seq, d_model)
    hn = rmsnorm(h, norm_g[0, 0])
    for layer in range(depth):
        g = norm_g[layer]
        m = ffn(hn, ffn_a_gate, ffn_a_up, ffn_a_down, layer)
        h, hn = residual_norm(h, m, g[1], g[2], 0.5)
        if layer % 2 == 0:
            e = layer // 2
            lam_init = 0.8 - 0.6 * math.exp(-0.3 * layer)
            m = even_mixer(hn, ev_w_in, ev_w_out, e, ev_lambda[e], ev_subln_g[e], lam_init, tables, batch, seq)
        else:
            o = layer // 2
            m = odd_mixer(hn, od_w_in, od_w_out, o, od_cmp_pe[o], od_cmp_w1[o], od_cmp_w2[o], tables, batch, seq)
        h, hn = residual_norm(h, m, g[3], g[4], 1.0)
        m = ffn(hn, ffn_b_gate, ffn_b_up, ffn_b_down, layer)
        g_next = norm_g[layer + 1, 0] if layer + 1 < depth else None
        h, hn = residual_norm(h, m, g[5], g_next, 0.5)
    return h.reshape(batch, seq, d_model)
```

```python
import functools
import math

import numpy as np
import jax
import jax.numpy as jnp
from jax import lax
from jax.experimental import pallas as pl
from jax.experimental.pallas import tpu as pltpu

F32 = jnp.float32
BF16 = jnp.bfloat16

HEAD_DIM = 128
ROT_DIM = HEAD_DIM // 4
ROPE_THETA = 500000.0
RMS_EPS = 1e-6
NEG = -1e30
BIG = 1e30
Q_SCALE = HEAD_DIM ** -0.5 * math.log2(math.e)

DIFF_V_DIM = 2 * HEAD_DIM
MOBA_BLOCK = 256
MOBA_TOPK = 3
NSA_GROUP = 8
NSA_CMP_LEN = 32
NSA_CMP_STRIDE = 16
NSA_SEL_BLOCK = 64
NSA_SEL_TOPN = 16
NSA_WINDOW = 512
NSA_GATE_PAD = 128
NSA_LANE_SPLIT = 2

LANES = 128
SUBLANES = 8
VMEM_LIMIT_BYTES = 56 * 1024 * 1024


def _params(*semantics):
    return pltpu.CompilerParams(dimension_semantics=semantics, vmem_limit_bytes=VMEM_LIMIT_BYTES)


def _dot(a, b):
    return jnp.dot(a, b, preferred_element_type=F32)


def _dot_nt(a, b):
    return lax.dot_general(a, b, (((1,), (1,)), ((), ())), preferred_element_type=F32)


def _dot_tn(a, b):
    return lax.dot_general(a, b, (((0,), (0,)), ((), ())), preferred_element_type=F32)


def _split_bf16(x):
    hi = x.astype(BF16)
    lo = (x - hi.astype(F32)).astype(BF16)
    return hi, lo


def _rms(x, g):
    return x * lax.rsqrt(jnp.mean(x * x, axis=-1, keepdims=True) + RMS_EPS) * g


def _rmsnorm_kernel(x_ref, g_ref, o_ref):
    o_ref[...] = _rms(x_ref[...], g_ref[...]).astype(o_ref.dtype)


def rmsnorm(x, g, tm=256):
    m, d = x.shape
    tm = min(tm, m)
    return pl.pallas_call(
        _rmsnorm_kernel,
        grid=(m // tm,),
        in_specs=[pl.BlockSpec((tm, d), lambda i: (i, 0)), pl.BlockSpec((1, d), lambda i: (0, 0))],
        out_specs=pl.BlockSpec((tm, d), lambda i: (i, 0)),
        out_shape=jax.ShapeDtypeStruct((m, d), BF16),
        compiler_params=_params("parallel"),
    )(x, g.reshape(1, d))


def _residual_kernel(h_ref, m_ref, gpost_ref, gnext_ref, h_out_ref, hn_out_ref, *, coef):
    y = _rms(m_ref[...], gpost_ref[...])
    if coef != 1.0:
        y = coef * y
    h = h_ref[...] + y
    h_out_ref[...] = h
    hn_out_ref[...] = _rms(h, gnext_ref[...]).astype(hn_out_ref.dtype)


def _residual_last_kernel(h_ref, m_ref, gpost_ref, h_out_ref, *, coef):
    y = _rms(m_ref[...], gpost_ref[...])
    if coef != 1.0:
        y = coef * y
    h_out_ref[...] = h_ref[...] + y


def residual_norm(h, mix, g_post, g_next, coef, tm=256):
    m, d = h.shape
    tm = min(tm, m)
    row = pl.BlockSpec((tm, d), lambda i: (i, 0))
    vec = pl.BlockSpec((1, d), lambda i: (0, 0))
    if g_next is None:
        return pl.pallas_call(
            functools.partial(_residual_last_kernel, coef=coef),
            grid=(m // tm,),
            in_specs=[row, row, vec],
            out_specs=row,
            out_shape=jax.ShapeDtypeStruct((m, d), F32),
            compiler_params=_params("parallel"),
        )(h, mix, g_post.reshape(1, d)), None
    return pl.pallas_call(
        functools.partial(_residual_kernel, coef=coef),
        grid=(m // tm,),
        in_specs=[row, row, vec, vec],
        out_specs=[row, row],
        out_shape=[jax.ShapeDtypeStruct((m, d), F32), jax.ShapeDtypeStruct((m, d), BF16)],
        compiler_params=_params("parallel"),
    )(h, mix, g_post.reshape(1, d), g_next.reshape(1, d))


def _bf16(w):
    return w if w.dtype == BF16 else w.astype(BF16)


def _mm_kernel(x_ref, w_ref, o_ref):
    o_ref[...] = _dot(x_ref[...], _bf16(w_ref[...])).astype(o_ref.dtype)


def _mm_ksplit_kernel(x_ref, w_ref, o_ref):
    k = pl.program_id(2)

    @pl.when(k == 0)
    def _():
        o_ref[...] = _dot(x_ref[...], _bf16(w_ref[...]))

    @pl.when(k > 0)
    def _():
        o_ref[...] += _dot(x_ref[...], _bf16(w_ref[...]))


def matmul(x, w, layer, out_dtype, tm=1024, tn=512, k_splits=1):
    m, kd = x.shape
    n = w.shape[2]
    tm, tn = min(tm, m), min(tn, n)
    if k_splits == 1:
        return pl.pallas_call(
            _mm_kernel,
            grid=(m // tm, n // tn),
            in_specs=[pl.BlockSpec((tm, kd), lambda i, j: (i, 0)),
                      pl.BlockSpec((None, kd, tn), lambda i, j: (layer, 0, j))],
            out_specs=pl.BlockSpec((tm, tn), lambda i, j: (i, j)),
            out_shape=jax.ShapeDtypeStruct((m, n), out_dtype),
            compiler_params=_params("parallel", "parallel"),
        )(x, w)
    assert out_dtype == F32 and kd % k_splits == 0
    tk = kd // k_splits
    return pl.pallas_call(
        _mm_ksplit_kernel,
        grid=(m // tm, n // tn, k_splits),
        in_specs=[pl.BlockSpec((tm, tk), lambda i, j, k: (i, k)),
                  pl.BlockSpec((None, tk, tn), lambda i, j, k: (layer, k, j))],
        out_specs=pl.BlockSpec((tm, tn), lambda i, j, k: (i, j)),
        out_shape=jax.ShapeDtypeStruct((m, n), F32),
        compiler_params=_params("parallel", "parallel", "arbitrary"),
    )(x, w)


def _mm_nt_kernel(w_ref, x_ref, o_ref):
    o_ref[...] = _dot_nt(w_ref[...].astype(BF16), x_ref[...])


def matmul_nt(w_t, x, tm=1024):
    n, kd = w_t.shape
    m = x.shape[0]
    tm = min(tm, m)
    return pl.pallas_call(
        _mm_nt_kernel,
        grid=(m // tm,),
        in_specs=[pl.BlockSpec((n, kd), lambda i: (0, 0)), pl.BlockSpec((tm, kd), lambda i: (i, 0))],
        out_specs=pl.BlockSpec((n, tm), lambda i: (0, i)),
        out_shape=jax.ShapeDtypeStruct((n, m), F32),
        compiler_params=_params("parallel"),
    )(w_t, x)


def _swiglu_kernel(x_ref, wg_ref, wu_ref, o_ref):
    x = x_ref[...]
    g = _dot(x, wg_ref[...].astype(BF16))
    u = _dot(x, wu_ref[...].astype(BF16))
    o_ref[...] = (g * jax.nn.sigmoid(g) * u).astype(o_ref.dtype)


def swiglu_up(x, w_gate, w_up, layer, tm=1024, tn=256):
    m, kd = x.shape
    n = w_gate.shape[2]
    tm, tn = min(tm, m), min(tn, n)
    wspec = pl.BlockSpec((None, kd, tn), lambda i, j: (layer, 0, j))
    return pl.pallas_call(
        _swiglu_kernel,
        grid=(m // tm, n // tn),
        in_specs=[pl.BlockSpec((tm, kd), lambda i, j: (i, 0)), wspec, wspec],
        out_specs=pl.BlockSpec((tm, tn), lambda i, j: (i, j)),
        out_shape=jax.ShapeDtypeStruct((m, n), BF16),
        compiler_params=_params("parallel", "parallel"),
    )(x, w_gate, w_up)


def rope_tables(seq):
    half = ROT_DIM // 2
    pos = jnp.arange(seq, dtype=F32)
    inv = ROPE_THETA ** (-jnp.arange(0, ROT_DIM, 2, dtype=F32) / ROT_DIM)
    ang = pos[:, None] * inv[None, :]
    cos, sin = jnp.cos(ang), jnp.sin(ang)
    zeros = jnp.zeros((seq, HEAD_DIM - ROT_DIM), F32)
    zhalf = jnp.zeros((seq, half), F32)
    c = jnp.concatenate([cos, cos, zeros + 1.0], axis=-1)
    s1 = jnp.concatenate([zhalf, sin, zeros], axis=-1)
    s2 = jnp.concatenate([-sin, zhalf, zeros], axis=-1)
    return c, s1, s2


PLAIN, ROPE, ROPE_SCALED = 0, 1, 2


def _proj_rope_kernel(x_ref, w_ref, c_ref, s1_ref, s2_ref, o_ref, *, tile_mode):
    mode = tile_mode(pl.program_id(1))
    half = ROT_DIM // 2
    sub = min(2 * HEAD_DIM, o_ref.shape[1])

    def run(rope, scale):
        x = x_ref[...]
        for c0 in range(0, o_ref.shape[1], sub):
            acc = _dot(x, w_ref[:, c0:c0 + sub].astype(BF16))
            if not rope:
                o_ref[:, c0:c0 + sub] = acc.astype(o_ref.dtype)
                continue
            c, s1, s2 = c_ref[...], s1_ref[...], s2_ref[...]
            for h0 in range(0, sub, HEAD_DIM):
                xh = acc[:, h0:h0 + HEAD_DIM]
                r = xh * c + pltpu.roll(xh, half, 1) * s1 + pltpu.roll(xh, HEAD_DIM - half, 1) * s2
                if scale != 1.0:
                    r = r * scale
                o_ref[:, c0 + h0:c0 + h0 + HEAD_DIM] = r.astype(o_ref.dtype)

    @pl.when(mode == ROPE_SCALED)
    def _():
        run(True, Q_SCALE)

    @pl.when(mode == ROPE)
    def _():
        run(True, 1.0)

    @pl.when(mode == PLAIN)
    def _():
        run(False, 1.0)


def proj_rope(x, w, layer, n, tables, seq, tile_mode, tm=1024, tn=512):
    m, kd = x.shape
    tm, tn = min(tm, seq), min(tn, n)
    assert seq % tm == 0 and m % tm == 0 and n % tn == 0
    pos_blocks = seq // tm
    tspec = pl.BlockSpec((tm, HEAD_DIM), lambda i, j: (i % pos_blocks, 0))
    return pl.pallas_call(
        functools.partial(_proj_rope_kernel, tile_mode=tile_mode),
        grid=(m // tm, n // tn),
        in_specs=[pl.BlockSpec((tm, kd), lambda i, j: (i, 0)),
                  pl.BlockSpec((None, kd, tn), lambda i, j: (layer, 0, j)),
                  tspec, tspec, tspec],
        out_specs=pl.BlockSpec((tm, tn), lambda i, j: (i, j)),
        out_shape=jax.ShapeDtypeStruct((m, n), BF16),
        compiler_params=_params("parallel", "parallel"),
    )(x, w, *tables)


def _flash_init_many(slots, s_list, v_list, m_ref, l_ref, acc_ref):
    m = [jnp.max(s, axis=0, keepdims=True) for s in s_list]
    p = [jnp.exp2(s - mm) for s, mm in zip(s_list, m)]
    pv = [_dot_tn(v, pp.astype(v.dtype)) for v, pp in zip(v_list, p)]
    for idx, c in enumerate(slots):
        m_ref[c] = m[idx]
        l_ref[c] = jnp.sum(p[idx], axis=0, keepdims=True)
        acc_ref[c] = pv[idx]


def _flash_update_many(slots, s_list, v_list, m_ref, l_ref, acc_ref):
    m_old = [m_ref[c] for c in slots]
    l_old = [l_ref[c] for c in slots]
    m_new = [jnp.maximum(mo, jnp.max(s, axis=0, keepdims=True)) for mo, s in zip(m_old, s_list)]
    p = [jnp.exp2(s - mn) for s, mn in zip(s_list, m_new)]
    pv = [_dot_tn(v, pp.astype(v.dtype)) for v, pp in zip(v_list, p)]
    for idx, c in enumerate(slots):
        alpha = jnp.exp2(m_old[idx] - m_new[idx])
        m_ref[c] = m_new[idx]
        l_ref[c] = alpha * l_old[idx] + jnp.sum(p[idx], axis=0, keepdims=True)
        acc_ref[c] = alpha * acc_ref[c] + pv[idx]


def _flash_result(slot, l_ref, acc_ref):
    return acc_ref[slot] * (1.0 / l_ref[slot])


def _softmax_pv(s_parts, v):
    e = [jnp.exp2(s - jnp.max(s, axis=0, keepdims=True)) for s in s_parts]
    pv = [_dot_tn(v, p.astype(v.dtype)) for p in e]
    return jnp.concatenate([o * (1.0 / jnp.sum(p, axis=0, keepdims=True)) for o, p in zip(pv, e)], axis=1)


def _pad_rows(n_rows):
    return -(-n_rows // SUBLANES) * SUBLANES


def _rank_before(v, n_rows):
    row = lax.broadcasted_iota(jnp.int32, v.shape, 0)
    rank = jnp.zeros(v.shape, jnp.int32)
    for j2 in range(n_rows):
        other = v[j2:j2 + 1, :]
        rank = rank + jnp.where(other > v, 1, jnp.where(other == v, (row > j2).astype(jnp.int32), 0))
    return rank


def _diff_kernel(lam_ref, g_ref, q_ref, k_ref, v_ref, o_ref, m_ref, l_ref, acc_ref, *, tq, hp, lam_init):
    i = pl.program_id(2)
    w = DIFF_V_DIM
    lp = lam_ref[...]
    lam = (jnp.exp(jnp.sum(lp[0:1] * lp[1:2], axis=-1, keepdims=True))
           - jnp.exp(jnp.sum(lp[2:3] * lp[3:4], axis=-1, keepdims=True)) + lam_init)
    key = lax.broadcasted_iota(jnp.int32, (tq, tq), 0)
    qry = lax.broadcasted_iota(jnp.int32, (tq, tq), 1)

    def scores(c, start):
        cols = slice(c * HEAD_DIM, (c + 1) * HEAD_DIM)
        return _dot_nt(k_ref[0, pl.ds(start, tq), cols], q_ref[0, :, cols])

    own = pl.multiple_of(i * tq, tq)
    s_own = [jnp.where(key <= qry, scores(c, own), NEG) for c in range(2 * hp)]
    v_own = [v_ref[0, pl.ds(own, tq), (c // 2) * w:(c // 2 + 1) * w] for c in range(2 * hp)]
    _flash_init_many(range(2 * hp), s_own, v_own, m_ref, l_ref, acc_ref)

    def body(j, carry):
        start = pl.multiple_of(j * tq, tq)
        s = [scores(c, start) for c in range(2 * hp)]
        v = [v_ref[0, pl.ds(start, tq), (c // 2) * w:(c // 2 + 1) * w] for c in range(2 * hp)]
        _flash_update_many(range(2 * hp), s, v, m_ref, l_ref, acc_ref)
        return carry

    lax.fori_loop(0, i, body, 0)
    for h in range(hp):
        o = (_flash_result(2 * h, l_ref, acc_ref) - lam * _flash_result(2 * h + 1, l_ref, acc_ref)).T
        o_ref[0, :, h * w:(h + 1) * w] = (_rms(o, g_ref[...]) * (1.0 - lam_init)).astype(o_ref.dtype)


def diff_attention(proj, lam_params, subln_g, lam_init, n_heads, q_col, k_col, v_col, tq=256, hp=4):
    b, s, _ = proj.shape
    tq = min(tq, s)
    w = DIFF_V_DIM
    hp = min(hp, n_heads)
    assert n_heads % hp == 0
    gw = hp * w
    qo, ko, vo = q_col // gw, k_col // gw, v_col // gw
    return pl.pallas_call(
        functools.partial(_diff_kernel, tq=tq, hp=hp, lam_init=lam_init),
        grid=(b, n_heads // hp, s // tq),
        in_specs=[pl.BlockSpec((4, HEAD_DIM), lambda bi, h, i: (0, 0)),
                  pl.BlockSpec((1, w), lambda bi, h, i: (0, 0)),
                  pl.BlockSpec((1, tq, gw), lambda bi, h, i: (bi, i, qo + h)),
                  pl.BlockSpec((1, s, gw), lambda bi, h, i: (bi, 0, ko + h)),
                  pl.BlockSpec((1, s, gw), lambda bi, h, i: (bi, 0, vo + h))],
        out_specs=pl.BlockSpec((1, tq, gw), lambda bi, h, i: (bi, i, h)),
        out_shape=jax.ShapeDtypeStruct((b, s, n_heads * w), BF16),
        scratch_shapes=[pltpu.VMEM((2 * hp, 1, tq), F32), pltpu.VMEM((2 * hp, 1, tq), F32),
                        pltpu.VMEM((2 * hp, w, tq), F32)],
        compiler_params=_params("parallel", "parallel", "arbitrary"),
    )(lam_params, subln_g.reshape(1, w), proj, proj, proj)


def _moba_kernel(q_ref, k_ref, v_ref, o_ref, kmean_ref, bias_ref, m_ref, l_ref, acc_ref, *, nb, hp):
    i = pl.program_id(2)
    blk = MOBA_BLOCK
    head = lambda h: slice(h * HEAD_DIM, (h + 1) * HEAD_DIM)

    @pl.when(i == 0)
    def _():
        kmean_ref[...] = jnp.zeros(kmean_ref.shape, F32)
        for h in range(hp):
            for n in range(nb):
                kb = k_ref[0, n * blk:(n + 1) * blk, head(h)].astype(F32)
                kmean_ref[h, n:n + 1, :] = jnp.mean(kb, axis=0, keepdims=True)

    key = lax.broadcasted_iota(jnp.int32, (blk, blk), 0)
    qry = lax.broadcasted_iota(jnp.int32, (blk, blk), 1)
    own = pl.multiple_of(i * blk, blk)
    qs = [q_ref[0, :, head(h)] for h in range(hp)]
    s_own = [jnp.where(key <= qry, _dot_nt(k_ref[0, pl.ds(own, blk), head(h)], qs[h]), NEG) for h in range(hp)]
    v_own = [v_ref[0, pl.ds(own, blk), head(h)] for h in range(hp)]
    biases = []
    for h in range(hp):
        km_hi, km_lo = _split_bf16(kmean_ref[h])
        gate = (_dot_nt(km_hi, qs[h]) + _dot_nt(km_lo, qs[h]))[:_pad_rows(nb)]
        past = lax.broadcasted_iota(jnp.int32, gate.shape, 0) < i
        rank = _rank_before(jnp.where(past, gate, NEG), nb)
        biases.append(jnp.where(past & (rank < MOBA_TOPK), 0.0, NEG))
    for h in range(hp):
        bias_ref[h] = biases[h]
    _flash_init_many(range(hp), s_own, v_own, m_ref, l_ref, acc_ref)

    def body(n, carry):
        start = pl.multiple_of(n * blk, blk)
        sn = [_dot_nt(k_ref[0, pl.ds(start, blk), head(h)], q_ref[0, :, head(h)]) + bias_ref[h, pl.ds(n, 1), :]
              for h in range(hp)]
        vn = [v_ref[0, pl.ds(start, blk), head(h)] for h in range(hp)]
        _flash_update_many(range(hp), sn, vn, m_ref, l_ref, acc_ref)
        return carry

    lax.fori_loop(0, i, body, 0)
    for h in range(hp):
        o_ref[0, :, head(h)] = _flash_result(h, l_ref, acc_ref).T.astype(o_ref.dtype)


def moba_attention(proj, n_heads, q_col, k_col, v_col, hp=8):
    b, s, _ = proj.shape
    blk = MOBA_BLOCK
    hp = min(hp, n_heads)
    assert s % blk == 0 and s // blk <= LANES and n_heads % hp == 0
    nb = s // blk
    gw = hp * HEAD_DIM
    qo, ko, vo = q_col // gw, k_col // gw, v_col // gw
    return pl.pallas_call(
        functools.partial(_moba_kernel, nb=nb, hp=hp),
        grid=(b, n_heads // hp, nb),
        in_specs=[pl.BlockSpec((1, blk, gw), lambda bi, h, i: (bi, i, qo + h)),
                  pl.BlockSpec((1, s, gw), lambda bi, h, i: (bi, 0, ko + h)),
                  pl.BlockSpec((1, s, gw), lambda bi, h, i: (bi, 0, vo + h))],
        out_specs=pl.BlockSpec((1, blk, gw), lambda bi, h, i: (bi, i, h)),
        out_shape=jax.ShapeDtypeStruct((b, s, n_heads * HEAD_DIM), BF16),
        scratch_shapes=[pltpu.VMEM((hp, LANES, HEAD_DIM), F32), pltpu.VMEM((hp, _pad_rows(nb), blk), F32),
                        pltpu.VMEM((hp, 1, blk), F32), pltpu.VMEM((hp, 1, blk), F32),
                        pltpu.VMEM((hp, HEAD_DIM, blk), F32)],
        compiler_params=_params("parallel", "parallel", "arbitrary"),
    )(proj, proj, proj)


def _compress_kernel(t_ref, pe_ref, w1_ref, w2_ref, o_ref):
    t = t_ref[0, 0].astype(F32)
    half = t.shape[1]
    top = (t + pe_ref[0, :, :half]).astype(BF16)
    bot = (t + pe_ref[0, :, half:]).astype(BF16)
    a = _dot(top, w1_ref[0, :half, :].astype(BF16))
    bm = _dot(bot, w1_ref[0, half:, :].astype(BF16))
    hidden = a + pltpu.roll(bm, bm.shape[0] - 1, 0)
    act = hidden * jax.nn.sigmoid(hidden)
    o_ref[0, 0] = _dot(act.astype(BF16), w2_ref[0].astype(BF16)).astype(o_ref.dtype)


def nsa_compress(t16, pe_flat, w1, w2):
    b, g2, nc, width = t16.shape
    g = g2 // 2
    hid = w1.shape[-1]
    return pl.pallas_call(
        _compress_kernel,
        grid=(b, g2),
        in_specs=[pl.BlockSpec((1, 1, nc, width), lambda bi, j: (bi, j, 0, 0)),
                  pl.BlockSpec((1, 1, 2 * width), lambda bi, j: (j // g, 0, 0)),
                  pl.BlockSpec((1, 2 * width, hid), lambda bi, j: (j // g, 0, 0)),
                  pl.BlockSpec((1, hid, HEAD_DIM), lambda bi, j: (j // g, 0, 0))],
        out_specs=pl.BlockSpec((1, 1, nc, HEAD_DIM), lambda bi, j: (bi, j, 0, 0)),
        out_shape=jax.ShapeDtypeStruct((b, g2, nc, HEAD_DIM), BF16),
        compiler_params=_params("parallel", "parallel"),
    )(t16, pe_flat, w1, w2)


def _lanes_x(a, n):
    return jnp.concatenate([a] * n, axis=1)


def _nsa_kernel(q_ref, ks_ref, vs_ref, kw_ref, vw_ref, kc_ref, vc_ref, gt_ref, cover_ref, expand_ref,
                o_ref, part_ref, m_ref, l_ref, acc_ref, *, tq, tk, n_sel):
    i = pl.program_id(2)
    z_heads = NSA_GROUP
    nc = kc_ref.shape[2]
    q = jnp.concatenate([q_ref[0, :, z * HEAD_DIM:(z + 1) * HEAD_DIM] for z in range(z_heads)], axis=0)
    qpos = i * tq + lax.broadcasted_iota(jnp.int32, (1, tq), 1)

    cmp_end = lax.broadcasted_iota(jnp.int32, (nc, tq), 0) * NSA_CMP_STRIDE + (NSA_CMP_LEN - 1)
    cmp_ok = _lanes_x(jnp.where(cmp_end <= qpos, 1.0, 0.0), z_heads)
    sc = jnp.where(cmp_ok > 0.5, _dot_nt(kc_ref[0, 0], q), NEG)
    pc = jnp.exp2(sc - jnp.max(sc, axis=0, keepdims=True))
    pc = pc * (1.0 / jnp.sum(pc, axis=0, keepdims=True)) * cmp_ok
    o_cmp = _dot_tn(vc_ref[0, 0], pc.astype(BF16))

    p_sum = pc[:, 0:tq]
    for z in range(1, z_heads):
        p_sum = p_sum + pc[:, z * tq:(z + 1) * tq]
    p_hi, p_lo = _split_bf16(p_sum)
    imp = (_dot(cover_ref[...], p_hi) + _dot(cover_ref[...], p_lo))[:_pad_rows(n_sel)]
    row = lax.broadcasted_iota(jnp.int32, imp.shape, 0)
    qblk = qpos // NSA_SEL_BLOCK
    forced = (row == 0) | (row == qblk) | (row == qblk - 1)
    allowed = row <= qblk
    imp = jnp.where(forced, BIG, jnp.where(allowed, imp, NEG))
    rank = _rank_before(imp, n_sel)
    block_bias = jnp.where(allowed & (rank < NSA_SEL_TOPN), 0.0, NEG)
    block_bias = jnp.concatenate(
        [block_bias, jnp.zeros((LANES - block_bias.shape[0], tq), F32)], axis=0).T.astype(BF16)
    q2 = jnp.concatenate([q, jnp.concatenate([block_bias] * z_heads, axis=0)], axis=1)
    span = NSA_WINDOW + tq
    w_start = pl.multiple_of(jnp.maximum(i * tq - NSA_WINDOW, 0), tq)
    dist = qpos - (w_start + lax.broadcasted_iota(jnp.int32, (span, tq), 0))
    w_bias = jnp.where((dist >= 0) & (dist < NSA_WINDOW), 0.0, NEG)
    kw = kw_ref[0, pl.ds(w_start, span), :]
    half_rows = z_heads * tq // 2
    sw = [_dot_nt(kw, q[p * half_rows:(p + 1) * half_rows]) + _lanes_x(w_bias, z_heads // 2) for p in range(2)]
    o_win = _softmax_pv(sw, vw_ref[0, pl.ds(w_start, span), :])
    gt = jax.nn.sigmoid(gt_ref[...])
    part_ref[...] = jnp.concatenate(
        [gt[3 * z:3 * z + 1] * o_cmp[:, z * tq:(z + 1) * tq] + gt[3 * z + 2:3 * z + 3] * o_win[:, z * tq:(z + 1) * tq]
         for z in range(z_heads)], axis=1)

    n_split = NSA_LANE_SPLIT
    part_w = z_heads * tq // n_split
    halves = tuple(slice(p * part_w, (p + 1) * part_w) for p in range(n_split))

    def sel_pair(t, causal):
        s, v = [], []
        for c in range(2):
            start = (2 * t + c) * tk
            if not isinstance(start, int):
                start = pl.multiple_of(start, tk)
            k2 = jnp.concatenate([ks_ref[0, pl.ds(start, tk), :], expand_ref[pl.ds(start, tk), :]], axis=1)
            sc = _dot_nt(k2, q2)
            if causal:
                kpos = start + lax.broadcasted_iota(jnp.int32, (tk, tq), 0)
                sc = sc + _lanes_x(jnp.where(kpos <= qpos, 0.0, NEG), z_heads)
            vt = vs_ref[0, pl.ds(start, tk), :]
            for lanes in halves:
                s.append(sc[:, lanes])
                v.append(vt)
        return s, v

    n_pairs = ((i + 1) * tq + 2 * tk - 1) // (2 * tk)
    s0, v0 = sel_pair(0, True)
    _flash_init_many(range(2 * n_split), s0, v0, m_ref, l_ref, acc_ref)

    def sel_body(t, carry):
        st, vt = sel_pair(t, False)
        _flash_update_many(range(2 * n_split), st, vt, m_ref, l_ref, acc_ref)
        return carry

    lax.fori_loop(1, n_pairs - 1, sel_body, 0)

    @pl.when(n_pairs > 1)
    def _():
        st, vt = sel_pair(n_pairs - 1, True)
        _flash_update_many(range(2 * n_split), st, vt, m_ref, l_ref, acc_ref)

    merged = []
    for a in range(n_split):
        b = n_split + a
        m_all = jnp.maximum(m_ref[a], m_ref[b])
        w0, w1 = jnp.exp2(m_ref[a] - m_all), jnp.exp2(m_ref[b] - m_all)
        merged.append((w0 * acc_ref[a] + w1 * acc_ref[b]) * (1.0 / (w0 * l_ref[a] + w1 * l_ref[b])))
    o_sel = jnp.concatenate(merged, axis=1)

    for z in range(z_heads):
        ls = slice(z * tq, (z + 1) * tq)
        o = part_ref[:, ls] + jax.nn.sigmoid(gt_ref[3 * z + 1:3 * z + 2, :]) * o_sel[:, ls]
        o_ref[0, :, z * HEAD_DIM:(z + 1) * HEAD_DIM] = o.T.astype(o_ref.dtype)


def nsa_attention(proj, gates_t, cmp_kv, n_groups, q_col, ks_col, vs_col, kw_col, vw_col, tq=128, tk=256):
    b, s, _ = proj.shape
    tq, tk = min(tq, s), min(tk, s)
    assert s % (2 * tk) == 0 and tk % tq == 0 and NSA_WINDOW % tq == 0 and tq % NSA_SEL_BLOCK == 0
    assert s >= NSA_WINDOW + tq
    nc = cmp_kv.shape[2]
    n_sel = s // NSA_SEL_BLOCK
    assert nc * NSA_CMP_STRIDE == s and n_sel <= LANES
    c_lo = np.arange(nc) * NSA_CMP_STRIDE
    j_lo = np.arange(LANES) * NSA_SEL_BLOCK
    cover_t = ((c_lo[None, :] < j_lo[:, None] + NSA_SEL_BLOCK) & (c_lo[None, :] + NSA_CMP_LEN > j_lo[:, None])
               & (np.arange(LANES)[:, None] < n_sel) & (c_lo[None, :] + NSA_CMP_LEN <= s))
    expand_t = (np.arange(s)[:, None] // NSA_SEL_BLOCK == np.arange(LANES)[None, :])
    gw = NSA_GROUP * HEAD_DIM
    qo = q_col // gw
    kso, vso, kwo, vwo = (c // HEAD_DIM for c in (ks_col, vs_col, kw_col, vw_col))
    kv = lambda off: pl.BlockSpec((1, s, HEAD_DIM), lambda bi, g, i: (bi, 0, off + g))
    q_tiles = s // tq
    rows = NSA_GROUP * tq
    return pl.pallas_call(
        functools.partial(_nsa_kernel, tq=tq, tk=tk, n_sel=n_sel),
        grid=(b, n_groups, q_tiles),
        in_specs=[pl.BlockSpec((1, tq, gw), lambda bi, g, i: (bi, i, qo + g)),
                  kv(kso), kv(vso), kv(kwo), kv(vwo),
                  pl.BlockSpec((1, 1, nc, HEAD_DIM), lambda bi, g, i: (bi, g, 0, 0)),
                  pl.BlockSpec((1, 1, nc, HEAD_DIM), lambda bi, g, i: (bi, n_groups + g, 0, 0)),
                  pl.BlockSpec((NSA_GATE_PAD, tq), lambda bi, g, i: (g, bi * q_tiles + i)),
                  pl.BlockSpec((LANES, nc), lambda bi, g, i: (0, 0)),
                  pl.BlockSpec((s, LANES), lambda bi, g, i: (0, 0))],
        out_specs=pl.BlockSpec((1, tq, gw), lambda bi, g, i: (bi, i, g)),
        out_shape=jax.ShapeDtypeStruct((b, s, n_groups * gw), BF16),
        scratch_shapes=[pltpu.VMEM((HEAD_DIM, rows), F32),
                        pltpu.VMEM((2 * NSA_LANE_SPLIT, 1, rows // NSA_LANE_SPLIT), F32),
                        pltpu.VMEM((2 * NSA_LANE_SPLIT, 1, rows // NSA_LANE_SPLIT), F32),
                        pltpu.VMEM((2 * NSA_LANE_SPLIT, HEAD_DIM, rows // NSA_LANE_SPLIT), F32)],
        compiler_params=_params("parallel", "parallel", "arbitrary"),
    )(proj, proj, proj, proj, proj, cmp_kv, cmp_kv, gates_t, jnp.asarray(cover_t, BF16),
      jnp.asarray(expand_t, BF16))


def _even_tile_mode(tn, d_model):
    seg_tiles = (d_model // 2) // tn
    return lambda j: ROPE_SCALED - (j // seg_tiles) % 3


def even_mixer(hn, w_in, w_out, e, lam_params, subln_g, lam_init, tables, batch, seq):
    d_model = hn.shape[1]
    seg = d_model // 2
    tn = min(512, seg)
    proj = proj_rope(hn, w_in, e, 6 * seg, tables, seq, _even_tile_mode(tn, d_model), tn=tn)
    proj = proj.reshape(batch, seq, -1)
    o_diff = diff_attention(proj, lam_params, subln_g, lam_init, seg // DIFF_V_DIM, 0, seg, 2 * seg)
    o_moba = moba_attention(proj, seg // HEAD_DIM, 3 * seg, 4 * seg, 5 * seg)
    o = jnp.concatenate([o_diff, o_moba], axis=-1).reshape(batch * seq, d_model)
    return matmul(o, w_out, e, F32)


def _odd_tile_mode(tn, d_model):
    slab_tiles = (d_model // 8) // tn
    q_tiles = d_model // tn
    return lambda j: jnp.where(j < q_tiles, ROPE_SCALED, ROPE - ((j - q_tiles) // slab_tiles) % 2)


def odd_mixer(hn, w_in, w_out, o_idx, cmp_pe, cmp_w1, cmp_w2, tables, batch, seq):
    d_model = hn.shape[1]
    g = d_model // (NSA_GROUP * HEAD_DIM)
    kvd = g * HEAD_DIM
    n_main = d_model + 6 * kvd
    tn = min(512, kvd)
    proj = proj_rope(hn, w_in, o_idx, n_main, tables, seq, _odd_tile_mode(tn, d_model), tn=tn)
    proj = proj.reshape(batch, seq, n_main)
    n_gate = 3 * NSA_GROUP
    w_gate_t = w_in[o_idx, :, n_main:].T.reshape(g, n_gate, d_model)
    w_gate_t = jnp.pad(w_gate_t, ((0, 0), (0, NSA_GATE_PAD - n_gate), (0, 0))).reshape(g * NSA_GATE_PAD, d_model)
    gates_t = matmul_nt(w_gate_t, hn)
    nc = seq // NSA_CMP_STRIDE
    t16 = proj[:, :, d_model:d_model + 2 * kvd].reshape(batch, seq, 2 * g, HEAD_DIM)
    t16 = t16.transpose(0, 2, 1, 3).reshape(batch, 2 * g, nc, NSA_CMP_STRIDE * HEAD_DIM)
    cmp_kv = nsa_compress(t16, cmp_pe.reshape(2, 1, NSA_CMP_LEN * HEAD_DIM), cmp_w1, cmp_w2)
    o = nsa_attention(proj, gates_t, cmp_kv, g, 0, d_model + 2 * kvd, d_model + 3 * kvd, d_model + 4 * kvd,
                      d_model + 5 * kvd)
    return matmul(o.reshape(batch * seq, d_model), w_out, o_idx, F32)


def ffn(hn, w_gate, w_up, w_down_bf16, layer):
    return matmul(swiglu_up(hn, w_gate, w_up, layer), w_down_bf16, layer, F32, tn=1024, k_splits=2)


def kernel(x, norm_g, ffn_a_gate, ffn_a_up, ffn_a_down, ffn_b_gate, ffn_b_up, ffn_b_down, ev_w_in, ev_w_out, ev_lambda, ev_subln_g, od_w_in, od_w_out, od_cmp_pe, od_cmp_w1, od_cmp_w2):
    batch, seq, d_model = x.shape
    depth = norm_g.shape[0]
    tables = rope_tables(seq)
    ffn_a_down, ffn_b_down = ffn_a_down.astype(BF16), ffn_b_down.astype(BF16)
    h = x.reshape(batch * seq, d_model)
    hn = rmsnorm(h, norm_g[0, 0])
    for layer in range(depth):
        g = norm_g[layer]
        m = ffn(hn, ffn_a_gate, ffn_a_up, ffn_a_down, layer)
        h, hn = residual_norm(h, m, g[1], g[2], 0.5)
        if layer % 2 == 0:
            e = layer // 2
            lam_init = 0.8 - 0.6 * math.exp(-0.3 * layer)
            m = even_mixer(hn, ev_w_in, ev_w_out, e, ev_lambda[e], ev_subln_g[e], lam_init, tables, batch, seq)
        else:
            o = layer // 2
            m = odd_mixer(hn, od_w_in, od_w_out, o, od_cmp_pe[o], od_cmp_w1[o], od_cmp_w2[o], tables, batch, seq)
        h, hn = residual_norm(h, m, g[3], g[4], 1.0)
        m = ffn(hn, ffn_b_gate, ffn_b_up, ffn_b_down, layer)
        g_next = norm_g[layer + 1, 0] if layer + 1 < depth else None
        h, hn = residual_norm(h, m, g[5], g_next, 0.5)
    return h.reshape(batch, seq, d_model)
```

```python
import functools
import math

import numpy as np
import jax
import jax.numpy as jnp
from jax import lax
from jax.experimental import pallas as pl
from jax.experimental.pallas import tpu as pltpu

F32 = jnp.float32
BF16 = jnp.bfloat16

HEAD_DIM = 128
ROT_DIM = HEAD_DIM // 4
ROPE_THETA = 500000.0
RMS_EPS = 1e-6
NEG = -1e30
BIG = 1e30
Q_SCALE = HEAD_DIM ** -0.5 * math.log2(math.e)

DIFF_V_DIM = 2 * HEAD_DIM
MOBA_BLOCK = 256
MOBA_TOPK = 3
NSA_GROUP = 8
NSA_CMP_LEN = 32
NSA_CMP_STRIDE = 16
NSA_SEL_BLOCK = 64
NSA_SEL_TOPN = 16
NSA_WINDOW = 512
NSA_GATE_PAD = 128
NSA_LANE_SPLIT = 2

LANES = 128
SUBLANES = 8
VMEM_LIMIT_BYTES = 56 * 1024 * 1024


def _params(*semantics):
    return pltpu.CompilerParams(dimension_semantics=semantics, vmem_limit_bytes=VMEM_LIMIT_BYTES)


def _dot(a, b):
    return jnp.dot(a, b, preferred_element_type=F32)


def _dot_nt(a, b):
    return lax.dot_general(a, b, (((1,), (1,)), ((), ())), preferred_element_type=F32)


def _dot_tn(a, b):
    return lax.dot_general(a, b, (((0,), (0,)), ((), ())), preferred_element_type=F32)


def _split_bf16(x):
    hi = x.astype(BF16)
    lo = (x - hi.astype(F32)).astype(BF16)
    return hi, lo


def _rms(x, g):
    return x * lax.rsqrt(jnp.mean(x * x, axis=-1, keepdims=True) + RMS_EPS) * g


def _rmsnorm_kernel(x_ref, g_ref, o_ref):
    o_ref[...] = _rms(x_ref[...], g_ref[...]).astype(o_ref.dtype)


def rmsnorm(x, g, tm=256):
    m, d = x.shape
    tm = min(tm, m)
    return pl.pallas_call(
        _rmsnorm_kernel,
        grid=(m // tm,),
        in_specs=[pl.BlockSpec((tm, d), lambda i: (i, 0)), pl.BlockSpec((1, d), lambda i: (0, 0))],
        out_specs=pl.BlockSpec((tm, d), lambda i: (i, 0)),
        out_shape=jax.ShapeDtypeStruct((m, d), BF16),
        compiler_params=_params("parallel"),
    )(x, g.reshape(1, d))


def _residual_kernel(h_ref, m_ref, gpost_ref, gnext_ref, h_out_ref, hn_out_ref, *, coef):
    y = _rms(m_ref[...], gpost_ref[...])
    if coef != 1.0:
        y = coef * y
    h = h_ref[...] + y
    h_out_ref[...] = h
    hn_out_ref[...] = _rms(h, gnext_ref[...]).astype(hn_out_ref.dtype)


def _residual_last_kernel(h_ref, m_ref, gpost_ref, h_out_ref, *, coef):
    y = _rms(m_ref[...], gpost_ref[...])
    if coef != 1.0:
        y = coef * y
    h_out_ref[...] = h_ref[...] + y


def residual_norm(h, mix, g_post, g_next, coef, tm=256):
    m, d = h.shape
    tm = min(tm, m)
    row = pl.BlockSpec((tm, d), lambda i: (i, 0))
    vec = pl.BlockSpec((1, d), lambda i: (0, 0))
    if g_next is None:
        return pl.pallas_call(
            functools.partial(_residual_last_kernel, coef=coef),
            grid=(m // tm,),
            in_specs=[row, row, vec],
            out_specs=row,
            out_shape=jax.ShapeDtypeStruct((m, d), F32),
            compiler_params=_params("parallel"),
        )(h, mix, g_post.reshape(1, d)), None
    return pl.pallas_call(
        functools.partial(_residual_kernel, coef=coef),
        grid=(m // tm,),
        in_specs=[row, row, vec, vec],
        out_specs=[row, row],
        out_shape=[jax.ShapeDtypeStruct((m, d), F32), jax.ShapeDtypeStruct((m, d), BF16)],
        compiler_params=_params("parallel"),
    )(h, mix, g_post.reshape(1, d), g_next.reshape(1, d))


def _mm_kernel(x_ref, w_ref, o_ref):
    o_ref[...] = _dot(x_ref[...], w_ref[...].astype(BF16)).astype(o_ref.dtype)


def matmul(x, w, layer, out_dtype, tm=1024, tn=512):
    m, kd = x.shape
    n = w.shape[2]
    tm, tn = min(tm, m), min(tn, n)
    return pl.pallas_call(
        _mm_kernel,
        grid=(m // tm, n // tn),
        in_specs=[pl.BlockSpec((tm, kd), lambda i, j: (i, 0)),
                  pl.BlockSpec((None, kd, tn), lambda i, j: (layer, 0, j))],
        out_specs=pl.BlockSpec((tm, tn), lambda i, j: (i, j)),
        out_shape=jax.ShapeDtypeStruct((m, n), out_dtype),
        compiler_params=_params("parallel", "parallel"),
    )(x, w)


def _mm_nt_kernel(w_ref, x_ref, o_ref):
    o_ref[...] = _dot_nt(w_ref[...].astype(BF16), x_ref[...])


def matmul_nt(w_t, x, tm=1024):
    n, kd = w_t.shape
    m = x.shape[0]
    tm = min(tm, m)
    return pl.pallas_call(
        _mm_nt_kernel,
        grid=(m // tm,),
        in_specs=[pl.BlockSpec((n, kd), lambda i: (0, 0)), pl.BlockSpec((tm, kd), lambda i: (i, 0))],
        out_specs=pl.BlockSpec((n, tm), lambda i: (0, i)),
        out_shape=jax.ShapeDtypeStruct((n, m), F32),
        compiler_params=_params("parallel"),
    )(w_t, x)


def _swiglu_kernel(x_ref, wg_ref, wu_ref, o_ref):
    x = x_ref[...]
    g = _dot(x, wg_ref[...].astype(BF16))
    u = _dot(x, wu_ref[...].astype(BF16))
    o_ref[...] = (g * jax.nn.sigmoid(g) * u).astype(o_ref.dtype)


def swiglu_up(x, w_gate, w_up, layer, tm=1024, tn=512):
    m, kd = x.shape
    n = w_gate.shape[2]
    tm, tn = min(tm, m), min(tn, n)
    wspec = pl.BlockSpec((None, kd, tn), lambda i, j: (layer, 0, j))
    return pl.pallas_call(
        _swiglu_kernel,
        grid=(m // tm, n // tn),
        in_specs=[pl.BlockSpec((tm, kd), lambda i, j: (i, 0), pipeline_mode=pl.Buffered(1)), wspec, wspec],
        out_specs=pl.BlockSpec((tm, tn), lambda i, j: (i, j)),
        out_shape=jax.ShapeDtypeStruct((m, n), BF16),
        compiler_params=_params("parallel", "parallel"),
    )(x, w_gate, w_up)


def rope_tables(seq):
    half = ROT_DIM // 2
    pos = jnp.arange(seq, dtype=F32)
    inv = ROPE_THETA ** (-jnp.arange(0, ROT_DIM, 2, dtype=F32) / ROT_DIM)
    ang = pos[:, None] * inv[None, :]
    cos, sin = jnp.cos(ang), jnp.sin(ang)
    zeros = jnp.zeros((seq, HEAD_DIM - ROT_DIM), F32)
    zhalf = jnp.zeros((seq, half), F32)
    c = jnp.concatenate([cos, cos, zeros + 1.0], axis=-1)
    s1 = jnp.concatenate([zhalf, sin, zeros], axis=-1)
    s2 = jnp.concatenate([-sin, zhalf, zeros], axis=-1)
    return c, s1, s2


PLAIN, ROPE, ROPE_SCALED = 0, 1, 2


def _proj_rope_kernel(x_ref, w_ref, c_ref, s1_ref, s2_ref, o_ref, *, tile_mode):
    mode = tile_mode(pl.program_id(1))
    half = ROT_DIM // 2
    sub = min(2 * HEAD_DIM, o_ref.shape[1])

    def run(rope, scale):
        x = x_ref[...]
        for c0 in range(0, o_ref.shape[1], sub):
            acc = _dot(x, w_ref[:, c0:c0 + sub].astype(BF16))
            if not rope:
                o_ref[:, c0:c0 + sub] = acc.astype(o_ref.dtype)
                continue
            c, s1, s2 = c_ref[...], s1_ref[...], s2_ref[...]
            for h0 in range(0, sub, HEAD_DIM):
                xh = acc[:, h0:h0 + HEAD_DIM]
                r = xh * c + pltpu.roll(xh, half, 1) * s1 + pltpu.roll(xh, HEAD_DIM - half, 1) * s2
                if scale != 1.0:
                    r = r * scale
                o_ref[:, c0 + h0:c0 + h0 + HEAD_DIM] = r.astype(o_ref.dtype)

    @pl.when(mode == ROPE_SCALED)
    def _():
        run(True, Q_SCALE)

    @pl.when(mode == ROPE)
    def _():
        run(True, 1.0)

    @pl.when(mode == PLAIN)
    def _():
        run(False, 1.0)


def proj_rope(x, w, layer, n, tables, seq, tile_mode, tm=1024, tn=512):
    m, kd = x.shape
    tm, tn = min(tm, seq), min(tn, n)
    assert seq % tm == 0 and m % tm == 0 and n % tn == 0
    pos_blocks = seq // tm
    tspec = pl.BlockSpec((tm, HEAD_DIM), lambda i, j: (i % pos_blocks, 0), pipeline_mode=pl.Buffered(1))
    return pl.pallas_call(
        functools.partial(_proj_rope_kernel, tile_mode=tile_mode),
        grid=(m // tm, n // tn),
        in_specs=[pl.BlockSpec((tm, kd), lambda i, j: (i, 0), pipeline_mode=pl.Buffered(1)),
                  pl.BlockSpec((None, kd, tn), lambda i, j: (layer, 0, j)),
                  tspec, tspec, tspec],
        out_specs=pl.BlockSpec((tm, tn), lambda i, j: (i, j)),
        out_shape=jax.ShapeDtypeStruct((m, n), BF16),
        compiler_params=_params("parallel", "parallel"),
    )(x, w, *tables)


def _flash_init_many(slots, s_list, v_list, m_ref, l_ref, acc_ref):
    m = [jnp.max(s, axis=0, keepdims=True) for s in s_list]
    p = [jnp.exp2(s - mm) for s, mm in zip(s_list, m)]
    pv = [_dot_tn(v, pp.astype(v.dtype)) for v, pp in zip(v_list, p)]
    for idx, c in enumerate(slots):
        m_ref[c] = m[idx]
        l_ref[c] = jnp.sum(p[idx], axis=0, keepdims=True)
        acc_ref[c] = pv[idx]


def _flash_update_many(slots, s_list, v_list, m_ref, l_ref, acc_ref):
    m_old = [m_ref[c] for c in slots]
    l_old = [l_ref[c] for c in slots]
    m_new = [jnp.maximum(mo, jnp.max(s, axis=0, keepdims=True)) for mo, s in zip(m_old, s_list)]
    p = [jnp.exp2(s - mn) for s, mn in zip(s_list, m_new)]
    pv = [_dot_tn(v, pp.astype(v.dtype)) for v, pp in zip(v_list, p)]
    for idx, c in enumerate(slots):
        alpha = jnp.exp2(m_old[idx] - m_new[idx])
        m_ref[c] = m_new[idx]
        l_ref[c] = alpha * l_old[idx] + jnp.sum(p[idx], axis=0, keepdims=True)
        acc_ref[c] = alpha * acc_ref[c] + pv[idx]


def _flash_result(slot, l_ref, acc_ref):
    return acc_ref[slot] * (1.0 / l_ref[slot])


def _softmax_pv(s_parts, v):
    e = [jnp.exp2(s - jnp.max(s, axis=0, keepdims=True)) for s in s_parts]
    pv = [_dot_tn(v, p.astype(v.dtype)) for p in e]
    return jnp.concatenate([o * (1.0 / jnp.sum(p, axis=0, keepdims=True)) for o, p in zip(pv, e)], axis=1)


def _pad_rows(n_rows):
    return -(-n_rows // SUBLANES) * SUBLANES


def _rank_before(v, n_rows):
    row = lax.broadcasted_iota(jnp.int32, v.shape, 0)
    rank = jnp.zeros(v.shape, jnp.int32)
    for j2 in range(n_rows):
        other = v[j2:j2 + 1, :]
        rank = rank + jnp.where(other > v, 1, jnp.where(other == v, (row > j2).astype(jnp.int32), 0))
    return rank


def _diff_kernel(lam_ref, g_ref, q_ref, k_ref, v_ref, o_ref, m_ref, l_ref, acc_ref, *, tq, hp, lam_init):
    i = pl.program_id(2)
    w = DIFF_V_DIM
    lp = lam_ref[...]
    lam = (jnp.exp(jnp.sum(lp[0:1] * lp[1:2], axis=-1, keepdims=True))
           - jnp.exp(jnp.sum(lp[2:3] * lp[3:4], axis=-1, keepdims=True)) + lam_init)
    key = lax.broadcasted_iota(jnp.int32, (tq, tq), 0)
    qry = lax.broadcasted_iota(jnp.int32, (tq, tq), 1)

    def scores(c, start):
        cols = slice(c * HEAD_DIM, (c + 1) * HEAD_DIM)
        return _dot_nt(k_ref[0, pl.ds(start, tq), cols], q_ref[0, :, cols])

    own = pl.multiple_of(i * tq, tq)
    s_own = [jnp.where(key <= qry, scores(c, own), NEG) for c in range(2 * hp)]
    v_own = [v_ref[0, pl.ds(own, tq), (c // 2) * w:(c // 2 + 1) * w] for c in range(2 * hp)]
    _flash_init_many(range(2 * hp), s_own, v_own, m_ref, l_ref, acc_ref)

    def body(j, carry):
        start = pl.multiple_of(j * tq, tq)
        s = [scores(c, start) for c in range(2 * hp)]
        v = [v_ref[0, pl.ds(start, tq), (c // 2) * w:(c // 2 + 1) * w] for c in range(2 * hp)]
        _flash_update_many(range(2 * hp), s, v, m_ref, l_ref, acc_ref)
        return carry

    lax.fori_loop(0, i, body, 0)
    for h in range(hp):
        o = (_flash_result(2 * h, l_ref, acc_ref) - lam * _flash_result(2 * h + 1, l_ref, acc_ref)).T
        o_ref[0, :, h * w:(h + 1) * w] = (_rms(o, g_ref[...]) * (1.0 - lam_init)).astype(o_ref.dtype)


def diff_attention(proj, lam_params, subln_g, lam_init, n_heads, q_col, k_col, v_col, tq=256, hp=4):
    b, s, _ = proj.shape
    tq = min(tq, s)
    w = DIFF_V_DIM
    hp = min(hp, n_heads)
    assert n_heads % hp == 0
    gw = hp * w
    qo, ko, vo = q_col // gw, k_col // gw, v_col // gw
    return pl.pallas_call(
        functools.partial(_diff_kernel, tq=tq, hp=hp, lam_init=lam_init),
        grid=(b, n_heads // hp, s // tq),
        in_specs=[pl.BlockSpec((4, HEAD_DIM), lambda bi, h, i: (0, 0)),
                  pl.BlockSpec((1, w), lambda bi, h, i: (0, 0)),
                  pl.BlockSpec((1, tq, gw), lambda bi, h, i: (bi, i, qo + h)),
                  pl.BlockSpec((1, s, gw), lambda bi, h, i: (bi, 0, ko + h)),
                  pl.BlockSpec((1, s, gw), lambda bi, h, i: (bi, 0, vo + h))],
        out_specs=pl.BlockSpec((1, tq, gw), lambda bi, h, i: (bi, i, h)),
        out_shape=jax.ShapeDtypeStruct((b, s, n_heads * w), BF16),
        scratch_shapes=[pltpu.VMEM((2 * hp, 1, tq), F32), pltpu.VMEM((2 * hp, 1, tq), F32),
                        pltpu.VMEM((2 * hp, w, tq), F32)],
        compiler_params=_params("parallel", "parallel", "arbitrary"),
    )(lam_params, subln_g.reshape(1, w), proj, proj, proj)


def _moba_kernel(q_ref, k_ref, v_ref, o_ref, kmean_ref, bias_ref, m_ref, l_ref, acc_ref, *, nb, hp):
    i = pl.program_id(2)
    blk = MOBA_BLOCK
    head = lambda h: slice(h * HEAD_DIM, (h + 1) * HEAD_DIM)

    @pl.when(i == 0)
    def _():
        kmean_ref[...] = jnp.zeros(kmean_ref.shape, F32)
        for h in range(hp):
            for n in range(nb):
                kb = k_ref[0, n * blk:(n + 1) * blk, head(h)].astype(F32)
                kmean_ref[h, n:n + 1, :] = jnp.mean(kb, axis=0, keepdims=True)

    key = lax.broadcasted_iota(jnp.int32, (blk, blk), 0)
    qry = lax.broadcasted_iota(jnp.int32, (blk, blk), 1)
    own = pl.multiple_of(i * blk, blk)
    qs = [q_ref[0, :, head(h)] for h in range(hp)]
    s_own = [jnp.where(key <= qry, _dot_nt(k_ref[0, pl.ds(own, blk), head(h)], qs[h]), NEG) for h in range(hp)]
    v_own = [v_ref[0, pl.ds(own, blk), head(h)] for h in range(hp)]
    biases = []
    for h in range(hp):
        km_hi, km_lo = _split_bf16(kmean_ref[h])
        gate = (_dot_nt(km_hi, qs[h]) + _dot_nt(km_lo, qs[h]))[:_pad_rows(nb)]
        past = lax.broadcasted_iota(jnp.int32, gate.shape, 0) < i
        rank = _rank_before(jnp.where(past, gate, NEG), nb)
        biases.append(jnp.where(past & (rank < MOBA_TOPK), 0.0, NEG))
    for h in range(hp):
        bias_ref[h] = biases[h]
    _flash_init_many(range(hp), s_own, v_own, m_ref, l_ref, acc_ref)

    def body(n, carry):
        start = pl.multiple_of(n * blk, blk)
        sn = [_dot_nt(k_ref[0, pl.ds(start, blk), head(h)], q_ref[0, :, head(h)]) + bias_ref[h, pl.ds(n, 1), :]
              for h in range(hp)]
        vn = [v_ref[0, pl.ds(start, blk), head(h)] for h in range(hp)]
        _flash_update_many(range(hp), sn, vn, m_ref, l_ref, acc_ref)
        return carry

    lax.fori_loop(0, i, body, 0)
    for h in range(hp):
        o_ref[0, :, head(h)] = _flash_result(h, l_ref, acc_ref).T.astype(o_ref.dtype)


def moba_attention(proj, n_heads, q_col, k_col, v_col, hp=8):
    b, s, _ = proj.shape
    blk = MOBA_BLOCK
    hp = min(hp, n_heads)
    assert s % blk == 0 and s // blk <= LANES and n_heads % hp == 0
    nb = s // blk
    gw = hp * HEAD_DIM
    qo, ko, vo = q_col // gw, k_col // gw, v_col // gw
    return pl.pallas_call(
        functools.partial(_moba_kernel, nb=nb, hp=hp),
        grid=(b, n_heads // hp, nb),
        in_specs=[pl.BlockSpec((1, blk, gw), lambda bi, h, i: (bi, i, qo + h)),
                  pl.BlockSpec((1, s, gw), lambda bi, h, i: (bi, 0, ko + h)),
                  pl.BlockSpec((1, s, gw), lambda bi, h, i: (bi, 0, vo + h))],
        out_specs=pl.BlockSpec((1, blk, gw), lambda bi, h, i: (bi, i, h)),
        out_shape=jax.ShapeDtypeStruct((b, s, n_heads * HEAD_DIM), BF16),
        scratch_shapes=[pltpu.VMEM((hp, LANES, HEAD_DIM), F32), pltpu.VMEM((hp, _pad_rows(nb), blk), F32),
                        pltpu.VMEM((hp, 1, blk), F32), pltpu.VMEM((hp, 1, blk), F32),
                        pltpu.VMEM((hp, HEAD_DIM, blk), F32)],
        compiler_params=_params("parallel", "parallel", "arbitrary"),
    )(proj, proj, proj)


def _compress_kernel(t_ref, pe_ref, w1_ref, w2_ref, o_ref):
    t = t_ref[0, 0].astype(F32)
    half = t.shape[1]
    top = (t + pe_ref[0, :, :half]).astype(BF16)
    bot = (t + pe_ref[0, :, half:]).astype(BF16)
    a = _dot(top, w1_ref[0, :half, :].astype(BF16))
    bm = _dot(bot, w1_ref[0, half:, :].astype(BF16))
    hidden = a + pltpu.roll(bm, bm.shape[0] - 1, 0)
    act = hidden * jax.nn.sigmoid(hidden)
    o_ref[0, 0] = _dot(act.astype(BF16), w2_ref[0].astype(BF16)).astype(o_ref.dtype)


def nsa_compress(t16, pe_flat, w1, w2):
    b, g2, nc, width = t16.shape
    g = g2 // 2
    hid = w1.shape[-1]
    return pl.pallas_call(
        _compress_kernel,
        grid=(b, g2),
        in_specs=[pl.BlockSpec((1, 1, nc, width), lambda bi, j: (bi, j, 0, 0)),
                  pl.BlockSpec((1, 1, 2 * width), lambda bi, j: (j // g, 0, 0)),
                  pl.BlockSpec((1, 2 * width, hid), lambda bi, j: (j // g, 0, 0)),
                  pl.BlockSpec((1, hid, HEAD_DIM), lambda bi, j: (j // g, 0, 0))],
        out_specs=pl.BlockSpec((1, 1, nc, HEAD_DIM), lambda bi, j: (bi, j, 0, 0)),
        out_shape=jax.ShapeDtypeStruct((b, g2, nc, HEAD_DIM), BF16),
        compiler_params=_params("parallel", "parallel"),
    )(t16, pe_flat, w1, w2)


def _lanes_x(a, n):
    return jnp.concatenate([a] * n, axis=1)


def _nsa_kernel(q_ref, ks_ref, vs_ref, kw_ref, vw_ref, kc_ref, vc_ref, gt_ref, cover_ref, expand_ref,
                o_ref, part_ref, m_ref, l_ref, acc_ref, *, tq, tk, n_sel):
    i = pl.program_id(2)
    z_heads = NSA_GROUP
    nc = kc_ref.shape[2]
    q = jnp.concatenate([q_ref[0, :, z * HEAD_DIM:(z + 1) * HEAD_DIM] for z in range(z_heads)], axis=0)
    qpos = i * tq + lax.broadcasted_iota(jnp.int32, (1, tq), 1)

    cmp_end = lax.broadcasted_iota(jnp.int32, (nc, tq), 0) * NSA_CMP_STRIDE + (NSA_CMP_LEN - 1)
    cmp_ok = _lanes_x(jnp.where(cmp_end <= qpos, 1.0, 0.0), z_heads)
    sc = jnp.where(cmp_ok > 0.5, _dot_nt(kc_ref[0, 0], q), NEG)
    pc = jnp.exp2(sc - jnp.max(sc, axis=0, keepdims=True))
    pc = pc * (1.0 / jnp.sum(pc, axis=0, keepdims=True)) * cmp_ok
    o_cmp = _dot_tn(vc_ref[0, 0], pc.astype(BF16))

    p_sum = pc[:, 0:tq]
    for z in range(1, z_heads):
        p_sum = p_sum + pc[:, z * tq:(z + 1) * tq]
    p_hi, p_lo = _split_bf16(p_sum)
    imp = (_dot(cover_ref[...], p_hi) + _dot(cover_ref[...], p_lo))[:_pad_rows(n_sel)]
    row = lax.broadcasted_iota(jnp.int32, imp.shape, 0)
    qblk = qpos // NSA_SEL_BLOCK
    forced = (row == 0) | (row == qblk) | (row == qblk - 1)
    allowed = row <= qblk
    imp = jnp.where(forced, BIG, jnp.where(allowed, imp, NEG))
    rank = _rank_before(imp, n_sel)
    block_bias = jnp.where(allowed & (rank < NSA_SEL_TOPN), 0.0, NEG)
    block_bias = jnp.concatenate(
        [block_bias, jnp.zeros((LANES - block_bias.shape[0], tq), F32)], axis=0).T.astype(BF16)
    q2 = jnp.concatenate([q, jnp.concatenate([block_bias] * z_heads, axis=0)], axis=1)
    span = NSA_WINDOW + tq
    w_start = pl.multiple_of(jnp.maximum(i * tq - NSA_WINDOW, 0), tq)
    dist = qpos - (w_start + lax.broadcasted_iota(jnp.int32, (span, tq), 0))
    w_bias = jnp.where((dist >= 0) & (dist < NSA_WINDOW), 0.0, NEG)
    kw = kw_ref[0, pl.ds(w_start, span), :]
    half_rows = z_heads * tq // 2
    sw = [_dot_nt(kw, q[p * half_rows:(p + 1) * half_rows]) + _lanes_x(w_bias, z_heads // 2) for p in range(2)]
    o_win = _softmax_pv(sw, vw_ref[0, pl.ds(w_start, span), :])
    gt = jax.nn.sigmoid(gt_ref[...])
    part_ref[...] = jnp.concatenate(
        [gt[3 * z:3 * z + 1] * o_cmp[:, z * tq:(z + 1) * tq] + gt[3 * z + 2:3 * z + 3] * o_win[:, z * tq:(z + 1) * tq]
         for z in range(z_heads)], axis=1)

    n_split = NSA_LANE_SPLIT
    part_w = z_heads * tq // n_split
    halves = tuple(slice(p * part_w, (p + 1) * part_w) for p in range(n_split))

    def sel_pair(t, causal):
        s, v = [], []
        for c in range(2):
            start = (2 * t + c) * tk
            if not isinstance(start, int):
                start = pl.multiple_of(start, tk)
            k2 = jnp.concatenate([ks_ref[0, pl.ds(start, tk), :], expand_ref[pl.ds(start, tk), :]], axis=1)
            sc = _dot_nt(k2, q2)
            if causal:
                kpos = start + lax.broadcasted_iota(jnp.int32, (tk, tq), 0)
                sc = sc + _lanes_x(jnp.where(kpos <= qpos, 0.0, NEG), z_heads)
            vt = vs_ref[0, pl.ds(start, tk), :]
            for lanes in halves:
                s.append(sc[:, lanes])
                v.append(vt)
        return s, v

    n_pairs = ((i + 1) * tq + 2 * tk - 1) // (2 * tk)
    s0, v0 = sel_pair(0, True)
    _flash_init_many(range(2 * n_split), s0, v0, m_ref, l_ref, acc_ref)

    def sel_body(t, carry):
        st, vt = sel_pair(t, False)
        _flash_update_many(range(2 * n_split), st, vt, m_ref, l_ref, acc_ref)
        return carry

    lax.fori_loop(1, n_pairs - 1, sel_body, 0)

    @pl.when(n_pairs > 1)
    def _():
        st, vt = sel_pair(n_pairs - 1, True)
        _flash_update_many(range(2 * n_split), st, vt, m_ref, l_ref, acc_ref)

    merged = []
    for a in range(n_split):
        b = n_split + a
        m_all = jnp.maximum(m_ref[a], m_ref[b])
        w0, w1 = jnp.exp2(m_ref[a] - m_all), jnp.exp2(m_ref[b] - m_all)
        merged.append((w0 * acc_ref[a] + w1 * acc_ref[b]) * (1.0 / (w0 * l_ref[a] + w1 * l_ref[b])))
    o_sel = jnp.concatenate(merged, axis=1)

    for z in range(z_heads):
        ls = slice(z * tq, (z + 1) * tq)
        o = part_ref[:, ls] + jax.nn.sigmoid(gt_ref[3 * z + 1:3 * z + 2, :]) * o_sel[:, ls]
        o_ref[0, :, z * HEAD_DIM:(z + 1) * HEAD_DIM] = o.T.astype(o_ref.dtype)


def nsa_attention(proj, gates_t, cmp_kv, n_groups, q_col, ks_col, vs_col, kw_col, vw_col, tq=128, tk=256):
    b, s, _ = proj.shape
    tq, tk = min(tq, s), min(tk, s)
    assert s % (2 * tk) == 0 and tk % tq == 0 and NSA_WINDOW % tq == 0 and tq % NSA_SEL_BLOCK == 0
    assert s >= NSA_WINDOW + tq
    nc = cmp_kv.shape[2]
    n_sel = s // NSA_SEL_BLOCK
    assert nc * NSA_CMP_STRIDE == s and n_sel <= LANES
    c_lo = np.arange(nc) * NSA_CMP_STRIDE
    j_lo = np.arange(LANES) * NSA_SEL_BLOCK
    cover_t = ((c_lo[None, :] < j_lo[:, None] + NSA_SEL_BLOCK) & (c_lo[None, :] + NSA_CMP_LEN > j_lo[:, None])
               & (np.arange(LANES)[:, None] < n_sel) & (c_lo[None, :] + NSA_CMP_LEN <= s))
    expand_t = (np.arange(s)[:, None] // NSA_SEL_BLOCK == np.arange(LANES)[None, :])
    gw = NSA_GROUP * HEAD_DIM
    qo = q_col // gw
    kso, vso, kwo, vwo = (c // HEAD_DIM for c in (ks_col, vs_col, kw_col, vw_col))
    kv = lambda off: pl.BlockSpec((1, s, HEAD_DIM), lambda bi, g, i: (bi, 0, off + g))
    q_tiles = s // tq
    rows = NSA_GROUP * tq
    return pl.pallas_call(
        functools.partial(_nsa_kernel, tq=tq, tk=tk, n_sel=n_sel),
        grid=(b, n_groups, q_tiles),
        in_specs=[pl.BlockSpec((1, tq, gw), lambda bi, g, i: (bi, i, qo + g)),
                  kv(kso), kv(vso), kv(kwo), kv(vwo),
                  pl.BlockSpec((1, 1, nc, HEAD_DIM), lambda bi, g, i: (bi, g, 0, 0)),
                  pl.BlockSpec((1, 1, nc, HEAD_DIM), lambda bi, g, i: (bi, n_groups + g, 0, 0)),
                  pl.BlockSpec((NSA_GATE_PAD, tq), lambda bi, g, i: (g, bi * q_tiles + i)),
                  pl.BlockSpec((LANES, nc), lambda bi, g, i: (0, 0)),
                  pl.BlockSpec((s, LANES), lambda bi, g, i: (0, 0))],
        out_specs=pl.BlockSpec((1, tq, gw), lambda bi, g, i: (bi, i, g)),
        out_shape=jax.ShapeDtypeStruct((b, s, n_groups * gw), BF16),
        scratch_shapes=[pltpu.VMEM((HEAD_DIM, rows), F32),
                        pltpu.VMEM((2 * NSA_LANE_SPLIT, 1, rows // NSA_LANE_SPLIT), F32),
                        pltpu.VMEM((2 * NSA_LANE_SPLIT, 1, rows // NSA_LANE_SPLIT), F32),
                        pltpu.VMEM((2 * NSA_LANE_SPLIT, HEAD_DIM, rows // NSA_LANE_SPLIT), F32)],
        compiler_params=_params("parallel", "parallel", "arbitrary"),
    )(proj, proj, proj, proj, proj, cmp_kv, cmp_kv, gates_t, jnp.asarray(cover_t, BF16),
      jnp.asarray(expand_t, BF16))


def _even_tile_mode(tn, d_model):
    seg_tiles = (d_model // 2) // tn
    return lambda j: ROPE_SCALED - (j // seg_tiles) % 3


def even_mixer(hn, w_in, w_out, e, lam_params, subln_g, lam_init, tables, batch, seq):
    d_model = hn.shape[1]
    seg = d_model // 2
    tn = min(1024, seg)
    proj = proj_rope(hn, w_in, e, 6 * seg, tables, seq, _even_tile_mode(tn, d_model), tn=tn)
    proj = proj.reshape(batch, seq, -1)
    o_diff = diff_attention(proj, lam_params, subln_g, lam_init, seg // DIFF_V_DIM, 0, seg, 2 * seg)
    o_moba = moba_attention(proj, seg // HEAD_DIM, 3 * seg, 4 * seg, 5 * seg)
    o = jnp.concatenate([o_diff, o_moba], axis=-1).reshape(batch * seq, d_model)
    return matmul(o, w_out, e, F32)


def _odd_tile_mode(tn, d_model):
    slab_tiles = (d_model // 8) // tn
    q_tiles = d_model // tn
    return lambda j: jnp.where(j < q_tiles, ROPE_SCALED, ROPE - ((j - q_tiles) // slab_tiles) % 2)


def odd_mixer(hn, w_in, w_out, o_idx, cmp_pe, cmp_w1, cmp_w2, tables, batch, seq):
    d_model = hn.shape[1]
    g = d_model // (NSA_GROUP * HEAD_DIM)
    kvd = g * HEAD_DIM
    n_main = d_model + 6 * kvd
    tn = min(512, kvd)
    proj = proj_rope(hn, w_in, o_idx, n_main, tables, seq, _odd_tile_mode(tn, d_model), tn=tn)
    proj = proj.reshape(batch, seq, n_main)
    n_gate = 3 * NSA_GROUP
    w_gate_t = w_in[o_idx, :, n_main:].T.reshape(g, n_gate, d_model)
    w_gate_t = jnp.pad(w_gate_t, ((0, 0), (0, NSA_GATE_PAD - n_gate), (0, 0))).reshape(g * NSA_GATE_PAD, d_model)
    gates_t = matmul_nt(w_gate_t, hn)
    nc = seq // NSA_CMP_STRIDE
    t16 = proj[:, :, d_model:d_model + 2 * kvd].reshape(batch, seq, 2 * g, HEAD_DIM)
    t16 = t16.transpose(0, 2, 1, 3).reshape(batch, 2 * g, nc, NSA_CMP_STRIDE * HEAD_DIM)
    cmp_kv = nsa_compress(t16, cmp_pe.reshape(2, 1, NSA_CMP_LEN * HEAD_DIM), cmp_w1, cmp_w2)
    o = nsa_attention(proj, gates_t, cmp_kv, g, 0, d_model + 2 * kvd, d_model + 3 * kvd, d_model + 4 * kvd,
                      d_model + 5 * kvd)
    return matmul(o.reshape(batch * seq, d_model), w_out, o_idx, F32)


def ffn(hn, w_gate, w_up, w_down, layer):
    return matmul(swiglu_up(hn, w_gate, w_up, layer), w_down, layer, F32, tn=256)


def kernel(x, norm_g, ffn_a_gate, ffn_a_up, ffn_a_down, ffn_b_gate, ffn_b_up, ffn_b_down, ev_w_in, ev_w_out, ev_lambda, ev_subln_g, od_w_in, od_w_out, od_cmp_pe, od_cmp_w1, od_cmp_w2):
    batch, seq, d_model = x.shape
    depth = norm_g.shape[0]
    tables = rope_tables(seq)
    h = x.reshape(batch * seq, d_model)
    hn = rmsnorm(h, norm_g[0, 0])
    for layer in range(depth):
        g = norm_g[layer]
        m = ffn(hn, ffn_a_gate, ffn_a_up, ffn_a_down, layer)
        h, hn = residual_norm(h, m, g[1], g[2], 0.5)
        if layer % 2 == 0:
            e = layer // 2
            lam_init = 0.8 - 0.6 * math.exp(-0.3 * layer)
            m = even_mixer(hn, ev_w_in, ev_w_out, e, ev_lambda[e], ev_subln_g[e], lam_init, tables, batch, seq)
        else:
            o = layer // 2
            m = odd_mixer(hn, od_w_in, od_w_out, o, od_cmp_pe[o], od_cmp_w1[o], od_cmp_w2[o], tables, batch, seq)
        h, hn = residual_norm(h, m, g[3], g[4], 1.0)
        m = ffn(hn, ffn_b_gate, ffn_b_up, ffn_b_down, layer)
        g_next = norm_g[layer + 1, 0] if layer + 1 < depth else None
        h, hn = residual_norm(h, m, g[5], g_next, 0.5)
    return h.reshape(batch, seq, d_model)
```

```python
import functools
import math

import numpy as np
import jax
import jax.numpy as jnp
from jax import lax
from jax.experimental import pallas as pl
from jax.experimental.pallas import tpu as pltpu

F32 = jnp.float32
BF16 = jnp.bfloat16

HEAD_DIM = 128
ROT_DIM = HEAD_DIM // 4
ROPE_THETA = 500000.0
RMS_EPS = 1e-6
NEG = -1e30
BIG = 1e30
Q_SCALE = HEAD_DIM ** -0.5 * math.log2(math.e)

DIFF_V_DIM = 2 * HEAD_DIM
MOBA_BLOCK = 256
MOBA_TOPK = 3
NSA_GROUP = 8
NSA_CMP_LEN = 32
NSA_CMP_STRIDE = 16
NSA_SEL_BLOCK = 64
NSA_SEL_TOPN = 16
NSA_WINDOW = 512
NSA_GATE_PAD = 128
NSA_LANE_SPLIT = 2

LANES = 128
SUBLANES = 8
VMEM_LIMIT_BYTES = 56 * 1024 * 1024


def _params(*semantics):
    return pltpu.CompilerParams(dimension_semantics=semantics, vmem_limit_bytes=VMEM_LIMIT_BYTES)


def _dot(a, b):
    return jnp.dot(a, b, preferred_element_type=F32)


def _dot_nt(a, b):
    return lax.dot_general(a, b, (((1,), (1,)), ((), ())), preferred_element_type=F32)


def _dot_tn(a, b):
    return lax.dot_general(a, b, (((0,), (0,)), ((), ())), preferred_element_type=F32)


def _split_bf16(x):
    hi = x.astype(BF16)
    lo = (x - hi.astype(F32)).astype(BF16)
    return hi, lo


def _rms(x, g):
    return x * lax.rsqrt(jnp.mean(x * x, axis=-1, keepdims=True) + RMS_EPS) * g


def _rmsnorm_kernel(x_ref, g_ref, o_ref):
    o_ref[...] = _rms(x_ref[...], g_ref[...]).astype(o_ref.dtype)


def rmsnorm(x, g, tm=256):
    m, d = x.shape
    tm = min(tm, m)
    return pl.pallas_call(
        _rmsnorm_kernel,
        grid=(m // tm,),
        in_specs=[pl.BlockSpec((tm, d), lambda i: (i, 0)), pl.BlockSpec((1, d), lambda i: (0, 0))],
        out_specs=pl.BlockSpec((tm, d), lambda i: (i, 0)),
        out_shape=jax.ShapeDtypeStruct((m, d), BF16),
        compiler_params=_params("parallel"),
    )(x, g.reshape(1, d))


def _residual_kernel(h_ref, m_ref, gpost_ref, gnext_ref, h_out_ref, hn_out_ref, *, coef):
    y = _rms(m_ref[...], gpost_ref[...])
    if coef != 1.0:
        y = coef * y
    h = h_ref[...] + y
    h_out_ref[...] = h
    hn_out_ref[...] = _rms(h, gnext_ref[...]).astype(hn_out_ref.dtype)


def _residual_last_kernel(h_ref, m_ref, gpost_ref, h_out_ref, *, coef):
    y = _rms(m_ref[...], gpost_ref[...])
    if coef != 1.0:
        y = coef * y
    h_out_ref[...] = h_ref[...] + y


def residual_norm(h, mix, g_post, g_next, coef, tm=256):
    m, d = h.shape
    tm = min(tm, m)
    row = pl.BlockSpec((tm, d), lambda i: (i, 0))
    vec = pl.BlockSpec((1, d), lambda i: (0, 0))
    if g_next is None:
        return pl.pallas_call(
            functools.partial(_residual_last_kernel, coef=coef),
            grid=(m // tm,),
            in_specs=[row, row, vec],
            out_specs=row,
            out_shape=jax.ShapeDtypeStruct((m, d), F32),
            compiler_params=_params("parallel"),
        )(h, mix, g_post.reshape(1, d)), None
    return pl.pallas_call(
        functools.partial(_residual_kernel, coef=coef),
        grid=(m // tm,),
        in_specs=[row, row, vec, vec],
        out_specs=[row, row],
        out_shape=[jax.ShapeDtypeStruct((m, d), F32), jax.ShapeDtypeStruct((m, d), BF16)],
        compiler_params=_params("parallel"),
    )(h, mix, g_post.reshape(1, d), g_next.reshape(1, d))


def _mm_kernel(x_ref, w_ref, o_ref):
    o_ref[...] = _dot(x_ref[...], w_ref[...].astype(BF16)).astype(o_ref.dtype)


def matmul(x, w, layer, out_dtype, tm=1024, tn=512):
    m, kd = x.shape
    n = w.shape[2]
    tm, tn = min(tm, m), min(tn, n)
    return pl.pallas_call(
        _mm_kernel,
        grid=(m // tm, n // tn),
        in_specs=[pl.BlockSpec((tm, kd), lambda i, j: (i, 0)),
                  pl.BlockSpec((None, kd, tn), lambda i, j: (layer, 0, j))],
        out_specs=pl.BlockSpec((tm, tn), lambda i, j: (i, j)),
        out_shape=jax.ShapeDtypeStruct((m, n), out_dtype),
        compiler_params=_params("parallel", "parallel"),
    )(x, w)


def _mm_nt_kernel(w_ref, x_ref, o_ref):
    o_ref[...] = _dot_nt(w_ref[...].astype(BF16), x_ref[...])


def matmul_nt(w_t, x, tm=1024):
    n, kd = w_t.shape
    m = x.shape[0]
    tm = min(tm, m)
    return pl.pallas_call(
        _mm_nt_kernel,
        grid=(m // tm,),
        in_specs=[pl.BlockSpec((n, kd), lambda i: (0, 0)), pl.BlockSpec((tm, kd), lambda i: (i, 0))],
        out_specs=pl.BlockSpec((n, tm), lambda i: (0, i)),
        out_shape=jax.ShapeDtypeStruct((n, m), F32),
        compiler_params=_params("parallel"),
    )(w_t, x)


def _swiglu_kernel(x_ref, wg_ref, wu_ref, o_ref):
    x = x_ref[...]
    g = _dot(x, wg_ref[...].astype(BF16))
    u = _dot(x, wu_ref[...].astype(BF16))
    o_ref[...] = (g * jax.nn.sigmoid(g) * u).astype(o_ref.dtype)


def swiglu_up(x, w_gate, w_up, layer, tm=1024, tn=256):
    m, kd = x.shape
    n = w_gate.shape[2]
    tm, tn = min(tm, m), min(tn, n)
    wspec = pl.BlockSpec((None, kd, tn), lambda i, j: (layer, 0, j))
    return pl.pallas_call(
        _swiglu_kernel,
        grid=(m // tm, n // tn),
        in_specs=[pl.BlockSpec((tm, kd), lambda i, j: (i, 0)), wspec, wspec],
        out_specs=pl.BlockSpec((tm, tn), lambda i, j: (i, j)),
        out_shape=jax.ShapeDtypeStruct((m, n), BF16),
        compiler_params=_params("parallel", "parallel"),
    )(x, w_gate, w_up)


def rope_tables(seq):
    half = ROT_DIM // 2
    pos = jnp.arange(seq, dtype=F32)
    inv = ROPE_THETA ** (-jnp.arange(0, ROT_DIM, 2, dtype=F32) / ROT_DIM)
    ang = pos[:, None] * inv[None, :]
    cos, sin = jnp.cos(ang), jnp.sin(ang)
    zeros = jnp.zeros((seq, HEAD_DIM - ROT_DIM), F32)
    zhalf = jnp.zeros((seq, half), F32)
    c = jnp.concatenate([cos, cos, zeros + 1.0], axis=-1)
    s1 = jnp.concatenate([zhalf, sin, zeros], axis=-1)
    s2 = jnp.concatenate([-sin, zhalf, zeros], axis=-1)
    return c, s1, s2


PLAIN, ROPE, ROPE_SCALED = 0, 1, 2


def _proj_rope_kernel(x_ref, w_ref, c_ref, s1_ref, s2_ref, o_ref, *, tile_mode):
    mode = tile_mode(pl.program_id(1))
    half = ROT_DIM // 2
    sub = min(2 * HEAD_DIM, o_ref.shape[1])

    def run(rope, scale):
        x = x_ref[...]
        for c0 in range(0, o_ref.shape[1], sub):
            acc = _dot(x, w_ref[:, c0:c0 + sub].astype(BF16))
            if not rope:
                o_ref[:, c0:c0 + sub] = acc.astype(o_ref.dtype)
                continue
            c, s1, s2 = c_ref[...], s1_ref[...], s2_ref[...]
            for h0 in range(0, sub, HEAD_DIM):
                xh = acc[:, h0:h0 + HEAD_DIM]
                r = xh * c + pltpu.roll(xh, half, 1) * s1 + pltpu.roll(xh, HEAD_DIM - half, 1) * s2
                if scale != 1.0:
                    r = r * scale
                o_ref[:, c0 + h0:c0 + h0 + HEAD_DIM] = r.astype(o_ref.dtype)

    @pl.when(mode == ROPE_SCALED)
    def _():
        run(True, Q_SCALE)

    @pl.when(mode == ROPE)
    def _():
        run(True, 1.0)

    @pl.when(mode == PLAIN)
    def _():
        run(False, 1.0)


def proj_rope(x, w, layer, n, tables, seq, tile_mode, tm=1024, tn=512):
    m, kd = x.shape
    tm, tn = min(tm, seq), min(tn, n)
    assert seq % tm == 0 and m % tm == 0 and n % tn == 0
    pos_blocks = seq // tm
    row_bytes = tm * kd * 2 + 3 * tm * HEAD_DIM * 4
    sub = min(2 * HEAD_DIM, tn)
    stream_bytes = 2 * (kd * tn * 4 + tm * tn * 2) + kd * sub * 2 + tm * sub * 4
    rows_mode = {} if 2 * row_bytes + stream_bytes <= VMEM_LIMIT_BYTES else {"pipeline_mode": pl.Buffered(1)}
    tspec = pl.BlockSpec((tm, HEAD_DIM), lambda i, j: (i % pos_blocks, 0), **rows_mode)
    return pl.pallas_call(
        functools.partial(_proj_rope_kernel, tile_mode=tile_mode),
        grid=(m // tm, n // tn),
        in_specs=[pl.BlockSpec((tm, kd), lambda i, j: (i, 0), **rows_mode),
                  pl.BlockSpec((None, kd, tn), lambda i, j: (layer, 0, j)),
                  tspec, tspec, tspec],
        out_specs=pl.BlockSpec((tm, tn), lambda i, j: (i, j)),
        out_shape=jax.ShapeDtypeStruct((m, n), BF16),
        compiler_params=_params("parallel", "parallel"),
    )(x, w, *tables)


def _flash_init_many(slots, s_list, v_list, m_ref, l_ref, acc_ref):
    m = [jnp.max(s, axis=0, keepdims=True) for s in s_list]
    p = [jnp.exp2(s - mm) for s, mm in zip(s_list, m)]
    pv = [_dot_tn(v, pp.astype(v.dtype)) for v, pp in zip(v_list, p)]
    for idx, c in enumerate(slots):
        m_ref[c] = m[idx]
        l_ref[c] = jnp.sum(p[idx], axis=0, keepdims=True)
        acc_ref[c] = pv[idx]


def _flash_update_many(slots, s_list, v_list, m_ref, l_ref, acc_ref):
    m_old = [m_ref[c] for c in slots]
    l_old = [l_ref[c] for c in slots]
    m_new = [jnp.maximum(mo, jnp.max(s, axis=0, keepdims=True)) for mo, s in zip(m_old, s_list)]
    p = [jnp.exp2(s - mn) for s, mn in zip(s_list, m_new)]
    pv = [_dot_tn(v, pp.astype(v.dtype)) for v, pp in zip(v_list, p)]
    for idx, c in enumerate(slots):
        alpha = jnp.exp2(m_old[idx] - m_new[idx])
        m_ref[c] = m_new[idx]
        l_ref[c] = alpha * l_old[idx] + jnp.sum(p[idx], axis=0, keepdims=True)
        acc_ref[c] = alpha * acc_ref[c] + pv[idx]


def _flash_result(slot, l_ref, acc_ref):
    return acc_ref[slot] * (1.0 / l_ref[slot])


def _softmax_pv(s_parts, v):
    e = [jnp.exp2(s - jnp.max(s, axis=0, keepdims=True)) for s in s_parts]
    pv = [_dot_tn(v, p.astype(v.dtype)) for p in e]
    return jnp.concatenate([o * (1.0 / jnp.sum(p, axis=0, keepdims=True)) for o, p in zip(pv, e)], axis=1)


def _pad_rows(n_rows):
    return -(-n_rows // SUBLANES) * SUBLANES


def _rank_before(v, n_rows):
    row = lax.broadcasted_iota(jnp.int32, v.shape, 0)
    rank = jnp.zeros(v.shape, jnp.int32)
    for j2 in range(n_rows):
        other = v[j2:j2 + 1, :]
        rank = rank + jnp.where(other > v, 1, jnp.where(other == v, (row > j2).astype(jnp.int32), 0))
    return rank


def _diff_kernel(lam_ref, g_ref, q_ref, k_ref, v_ref, o_ref, m_ref, l_ref, acc_ref, *, tq, hp, lam_init):
    i = pl.program_id(2)
    w = DIFF_V_DIM
    lp = lam_ref[...]
    lam = (jnp.exp(jnp.sum(lp[0:1] * lp[1:2], axis=-1, keepdims=True))
           - jnp.exp(jnp.sum(lp[2:3] * lp[3:4], axis=-1, keepdims=True)) + lam_init)
    key = lax.broadcasted_iota(jnp.int32, (tq, tq), 0)
    qry = lax.broadcasted_iota(jnp.int32, (tq, tq), 1)

    def scores(c, start):
        cols = slice(c * HEAD_DIM, (c + 1) * HEAD_DIM)
        return _dot_nt(k_ref[0, pl.ds(start, tq), cols], q_ref[0, :, cols])

    own = pl.multiple_of(i * tq, tq)
    s_own = [jnp.where(key <= qry, scores(c, own), NEG) for c in range(2 * hp)]
    v_own = [v_ref[0, pl.ds(own, tq), (c // 2) * w:(c // 2 + 1) * w] for c in range(2 * hp)]
    _flash_init_many(range(2 * hp), s_own, v_own, m_ref, l_ref, acc_ref)

    def body(j, carry):
        start = pl.multiple_of(j * tq, tq)
        s = [scores(c, start) for c in range(2 * hp)]
        v = [v_ref[0, pl.ds(start, tq), (c // 2) * w:(c // 2 + 1) * w] for c in range(2 * hp)]
        _flash_update_many(range(2 * hp), s, v, m_ref, l_ref, acc_ref)
        return carry

    lax.fori_loop(0, i, body, 0)
    for h in range(hp):
        o = (_flash_result(2 * h, l_ref, acc_ref) - lam * _flash_result(2 * h + 1, l_ref, acc_ref)).T
        o_ref[0, :, h * w:(h + 1) * w] = (_rms(o, g_ref[...]) * (1.0 - lam_init)).astype(o_ref.dtype)


def diff_attention(proj, lam_params, subln_g, lam_init, n_heads, q_col, k_col, v_col, tq=256, hp=4):
    b, s, _ = proj.shape
    tq = min(tq, s)
    w = DIFF_V_DIM
    hp = min(hp, n_heads)
    assert n_heads % hp == 0
    gw = hp * w
    qo, ko, vo = q_col // gw, k_col // gw, v_col // gw
    return pl.pallas_call(
        functools.partial(_diff_kernel, tq=tq, hp=hp, lam_init=lam_init),
        grid=(b, n_heads // hp, s // tq),
        in_specs=[pl.BlockSpec((4, HEAD_DIM), lambda bi, h, i: (0, 0)),
                  pl.BlockSpec((1, w), lambda bi, h, i: (0, 0)),
                  pl.BlockSpec((1, tq, gw), lambda bi, h, i: (bi, i, qo + h)),
                  pl.BlockSpec((1, s, gw), lambda bi, h, i: (bi, 0, ko + h)),
                  pl.BlockSpec((1, s, gw), lambda bi, h, i: (bi, 0, vo + h))],
        out_specs=pl.BlockSpec((1, tq, gw), lambda bi, h, i: (bi, i, h)),
        out_shape=jax.ShapeDtypeStruct((b, s, n_heads * w), BF16),
        scratch_shapes=[pltpu.VMEM((2 * hp, 1, tq), F32), pltpu.VMEM((2 * hp, 1, tq), F32),
                        pltpu.VMEM((2 * hp, w, tq), F32)],
        compiler_params=_params("parallel", "parallel", "arbitrary"),
    )(lam_params, subln_g.reshape(1, w), proj, proj, proj)


def _moba_kernel(q_ref, k_ref, v_ref, o_ref, kmean_ref, bias_ref, m_ref, l_ref, acc_ref, *, nb, hp):
    i = pl.program_id(2)
    blk = MOBA_BLOCK
    head = lambda h: slice(h * HEAD_DIM, (h + 1) * HEAD_DIM)

    @pl.when(i == 0)
    def _():
        kmean_ref[...] = jnp.zeros(kmean_ref.shape, F32)
        for h in range(hp):
            for n in range(nb):
                kb = k_ref[0, n * blk:(n + 1) * blk, head(h)].astype(F32)
                kmean_ref[h, n:n + 1, :] = jnp.mean(kb, axis=0, keepdims=True)

    key = lax.broadcasted_iota(jnp.int32, (blk, blk), 0)
    qry = lax.broadcasted_iota(jnp.int32, (blk, blk), 1)
    own = pl.multiple_of(i * blk, blk)
    qs = [q_ref[0, :, head(h)] for h in range(hp)]
    s_own = [jnp.where(key <= qry, _dot_nt(k_ref[0, pl.ds(own, blk), head(h)], qs[h]), NEG) for h in range(hp)]
    v_own = [v_ref[0, pl.ds(own, blk), head(h)] for h in range(hp)]
    biases = []
    for h in range(hp):
        km_hi, km_lo = _split_bf16(kmean_ref[h])
        gate = (_dot_nt(km_hi, qs[h]) + _dot_nt(km_lo, qs[h]))[:_pad_rows(nb)]
        past = lax.broadcasted_iota(jnp.int32, gate.shape, 0) < i
        rank = _rank_before(jnp.where(past, gate, NEG), nb)
        biases.append(jnp.where(past & (rank < MOBA_TOPK), 0.0, NEG))
    for h in range(hp):
        bias_ref[h] = biases[h]
    _flash_init_many(range(hp), s_own, v_own, m_ref, l_ref, acc_ref)

    def body(n, carry):
        start = pl.multiple_of(n * blk, blk)
        sn = [_dot_nt(k_ref[0, pl.ds(start, blk), head(h)], q_ref[0, :, head(h)]) + bias_ref[h, pl.ds(n, 1), :]
              for h in range(hp)]
        vn = [v_ref[0, pl.ds(start, blk), head(h)] for h in range(hp)]
        _flash_update_many(range(hp), sn, vn, m_ref, l_ref, acc_ref)
        return carry

    lax.fori_loop(0, i, body, 0)
    for h in range(hp):
        o_ref[0, :, head(h)] = _flash_result(h, l_ref, acc_ref).T.astype(o_ref.dtype)


def moba_attention(proj, n_heads, q_col, k_col, v_col, hp=8):
    b, s, _ = proj.shape
    blk = MOBA_BLOCK
    hp = min(hp, n_heads)
    assert s % blk == 0 and s // blk <= LANES and n_heads % hp == 0
    nb = s // blk
    gw = hp * HEAD_DIM
    qo, ko, vo = q_col // gw, k_col // gw, v_col // gw
    return pl.pallas_call(
        functools.partial(_moba_kernel, nb=nb, hp=hp),
        grid=(b, n_heads // hp, nb),
        in_specs=[pl.BlockSpec((1, blk, gw), lambda bi, h, i: (bi, i, qo + h)),
                  pl.BlockSpec((1, s, gw), lambda bi, h, i: (bi, 0, ko + h)),
                  pl.BlockSpec((1, s, gw), lambda bi, h, i: (bi, 0, vo + h))],
        out_specs=pl.BlockSpec((1, blk, gw), lambda bi, h, i: (bi, i, h)),
        out_shape=jax.ShapeDtypeStruct((b, s, n_heads * HEAD_DIM), BF16),
        scratch_shapes=[pltpu.VMEM((hp, LANES, HEAD_DIM), F32), pltpu.VMEM((hp, _pad_rows(nb), blk), F32),
                        pltpu.VMEM((hp, 1, blk), F32), pltpu.VMEM((hp, 1, blk), F32),
                        pltpu.VMEM((hp, HEAD_DIM, blk), F32)],
        compiler_params=_params("parallel", "parallel", "arbitrary"),
    )(proj, proj, proj)


def _compress_kernel(t_ref, pe_ref, w1_ref, w2_ref, o_ref):
    t = t_ref[0, 0].astype(F32)
    half = t.shape[1]
    top = (t + pe_ref[0, :, :half]).astype(BF16)
    bot = (t + pe_ref[0, :, half:]).astype(BF16)
    a = _dot(top, w1_ref[0, :half, :].astype(BF16))
    bm = _dot(bot, w1_ref[0, half:, :].astype(BF16))
    hidden = a + pltpu.roll(bm, bm.shape[0] - 1, 0)
    act = hidden * jax.nn.sigmoid(hidden)
    o_ref[0, 0] = _dot(act.astype(BF16), w2_ref[0].astype(BF16)).astype(o_ref.dtype)


def nsa_compress(t16, pe_flat, w1, w2):
    b, g2, nc, width = t16.shape
    g = g2 // 2
    hid = w1.shape[-1]
    return pl.pallas_call(
        _compress_kernel,
        grid=(b, g2),
        in_specs=[pl.BlockSpec((1, 1, nc, width), lambda bi, j: (bi, j, 0, 0)),
                  pl.BlockSpec((1, 1, 2 * width), lambda bi, j: (j // g, 0, 0)),
                  pl.BlockSpec((1, 2 * width, hid), lambda bi, j: (j // g, 0, 0)),
                  pl.BlockSpec((1, hid, HEAD_DIM), lambda bi, j: (j // g, 0, 0))],
        out_specs=pl.BlockSpec((1, 1, nc, HEAD_DIM), lambda bi, j: (bi, j, 0, 0)),
        out_shape=jax.ShapeDtypeStruct((b, g2, nc, HEAD_DIM), BF16),
        compiler_params=_params("parallel", "parallel"),
    )(t16, pe_flat, w1, w2)


def _lanes_x(a, n):
    return jnp.concatenate([a] * n, axis=1)


def _nsa_kernel(q_ref, ks_ref, vs_ref, kw_ref, vw_ref, kc_ref, vc_ref, gt_ref, cover_ref, expand_ref,
                o_ref, part_ref, m_ref, l_ref, acc_ref, *, tq, tk, n_sel):
    i = pl.program_id(2)
    z_heads = NSA_GROUP
    nc = kc_ref.shape[2]
    q = jnp.concatenate([q_ref[0, :, z * HEAD_DIM:(z + 1) * HEAD_DIM] for z in range(z_heads)], axis=0)
    qpos = i * tq + lax.broadcasted_iota(jnp.int32, (1, tq), 1)

    cmp_end = lax.broadcasted_iota(jnp.int32, (nc, tq), 0) * NSA_CMP_STRIDE + (NSA_CMP_LEN - 1)
    cmp_ok = _lanes_x(jnp.where(cmp_end <= qpos, 1.0, 0.0), z_heads)
    sc = jnp.where(cmp_ok > 0.5, _dot_nt(kc_ref[0, 0], q), NEG)
    pc = jnp.exp2(sc - jnp.max(sc, axis=0, keepdims=True))
    pc = pc * (1.0 / jnp.sum(pc, axis=0, keepdims=True)) * cmp_ok
    o_cmp = _dot_tn(vc_ref[0, 0], pc.astype(BF16))

    p_sum = pc[:, 0:tq]
    for z in range(1, z_heads):
        p_sum = p_sum + pc[:, z * tq:(z + 1) * tq]
    p_hi, p_lo = _split_bf16(p_sum)
    imp = (_dot(cover_ref[...], p_hi) + _dot(cover_ref[...], p_lo))[:_pad_rows(n_sel)]
    row = lax.broadcasted_iota(jnp.int32, imp.shape, 0)
    qblk = qpos // NSA_SEL_BLOCK
    forced = (row == 0) | (row == qblk) | (row == qblk - 1)
    allowed = row <= qblk
    imp = jnp.where(forced, BIG, jnp.where(allowed, imp, NEG))
    rank = _rank_before(imp, n_sel)
    block_bias = jnp.where(allowed & (rank < NSA_SEL_TOPN), 0.0, NEG)
    block_bias = jnp.concatenate(
        [block_bias, jnp.zeros((LANES - block_bias.shape[0], tq), F32)], axis=0).T.astype(BF16)
    q2 = jnp.concatenate([q, jnp.concatenate([block_bias] * z_heads, axis=0)], axis=1)
    span = NSA_WINDOW + tq
    w_start = pl.multiple_of(jnp.maximum(i * tq - NSA_WINDOW, 0), tq)
    dist = qpos - (w_start + lax.broadcasted_iota(jnp.int32, (span, tq), 0))
    w_bias = jnp.where((dist >= 0) & (dist < NSA_WINDOW), 0.0, NEG)
    kw = kw_ref[0, pl.ds(w_start, span), :]
    half_rows = z_heads * tq // 2
    sw = [_dot_nt(kw, q[p * half_rows:(p + 1) * half_rows]) + _lanes_x(w_bias, z_heads // 2) for p in range(2)]
    o_win = _softmax_pv(sw, vw_ref[0, pl.ds(w_start, span), :])
    gt = jax.nn.sigmoid(gt_ref[...])
    part_ref[...] = jnp.concatenate(
        [gt[3 * z:3 * z + 1] * o_cmp[:, z * tq:(z + 1) * tq] + gt[3 * z + 2:3 * z + 3] * o_win[:, z * tq:(z + 1) * tq]
         for z in range(z_heads)], axis=1)

    n_split = NSA_LANE_SPLIT
    part_w = z_heads * tq // n_split
    halves = tuple(slice(p * part_w, (p + 1) * part_w) for p in range(n_split))

    def sel_pair(t, causal):
        s, v = [], []
        for c in range(2):
            start = (2 * t + c) * tk
            if not isinstance(start, int):
                start = pl.multiple_of(start, tk)
            k2 = jnp.concatenate([ks_ref[0, pl.ds(start, tk), :], expand_ref[pl.ds(start, tk), :]], axis=1)
            sc = _dot_nt(k2, q2)
            if causal:
                kpos = start + lax.broadcasted_iota(jnp.int32, (tk, tq), 0)
                sc = sc + _lanes_x(jnp.where(kpos <= qpos, 0.0, NEG), z_heads)
            vt = vs_ref[0, pl.ds(start, tk), :]
            for lanes in halves:
                s.append(sc[:, lanes])
                v.append(vt)
        return s, v

    n_pairs = ((i + 1) * tq + 2 * tk - 1) // (2 * tk)
    s0, v0 = sel_pair(0, True)
    _flash_init_many(range(2 * n_split), s0, v0, m_ref, l_ref, acc_ref)

    def sel_body(t, carry):
        st, vt = sel_pair(t, False)
        _flash_update_many(range(2 * n_split), st, vt, m_ref, l_ref, acc_ref)
        return carry

    lax.fori_loop(1, n_pairs - 1, sel_body, 0)

    @pl.when(n_pairs > 1)
    def _():
        st, vt = sel_pair(n_pairs - 1, True)
        _flash_update_many(range(2 * n_split), st, vt, m_ref, l_ref, acc_ref)

    merged = []
    for a in range(n_split):
        b = n_split + a
        m_all = jnp.maximum(m_ref[a], m_ref[b])
        w0, w1 = jnp.exp2(m_ref[a] - m_all), jnp.exp2(m_ref[b] - m_all)
        merged.append((w0 * acc_ref[a] + w1 * acc_ref[b]) * (1.0 / (w0 * l_ref[a] + w1 * l_ref[b])))
    o_sel = jnp.concatenate(merged, axis=1)

    for z in range(z_heads):
        ls = slice(z * tq, (z + 1) * tq)
        o = part_ref[:, ls] + jax.nn.sigmoid(gt_ref[3 * z + 1:3 * z + 2, :]) * o_sel[:, ls]
        o_ref[0, :, z * HEAD_DIM:(z + 1) * HEAD_DIM] = o.T.astype(o_ref.dtype)


def nsa_attention(proj, gates_t, cmp_kv, n_groups, q_col, ks_col, vs_col, kw_col, vw_col, tq=128, tk=256):
    b, s, _ = proj.shape
    tq, tk = min(tq, s), min(tk, s)
    assert s % (2 * tk) == 0 and tk % tq == 0 and NSA_WINDOW % tq == 0 and tq % NSA_SEL_BLOCK == 0
    assert s >= NSA_WINDOW + tq
    nc = cmp_kv.shape[2]
    n_sel = s // NSA_SEL_BLOCK
    assert nc * NSA_CMP_STRIDE == s and n_sel <= LANES
    c_lo = np.arange(nc) * NSA_CMP_STRIDE
    j_lo = np.arange(LANES) * NSA_SEL_BLOCK
    cover_t = ((c_lo[None, :] < j_lo[:, None] + NSA_SEL_BLOCK) & (c_lo[None, :] + NSA_CMP_LEN > j_lo[:, None])
               & (np.arange(LANES)[:, None] < n_sel) & (c_lo[None, :] + NSA_CMP_LEN <= s))
    expand_t = (np.arange(s)[:, None] // NSA_SEL_BLOCK == np.arange(LANES)[None, :])
    gw = NSA_GROUP * HEAD_DIM
    qo = q_col // gw
    kso, vso, kwo, vwo = (c // HEAD_DIM for c in (ks_col, vs_col, kw_col, vw_col))
    kv = lambda off: pl.BlockSpec((1, s, HEAD_DIM), lambda bi, g, i: (bi, 0, off + g))
    q_tiles = s // tq
    rows = NSA_GROUP * tq
    return pl.pallas_call(
        functools.partial(_nsa_kernel, tq=tq, tk=tk, n_sel=n_sel),
        grid=(b, n_groups, q_tiles),
        in_specs=[pl.BlockSpec((1, tq, gw), lambda bi, g, i: (bi, i, qo + g)),
                  kv(kso), kv(vso), kv(kwo), kv(vwo),
                  pl.BlockSpec((1, 1, nc, HEAD_DIM), lambda bi, g, i: (bi, g, 0, 0)),
                  pl.BlockSpec((1, 1, nc, HEAD_DIM), lambda bi, g, i: (bi, n_groups + g, 0, 0)),
                  pl.BlockSpec((NSA_GATE_PAD, tq), lambda bi, g, i: (g, bi * q_tiles + i)),
                  pl.BlockSpec((LANES, nc), lambda bi, g, i: (0, 0)),
                  pl.BlockSpec((s, LANES), lambda bi, g, i: (0, 0))],
        out_specs=pl.BlockSpec((1, tq, gw), lambda bi, g, i: (bi, i, g)),
        out_shape=jax.ShapeDtypeStruct((b, s, n_groups * gw), BF16),
        scratch_shapes=[pltpu.VMEM((HEAD_DIM, rows), F32),
                        pltpu.VMEM((2 * NSA_LANE_SPLIT, 1, rows // NSA_LANE_SPLIT), F32),
                        pltpu.VMEM((2 * NSA_LANE_SPLIT, 1, rows // NSA_LANE_SPLIT), F32),
                        pltpu.VMEM((2 * NSA_LANE_SPLIT, HEAD_DIM, rows // NSA_LANE_SPLIT), F32)],
        compiler_params=_params("parallel", "parallel", "arbitrary"),
    )(proj, proj, proj, proj, proj, cmp_kv, cmp_kv, gates_t, jnp.asarray(cover_t, BF16),
      jnp.asarray(expand_t, BF16))


def _even_tile_mode(tn, d_model):
    seg_tiles = (d_model // 2) // tn
    return lambda j: ROPE_SCALED - (j // seg_tiles) % 3


def even_mixer(hn, w_in, w_out, e, lam_params, subln_g, lam_init, tables, batch, seq):
    d_model = hn.shape[1]
    seg = d_model // 2
    tn = min(1024, seg)
    proj = proj_rope(hn, w_in, e, 6 * seg, tables, seq, _even_tile_mode(tn, d_model), tn=tn)
    proj = proj.reshape(batch, seq, -1)
    o_diff = diff_attention(proj, lam_params, subln_g, lam_init, seg // DIFF_V_DIM, 0, seg, 2 * seg)
    o_moba = moba_attention(proj, seg // HEAD_DIM, 3 * seg, 4 * seg, 5 * seg)
    o = jnp.concatenate([o_diff, o_moba], axis=-1).reshape(batch * seq, d_model)
    return matmul(o, w_out, e, F32)


def _odd_tile_mode(tn, d_model):
    slab_tiles = (d_model // 8) // tn
    q_tiles = d_model // tn
    return lambda j: jnp.where(j < q_tiles, ROPE_SCALED, ROPE - ((j - q_tiles) // slab_tiles) % 2)


def odd_mixer(hn, w_in, w_out, o_idx, cmp_pe, cmp_w1, cmp_w2, tables, batch, seq):
    d_model = hn.shape[1]
    g = d_model // (NSA_GROUP * HEAD_DIM)
    kvd = g * HEAD_DIM
    n_main = d_model + 6 * kvd
    tn = min(512, kvd)
    proj = proj_rope(hn, w_in, o_idx, n_main, tables, seq, _odd_tile_mode(tn, d_model), tn=tn)
    proj = proj.reshape(batch, seq, n_main)
    n_gate = 3 * NSA_GROUP
    w_gate_t = w_in[o_idx, :, n_main:].T.reshape(g, n_gate, d_model)
    w_gate_t = jnp.pad(w_gate_t, ((0, 0), (0, NSA_GATE_PAD - n_gate), (0, 0))).reshape(g * NSA_GATE_PAD, d_model)
    gates_t = matmul_nt(w_gate_t, hn)
    nc = seq // NSA_CMP_STRIDE
    t16 = proj[:, :, d_model:d_model + 2 * kvd].reshape(batch, seq, 2 * g, HEAD_DIM)
    t16 = t16.transpose(0, 2, 1, 3).reshape(batch, 2 * g, nc, NSA_CMP_STRIDE * HEAD_DIM)
    cmp_kv = nsa_compress(t16, cmp_pe.reshape(2, 1, NSA_CMP_LEN * HEAD_DIM), cmp_w1, cmp_w2)
    o = nsa_attention(proj, gates_t, cmp_kv, g, 0, d_model + 2 * kvd, d_model + 3 * kvd, d_model + 4 * kvd,
                      d_model + 5 * kvd)
    return matmul(o.reshape(batch * seq, d_model), w_out, o_idx, F32)


def ffn(hn, w_gate, w_up, w_down, layer):
    return matmul(swiglu_up(hn, w_gate, w_up, layer), w_down, layer, F32, tn=256)


def kernel(x, norm_g, ffn_a_gate, ffn_a_up, ffn_a_down, ffn_b_gate, ffn_b_up, ffn_b_down, ev_w_in, ev_w_out, ev_lambda, ev_subln_g, od_w_in, od_w_out, od_cmp_pe, od_cmp_w1, od_cmp_w2):
    batch, seq, d_model = x.shape
    depth = norm_g.shape[0]
    tables = rope_tables(seq)
    h = x.reshape(batch * seq, d_model)
    hn = rmsnorm(h, norm_g[0, 0])
    for layer in range(depth):
        g = norm_g[layer]
        m = ffn(hn, ffn_a_gate, ffn_a_up, ffn_a_down, layer)
        h, hn = residual_norm(h, m, g[1], g[2], 0.5)
        if layer % 2 == 0:
            e = layer // 2
            lam_init = 0.8 - 0.6 * math.exp(-0.3 * layer)
            m = even_mixer(hn, ev_w_in, ev_w_out, e, ev_lambda[e], ev_subln_g[e], lam_init, tables, batch, seq)
        else:
            o = layer // 2
            m = odd_mixer(hn, od_w_in, od_w_out, o, od_cmp_pe[o], od_cmp_w1[o], od_cmp_w2[o], tables, batch, seq)
        h, hn = residual_norm(h, m, g[3], g[4], 1.0)
        m = ffn(hn, ffn_b_gate, ffn_b_up, ffn_b_down, layer)
        g_next = norm_g[layer + 1, 0] if layer + 1 < depth else None
        h, hn = residual_norm(h, m, g[5], g_next, 0.5)
    return h.reshape(batch, seq, d_model)
```

```python
import functools
import math

import numpy as np
import jax
import jax.numpy as jnp
from jax import lax
from jax.experimental import pallas as pl
from jax.experimental.pallas import tpu as pltpu

F32 = jnp.float32
BF16 = jnp.bfloat16

HEAD_DIM = 128
ROT_DIM = HEAD_DIM // 4
ROPE_THETA = 500000.0
RMS_EPS = 1e-6
NEG = -1e30
BIG = 1e30
Q_SCALE = HEAD_DIM ** -0.5 * math.log2(math.e)

DIFF_V_DIM = 2 * HEAD_DIM
MOBA_BLOCK = 256
MOBA_TOPK = 3
NSA_GROUP = 8
NSA_CMP_LEN = 32
NSA_CMP_STRIDE = 16
NSA_SEL_BLOCK = 64
NSA_SEL_TOPN = 16
NSA_WINDOW = 512
NSA_GATE_PAD = 128
NSA_LANE_SPLIT = 2

LANES = 128
SUBLANES = 8
VMEM_LIMIT_BYTES = 56 * 1024 * 1024


def _params(*semantics):
    return pltpu.CompilerParams(dimension_semantics=semantics, vmem_limit_bytes=VMEM_LIMIT_BYTES)


def _dot(a, b):
    return jnp.dot(a, b, preferred_element_type=F32)


def _dot_nt(a, b):
    return lax.dot_general(a, b, (((1,), (1,)), ((), ())), preferred_element_type=F32)


def _dot_tn(a, b):
    return lax.dot_general(a, b, (((0,), (0,)), ((), ())), preferred_element_type=F32)


def _split_bf16(x):
    hi = x.astype(BF16)
    lo = (x - hi.astype(F32)).astype(BF16)
    return hi, lo


def _rms(x, g):
    return x * lax.rsqrt(jnp.mean(x * x, axis=-1, keepdims=True) + RMS_EPS) * g


def _rmsnorm_kernel(x_ref, g_ref, o_ref):
    o_ref[...] = _rms(x_ref[...], g_ref[...]).astype(o_ref.dtype)


def rmsnorm(x, g, tm=256):
    m, d = x.shape
    tm = min(tm, m)
    return pl.pallas_call(
        _rmsnorm_kernel,
        grid=(m // tm,),
        in_specs=[pl.BlockSpec((tm, d), lambda i: (i, 0)), pl.BlockSpec((1, d), lambda i: (0, 0))],
        out_specs=pl.BlockSpec((tm, d), lambda i: (i, 0)),
        out_shape=jax.ShapeDtypeStruct((m, d), BF16),
        compiler_params=_params("parallel"),
    )(x, g.reshape(1, d))


def _residual_kernel(h_ref, m_ref, gpost_ref, gnext_ref, h_out_ref, hn_out_ref, *, coef):
    y = _rms(m_ref[...], gpost_ref[...])
    if coef != 1.0:
        y = coef * y
    h = h_ref[...] + y
    h_out_ref[...] = h
    hn_out_ref[...] = _rms(h, gnext_ref[...]).astype(hn_out_ref.dtype)


def _residual_last_kernel(h_ref, m_ref, gpost_ref, h_out_ref, *, coef):
    y = _rms(m_ref[...], gpost_ref[...])
    if coef != 1.0:
        y = coef * y
    h_out_ref[...] = h_ref[...] + y


def residual_norm(h, mix, g_post, g_next, coef, tm=256):
    m, d = h.shape
    tm = min(tm, m)
    row = pl.BlockSpec((tm, d), lambda i: (i, 0))
    vec = pl.BlockSpec((1, d), lambda i: (0, 0))
    if g_next is None:
        return pl.pallas_call(
            functools.partial(_residual_last_kernel, coef=coef),
            grid=(m // tm,),
            in_specs=[row, row, vec],
            out_specs=row,
            out_shape=jax.ShapeDtypeStruct((m, d), F32),
            compiler_params=_params("parallel"),
        )(h, mix, g_post.reshape(1, d)), None
    return pl.pallas_call(
        functools.partial(_residual_kernel, coef=coef),
        grid=(m // tm,),
        in_specs=[row, row, vec, vec],
        out_specs=[row, row],
        out_shape=[jax.ShapeDtypeStruct((m, d), F32), jax.ShapeDtypeStruct((m, d), BF16)],
        compiler_params=_params("parallel"),
    )(h, mix, g_post.reshape(1, d), g_next.reshape(1, d))


def _mm_kernel(x_ref, w_ref, o_ref):
    o_ref[...] = _dot(x_ref[...], w_ref[...].astype(BF16)).astype(o_ref.dtype)


def matmul(x, w, layer, out_dtype, tm=1024, tn=512):
    m, kd = x.shape
    n = w.shape[2]
    tm, tn = min(tm, m), min(tn, n)
    return pl.pallas_call(
        _mm_kernel,
        grid=(m // tm, n // tn),
        in_specs=[pl.BlockSpec((tm, kd), lambda i, j: (i, 0)),
                  pl.BlockSpec((None, kd, tn), lambda i, j: (layer, 0, j))],
        out_specs=pl.BlockSpec((tm, tn), lambda i, j: (i, j)),
        out_shape=jax.ShapeDtypeStruct((m, n), out_dtype),
        compiler_params=_params("parallel", "parallel"),
    )(x, w)


def _mm_cat_kernel(x1_ref, x2_ref, w_ref, o_ref):
    k1 = x1_ref.shape[1]
    o_ref[...] = (_dot(x1_ref[...], w_ref[:k1, :].astype(BF16))
                  + _dot(x2_ref[...], w_ref[k1:, :].astype(BF16))).astype(o_ref.dtype)


def matmul_cat(x1, x2, w, layer, out_dtype, tm=1024, tn=512):
    m, k1 = x1.shape
    k2 = x2.shape[1]
    n = w.shape[2]
    tm, tn = min(tm, m), min(tn, n)
    return pl.pallas_call(
        _mm_cat_kernel,
        grid=(m // tm, n // tn),
        in_specs=[pl.BlockSpec((tm, k1), lambda i, j: (i, 0)), pl.BlockSpec((tm, k2), lambda i, j: (i, 0)),
                  pl.BlockSpec((None, k1 + k2, tn), lambda i, j: (layer, 0, j))],
        out_specs=pl.BlockSpec((tm, tn), lambda i, j: (i, j)),
        out_shape=jax.ShapeDtypeStruct((m, n), out_dtype),
        compiler_params=_params("parallel", "parallel"),
    )(x1, x2, w)


def _mm_nt_kernel(w_ref, x_ref, o_ref):
    o_ref[...] = _dot_nt(w_ref[...].astype(BF16), x_ref[...])


def matmul_nt(w_t, x, tm=1024):
    n, kd = w_t.shape
    m = x.shape[0]
    tm = min(tm, m)
    return pl.pallas_call(
        _mm_nt_kernel,
        grid=(m // tm,),
        in_specs=[pl.BlockSpec((n, kd), lambda i: (0, 0)), pl.BlockSpec((tm, kd), lambda i: (i, 0))],
        out_specs=pl.BlockSpec((n, tm), lambda i: (0, i)),
        out_shape=jax.ShapeDtypeStruct((n, m), F32),
        compiler_params=_params("parallel"),
    )(w_t, x)


def _swiglu_kernel(x_ref, wg_ref, wu_ref, o_ref):
    x = x_ref[...]
    g = _dot(x, wg_ref[...].astype(BF16))
    u = _dot(x, wu_ref[...].astype(BF16))
    o_ref[...] = (g * jax.nn.sigmoid(g) * u).astype(o_ref.dtype)


def swiglu_up(x, w_gate, w_up, layer, tm=1024, tn=256):
    m, kd = x.shape
    n = w_gate.shape[2]
    tm, tn = min(tm, m), min(tn, n)
    wspec = pl.BlockSpec((None, kd, tn), lambda i, j: (layer, 0, j))
    return pl.pallas_call(
        _swiglu_kernel,
        grid=(m // tm, n // tn),
        in_specs=[pl.BlockSpec((tm, kd), lambda i, j: (i, 0)), wspec, wspec],
        out_specs=pl.BlockSpec((tm, tn), lambda i, j: (i, j)),
        out_shape=jax.ShapeDtypeStruct((m, n), BF16),
        compiler_params=_params("parallel", "parallel"),
    )(x, w_gate, w_up)


def rope_tables(seq):
    half = ROT_DIM // 2
    pos = jnp.arange(seq, dtype=F32)
    inv = ROPE_THETA ** (-jnp.arange(0, ROT_DIM, 2, dtype=F32) / ROT_DIM)
    ang = pos[:, None] * inv[None, :]
    cos, sin = jnp.cos(ang), jnp.sin(ang)
    zeros = jnp.zeros((seq, HEAD_DIM - ROT_DIM), F32)
    zhalf = jnp.zeros((seq, half), F32)
    c = jnp.concatenate([cos, cos, zeros + 1.0], axis=-1)
    s1 = jnp.concatenate([zhalf, sin, zeros], axis=-1)
    s2 = jnp.concatenate([-sin, zhalf, zeros], axis=-1)
    return c, s1, s2


PLAIN, ROPE, ROPE_SCALED = 0, 1, 2


def _proj_rope_kernel(x_ref, w_ref, c_ref, s1_ref, s2_ref, o_ref, *, tile_mode):
    mode = tile_mode(pl.program_id(1))
    half = ROT_DIM // 2
    sub = min(2 * HEAD_DIM, o_ref.shape[1])

    def run(rope, scale):
        x = x_ref[...]
        for c0 in range(0, o_ref.shape[1], sub):
            acc = _dot(x, w_ref[:, c0:c0 + sub].astype(BF16))
            if not rope:
                o_ref[:, c0:c0 + sub] = acc.astype(o_ref.dtype)
                continue
            c, s1, s2 = c_ref[...], s1_ref[...], s2_ref[...]
            for h0 in range(0, sub, HEAD_DIM):
                xh = acc[:, h0:h0 + HEAD_DIM]
                r = xh * c + pltpu.roll(xh, half, 1) * s1 + pltpu.roll(xh, HEAD_DIM - half, 1) * s2
                if scale != 1.0:
                    r = r * scale
                o_ref[:, c0 + h0:c0 + h0 + HEAD_DIM] = r.astype(o_ref.dtype)

    @pl.when(mode == ROPE_SCALED)
    def _():
        run(True, Q_SCALE)

    @pl.when(mode == ROPE)
    def _():
        run(True, 1.0)

    @pl.when(mode == PLAIN)
    def _():
        run(False, 1.0)


def proj_rope(x, w, layer, n, tables, seq, tile_mode, tm=1024, tn=512):
    m, kd = x.shape
    tm, tn = min(tm, seq), min(tn, n)
    assert seq % tm == 0 and m % tm == 0 and n % tn == 0
    pos_blocks = seq // tm
    row_bytes = tm * kd * 2 + 3 * tm * HEAD_DIM * 4
    sub = min(2 * HEAD_DIM, tn)
    stream_bytes = 2 * (kd * tn * 4 + tm * tn * 2) + kd * sub * 2 + tm * sub * 4
    rows_mode = {} if 2 * row_bytes + stream_bytes <= VMEM_LIMIT_BYTES else {"pipeline_mode": pl.Buffered(1)}
    tspec = pl.BlockSpec((tm, HEAD_DIM), lambda i, j: (i % pos_blocks, 0), **rows_mode)
    return pl.pallas_call(
        functools.partial(_proj_rope_kernel, tile_mode=tile_mode),
        grid=(m // tm, n // tn),
        in_specs=[pl.BlockSpec((tm, kd), lambda i, j: (i, 0), **rows_mode),
                  pl.BlockSpec((None, kd, tn), lambda i, j: (layer, 0, j)),
                  tspec, tspec, tspec],
        out_specs=pl.BlockSpec((tm, tn), lambda i, j: (i, j)),
        out_shape=jax.ShapeDtypeStruct((m, n), BF16),
        compiler_params=_params("parallel", "parallel"),
    )(x, w, *tables)


def _flash_init_many(slots, s_list, v_list, m_ref, l_ref, acc_ref):
    m = [jnp.max(s, axis=0, keepdims=True) for s in s_list]
    p = [jnp.exp2(s - mm) for s, mm in zip(s_list, m)]
    pv = [_dot_tn(v, pp.astype(v.dtype)) for v, pp in zip(v_list, p)]
    for idx, c in enumerate(slots):
        m_ref[c] = m[idx]
        l_ref[c] = jnp.sum(p[idx], axis=0, keepdims=True)
        acc_ref[c] = pv[idx]


def _flash_update_many(slots, s_list, v_list, m_ref, l_ref, acc_ref):
    m_old = [m_ref[c] for c in slots]
    l_old = [l_ref[c] for c in slots]
    m_new = [jnp.maximum(mo, jnp.max(s, axis=0, keepdims=True)) for mo, s in zip(m_old, s_list)]
    p = [jnp.exp2(s - mn) for s, mn in zip(s_list, m_new)]
    pv = [_dot_tn(v, pp.astype(v.dtype)) for v, pp in zip(v_list, p)]
    for idx, c in enumerate(slots):
        alpha = jnp.exp2(m_old[idx] - m_new[idx])
        m_ref[c] = m_new[idx]
        l_ref[c] = alpha * l_old[idx] + jnp.sum(p[idx], axis=0, keepdims=True)
        acc_ref[c] = alpha * acc_ref[c] + pv[idx]


def _flash_result(slot, l_ref, acc_ref):
    return acc_ref[slot] * (1.0 / l_ref[slot])


def _softmax_pv(s_parts, v):
    e = [jnp.exp2(s - jnp.max(s, axis=0, keepdims=True)) for s in s_parts]
    pv = [_dot_tn(v, p.astype(v.dtype)) for p in e]
    return jnp.concatenate([o * (1.0 / jnp.sum(p, axis=0, keepdims=True)) for o, p in zip(pv, e)], axis=1)


def _pad_rows(n_rows):
    return -(-n_rows // SUBLANES) * SUBLANES


def _rank_before(v, n_rows):
    row = lax.broadcasted_iota(jnp.int32, v.shape, 0)
    rank = jnp.zeros(v.shape, jnp.int32)
    for j2 in range(n_rows):
        other = v[j2:j2 + 1, :]
        rank = rank + jnp.where(other > v, 1, jnp.where(other == v, (row > j2).astype(jnp.int32), 0))
    return rank


def _diff_kernel(lam_ref, g_ref, q_ref, k_ref, v_ref, o_ref, m_ref, l_ref, acc_ref, *, tq, hp, lam_init):
    i = pl.program_id(2)
    w = DIFF_V_DIM
    lp = lam_ref[...]
    lam = (jnp.exp(jnp.sum(lp[0:1] * lp[1:2], axis=-1, keepdims=True))
           - jnp.exp(jnp.sum(lp[2:3] * lp[3:4], axis=-1, keepdims=True)) + lam_init)
    key = lax.broadcasted_iota(jnp.int32, (tq, tq), 0)
    qry = lax.broadcasted_iota(jnp.int32, (tq, tq), 1)

    def scores(c, start):
        cols = slice(c * HEAD_DIM, (c + 1) * HEAD_DIM)
        return _dot_nt(k_ref[0, pl.ds(start, tq), cols], q_ref[0, :, cols])

    own = pl.multiple_of(i * tq, tq)
    s_own = [jnp.where(key <= qry, scores(c, own), NEG) for c in range(2 * hp)]
    v_own = [v_ref[0, pl.ds(own, tq), (c // 2) * w:(c // 2 + 1) * w] for c in range(2 * hp)]
    _flash_init_many(range(2 * hp), s_own, v_own, m_ref, l_ref, acc_ref)

    def body(j, carry):
        start = pl.multiple_of(j * tq, tq)
        s = [scores(c, start) for c in range(2 * hp)]
        v = [v_ref[0, pl.ds(start, tq), (c // 2) * w:(c // 2 + 1) * w] for c in range(2 * hp)]
        _flash_update_many(range(2 * hp), s, v, m_ref, l_ref, acc_ref)
        return carry

    lax.fori_loop(0, i, body, 0)
    for h in range(hp):
        o = (_flash_result(2 * h, l_ref, acc_ref) - lam * _flash_result(2 * h + 1, l_ref, acc_ref)).T
        o_ref[0, :, h * w:(h + 1) * w] = (_rms(o, g_ref[...]) * (1.0 - lam_init)).astype(o_ref.dtype)


def diff_attention(proj, lam_params, subln_g, lam_init, n_heads, q_col, k_col, v_col, tq=256, hp=4):
    b, s, _ = proj.shape
    tq = min(tq, s)
    w = DIFF_V_DIM
    hp = min(hp, n_heads)
    assert n_heads % hp == 0
    gw = hp * w
    qo, ko, vo = q_col // gw, k_col // gw, v_col // gw
    return pl.pallas_call(
        functools.partial(_diff_kernel, tq=tq, hp=hp, lam_init=lam_init),
        grid=(b, n_heads // hp, s // tq),
        in_specs=[pl.BlockSpec((4, HEAD_DIM), lambda bi, h, i: (0, 0)),
                  pl.BlockSpec((1, w), lambda bi, h, i: (0, 0)),
                  pl.BlockSpec((1, tq, gw), lambda bi, h, i: (bi, i, qo + h)),
                  pl.BlockSpec((1, s, gw), lambda bi, h, i: (bi, 0, ko + h)),
                  pl.BlockSpec((1, s, gw), lambda bi, h, i: (bi, 0, vo + h))],
        out_specs=pl.BlockSpec((1, tq, gw), lambda bi, h, i: (bi, i, h)),
        out_shape=jax.ShapeDtypeStruct((b, s, n_heads * w), BF16),
        scratch_shapes=[pltpu.VMEM((2 * hp, 1, tq), F32), pltpu.VMEM((2 * hp, 1, tq), F32),
                        pltpu.VMEM((2 * hp, w, tq), F32)],
        compiler_params=_params("parallel", "parallel", "arbitrary"),
    )(lam_params, subln_g.reshape(1, w), proj, proj, proj)


def _moba_kernel(q_ref, k_ref, v_ref, o_ref, kmean_ref, bias_ref, m_ref, l_ref, acc_ref, *, nb, hp):
    i = pl.program_id(2)
    blk = MOBA_BLOCK
    head = lambda h: slice(h * HEAD_DIM, (h + 1) * HEAD_DIM)

    @pl.when(i == 0)
    def _():
        kmean_ref[...] = jnp.zeros(kmean_ref.shape, F32)
        for h in range(hp):
            for n in range(nb):
                kb = k_ref[0, n * blk:(n + 1) * blk, head(h)].astype(F32)
                kmean_ref[h, n:n + 1, :] = jnp.mean(kb, axis=0, keepdims=True)

    key = lax.broadcasted_iota(jnp.int32, (blk, blk), 0)
    qry = lax.broadcasted_iota(jnp.int32, (blk, blk), 1)
    own = pl.multiple_of(i * blk, blk)
    qs = [q_ref[0, :, head(h)] for h in range(hp)]
    s_own = [jnp.where(key <= qry, _dot_nt(k_ref[0, pl.ds(own, blk), head(h)], qs[h]), NEG) for h in range(hp)]
    v_own = [v_ref[0, pl.ds(own, blk), head(h)] for h in range(hp)]
    biases = []
    for h in range(hp):
        km_hi, km_lo = _split_bf16(kmean_ref[h])
        gate = (_dot_nt(km_hi, qs[h]) + _dot_nt(km_lo, qs[h]))[:_pad_rows(nb)]
        past = lax.broadcasted_iota(jnp.int32, gate.shape, 0) < i
        rank = _rank_before(jnp.where(past, gate, NEG), nb)
        biases.append(jnp.where(past & (rank < MOBA_TOPK), 0.0, NEG))
    for h in range(hp):
        bias_ref[h] = biases[h]
    _flash_init_many(range(hp), s_own, v_own, m_ref, l_ref, acc_ref)

    def body(n, carry):
        start = pl.multiple_of(n * blk, blk)
        sn = [_dot_nt(k_ref[0, pl.ds(start, blk), head(h)], q_ref[0, :, head(h)]) + bias_ref[h, pl.ds(n, 1), :]
              for h in range(hp)]
        vn = [v_ref[0, pl.ds(start, blk), head(h)] for h in range(hp)]
        _flash_update_many(range(hp), sn, vn, m_ref, l_ref, acc_ref)
        return carry

    lax.fori_loop(0, i, body, 0)
    for h in range(hp):
        o_ref[0, :, head(h)] = _flash_result(h, l_ref, acc_ref).T.astype(o_ref.dtype)


def moba_attention(proj, n_heads, q_col, k_col, v_col, hp=8):
    b, s, _ = proj.shape
    blk = MOBA_BLOCK
    hp = min(hp, n_heads)
    assert s % blk == 0 and s // blk <= LANES and n_heads % hp == 0
    nb = s // blk
    gw = hp * HEAD_DIM
    qo, ko, vo = q_col // gw, k_col // gw, v_col // gw
    return pl.pallas_call(
        functools.partial(_moba_kernel, nb=nb, hp=hp),
        grid=(b, n_heads // hp, nb),
        in_specs=[pl.BlockSpec((1, blk, gw), lambda bi, h, i: (bi, i, qo + h)),
                  pl.BlockSpec((1, s, gw), lambda bi, h, i: (bi, 0, ko + h)),
                  pl.BlockSpec((1, s, gw), lambda bi, h, i: (bi, 0, vo + h))],
        out_specs=pl.BlockSpec((1, blk, gw), lambda bi, h, i: (bi, i, h)),
        out_shape=jax.ShapeDtypeStruct((b, s, n_heads * HEAD_DIM), BF16),
        scratch_shapes=[pltpu.VMEM((hp, LANES, HEAD_DIM), F32), pltpu.VMEM((hp, _pad_rows(nb), blk), F32),
                        pltpu.VMEM((hp, 1, blk), F32), pltpu.VMEM((hp, 1, blk), F32),
                        pltpu.VMEM((hp, HEAD_DIM, blk), F32)],
        compiler_params=_params("parallel", "parallel", "arbitrary"),
    )(proj, proj, proj)


def _compress_kernel(t_ref, pe_ref, w1_ref, w2_ref, o_ref):
    t = t_ref[0, 0].astype(F32)
    half = t.shape[1]
    top = (t + pe_ref[0, :, :half]).astype(BF16)
    bot = (t + pe_ref[0, :, half:]).astype(BF16)
    a = _dot(top, w1_ref[0, :half, :].astype(BF16))
    bm = _dot(bot, w1_ref[0, half:, :].astype(BF16))
    hidden = a + pltpu.roll(bm, bm.shape[0] - 1, 0)
    act = hidden * jax.nn.sigmoid(hidden)
    o_ref[0, 0] = _dot(act.astype(BF16), w2_ref[0].astype(BF16)).astype(o_ref.dtype)


def nsa_compress(t16, pe_flat, w1, w2):
    b, g2, nc, width = t16.shape
    g = g2 // 2
    hid = w1.shape[-1]
    return pl.pallas_call(
        _compress_kernel,
        grid=(b, g2),
        in_specs=[pl.BlockSpec((1, 1, nc, width), lambda bi, j: (bi, j, 0, 0)),
                  pl.BlockSpec((1, 1, 2 * width), lambda bi, j: (j // g, 0, 0)),
                  pl.BlockSpec((1, 2 * width, hid), lambda bi, j: (j // g, 0, 0)),
                  pl.BlockSpec((1, hid, HEAD_DIM), lambda bi, j: (j // g, 0, 0))],
        out_specs=pl.BlockSpec((1, 1, nc, HEAD_DIM), lambda bi, j: (bi, j, 0, 0)),
        out_shape=jax.ShapeDtypeStruct((b, g2, nc, HEAD_DIM), BF16),
        compiler_params=_params("parallel", "parallel"),
    )(t16, pe_flat, w1, w2)


def _lanes_x(a, n):
    return jnp.concatenate([a] * n, axis=1)


def _nsa_kernel(q_ref, ks_ref, vs_ref, kw_ref, vw_ref, kc_ref, vc_ref, gt_ref, cover_ref, expand_ref,
                o_ref, part_ref, m_ref, l_ref, acc_ref, *, tq, tk, n_sel):
    i = pl.program_id(2)
    z_heads = NSA_GROUP
    nc = kc_ref.shape[2]
    q = jnp.concatenate([q_ref[0, :, z * HEAD_DIM:(z + 1) * HEAD_DIM] for z in range(z_heads)], axis=0)
    qpos = i * tq + lax.broadcasted_iota(jnp.int32, (1, tq), 1)

    cmp_end = lax.broadcasted_iota(jnp.int32, (nc, tq), 0) * NSA_CMP_STRIDE + (NSA_CMP_LEN - 1)
    cmp_ok = _lanes_x(jnp.where(cmp_end <= qpos, 1.0, 0.0), z_heads)
    sc = jnp.where(cmp_ok > 0.5, _dot_nt(kc_ref[0, 0], q), NEG)
    pc = jnp.exp2(sc - jnp.max(sc, axis=0, keepdims=True))
    pc = pc * (1.0 / jnp.sum(pc, axis=0, keepdims=True)) * cmp_ok
    o_cmp = _dot_tn(vc_ref[0, 0], pc.astype(BF16))

    p_sum = pc[:, 0:tq]
    for z in range(1, z_heads):
        p_sum = p_sum + pc[:, z * tq:(z + 1) * tq]
    p_hi, p_lo = _split_bf16(p_sum)
    imp = (_dot(cover_ref[...], p_hi) + _dot(cover_ref[...], p_lo))[:_pad_rows(n_sel)]
    row = lax.broadcasted_iota(jnp.int32, imp.shape, 0)
    qblk = qpos // NSA_SEL_BLOCK
    forced = (row == 0) | (row == qblk) | (row == qblk - 1)
    allowed = row <= qblk
    imp = jnp.where(forced, BIG, jnp.where(allowed, imp, NEG))
    rank = _rank_before(imp, n_sel)
    block_bias = jnp.where(allowed & (rank < NSA_SEL_TOPN), 0.0, NEG)
    block_bias = jnp.concatenate(
        [block_bias, jnp.zeros((LANES - block_bias.shape[0], tq), F32)], axis=0).T.astype(BF16)
    q2 = jnp.concatenate([q, jnp.concatenate([block_bias] * z_heads, axis=0)], axis=1)
    span = NSA_WINDOW + tq
    w_start = pl.multiple_of(jnp.maximum(i * tq - NSA_WINDOW, 0), tq)
    dist = qpos - (w_start + lax.broadcasted_iota(jnp.int32, (span, tq), 0))
    w_bias = jnp.where((dist >= 0) & (dist < NSA_WINDOW), 0.0, NEG)
    kw = kw_ref[0, pl.ds(w_start, span), :]
    half_rows = z_heads * tq // 2
    sw = [_dot_nt(kw, q[p * half_rows:(p + 1) * half_rows]) + _lanes_x(w_bias, z_heads // 2) for p in range(2)]
    o_win = _softmax_pv(sw, vw_ref[0, pl.ds(w_start, span), :])
    gt = jax.nn.sigmoid(gt_ref[...])
    part_ref[...] = jnp.concatenate(
        [gt[3 * z:3 * z + 1] * o_cmp[:, z * tq:(z + 1) * tq] + gt[3 * z + 2:3 * z + 3] * o_win[:, z * tq:(z + 1) * tq]
         for z in range(z_heads)], axis=1)

    n_split = NSA_LANE_SPLIT
    part_w = z_heads * tq // n_split
    halves = tuple(slice(p * part_w, (p + 1) * part_w) for p in range(n_split))

    def sel_pair(t, causal, tiles=(0, 1)):
        s, v = [], []
        for c in tiles:
            start = (2 * t + c) * tk
            if not isinstance(start, int):
                start = pl.multiple_of(start, tk)
            k2 = jnp.concatenate([ks_ref[0, pl.ds(start, tk), :], expand_ref[pl.ds(start, tk), :]], axis=1)
            sc = _dot_nt(k2, q2)
            if causal:
                kpos = start + lax.broadcasted_iota(jnp.int32, (tk, tq), 0)
                sc = sc + _lanes_x(jnp.where(kpos <= qpos, 0.0, NEG), z_heads)
            vt = vs_ref[0, pl.ds(start, tk), :]
            for lanes in halves:
                s.append(sc[:, lanes])
                v.append(vt)
        return s, v

    n_pairs = ((i + 1) * tq + 2 * tk - 1) // (2 * tk)
    s0, v0 = sel_pair(0, True)
    _flash_init_many(range(2 * n_split), s0, v0, m_ref, l_ref, acc_ref)

    def sel_body(t, carry):
        st, vt = sel_pair(t, False)
        _flash_update_many(range(2 * n_split), st, vt, m_ref, l_ref, acc_ref)
        return carry

    lax.fori_loop(1, n_pairs - 1, sel_body, 0)

    odd_needed = (2 * n_pairs - 1) * tk < (i + 1) * tq

    @pl.when((n_pairs > 1) & odd_needed)
    def _():
        st, vt = sel_pair(n_pairs - 1, True)
        _flash_update_many(range(2 * n_split), st, vt, m_ref, l_ref, acc_ref)

    @pl.when((n_pairs > 1) & jnp.logical_not(odd_needed))
    def _():
        st, vt = sel_pair(n_pairs - 1, True, tiles=(0,))
        _flash_update_many(range(n_split), st, vt, m_ref, l_ref, acc_ref)

    merged = []
    for a in range(n_split):
        b = n_split + a
        m_all = jnp.maximum(m_ref[a], m_ref[b])
        w0, w1 = jnp.exp2(m_ref[a] - m_all), jnp.exp2(m_ref[b] - m_all)
        merged.append((w0 * acc_ref[a] + w1 * acc_ref[b]) * (1.0 / (w0 * l_ref[a] + w1 * l_ref[b])))
    o_sel = jnp.concatenate(merged, axis=1)

    for z in range(z_heads):
        ls = slice(z * tq, (z + 1) * tq)
        o = part_ref[:, ls] + jax.nn.sigmoid(gt_ref[3 * z + 1:3 * z + 2, :]) * o_sel[:, ls]
        o_ref[0, :, z * HEAD_DIM:(z + 1) * HEAD_DIM] = o.T.astype(o_ref.dtype)


def nsa_attention(proj, gates_t, cmp_kv, n_groups, q_col, ks_col, vs_col, kw_col, vw_col, tq=128, tk=256):
    b, s, _ = proj.shape
    tq, tk = min(tq, s), min(tk, s)
    assert s % (2 * tk) == 0 and tk % tq == 0 and NSA_WINDOW % tq == 0 and tq % NSA_SEL_BLOCK == 0
    assert s >= NSA_WINDOW + tq
    nc = cmp_kv.shape[2]
    n_sel = s // NSA_SEL_BLOCK
    assert nc * NSA_CMP_STRIDE == s and n_sel <= LANES
    c_lo = np.arange(nc) * NSA_CMP_STRIDE
    j_lo = np.arange(LANES) * NSA_SEL_BLOCK
    cover_t = ((c_lo[None, :] < j_lo[:, None] + NSA_SEL_BLOCK) & (c_lo[None, :] + NSA_CMP_LEN > j_lo[:, None])
               & (np.arange(LANES)[:, None] < n_sel) & (c_lo[None, :] + NSA_CMP_LEN <= s))
    expand_t = (np.arange(s)[:, None] // NSA_SEL_BLOCK == np.arange(LANES)[None, :])
    gw = NSA_GROUP * HEAD_DIM
    qo = q_col // gw
    kso, vso, kwo, vwo = (c // HEAD_DIM for c in (ks_col, vs_col, kw_col, vw_col))
    kv = lambda off: pl.BlockSpec((1, s, HEAD_DIM), lambda bi, g, i: (bi, 0, off + g))
    q_tiles = s // tq
    rows = NSA_GROUP * tq
    return pl.pallas_call(
        functools.partial(_nsa_kernel, tq=tq, tk=tk, n_sel=n_sel),
        grid=(b, n_groups, q_tiles),
        in_specs=[pl.BlockSpec((1, tq, gw), lambda bi, g, i: (bi, i, qo + g)),
                  kv(kso), kv(vso), kv(kwo), kv(vwo),
                  pl.BlockSpec((1, 1, nc, HEAD_DIM), lambda bi, g, i: (bi, g, 0, 0)),
                  pl.BlockSpec((1, 1, nc, HEAD_DIM), lambda bi, g, i: (bi, n_groups + g, 0, 0)),
                  pl.BlockSpec((NSA_GATE_PAD, tq), lambda bi, g, i: (g, bi * q_tiles + i)),
                  pl.BlockSpec((LANES, nc), lambda bi, g, i: (0, 0)),
                  pl.BlockSpec((s, LANES), lambda bi, g, i: (0, 0))],
        out_specs=pl.BlockSpec((1, tq, gw), lambda bi, g, i: (bi, i, g)),
        out_shape=jax.ShapeDtypeStruct((b, s, n_groups * gw), BF16),
        scratch_shapes=[pltpu.VMEM((HEAD_DIM, rows), F32),
                        pltpu.VMEM((2 * NSA_LANE_SPLIT, 1, rows // NSA_LANE_SPLIT), F32),
                        pltpu.VMEM((2 * NSA_LANE_SPLIT, 1, rows // NSA_LANE_SPLIT), F32),
                        pltpu.VMEM((2 * NSA_LANE_SPLIT, HEAD_DIM, rows // NSA_LANE_SPLIT), F32)],
        compiler_params=_params("parallel", "parallel", "arbitrary"),
    )(proj, proj, proj, proj, proj, cmp_kv, cmp_kv, gates_t, jnp.asarray(cover_t, BF16),
      jnp.asarray(expand_t, BF16))


def _even_tile_mode(tn, d_model):
    seg_tiles = (d_model // 2) // tn
    return lambda j: ROPE_SCALED - (j // seg_tiles) % 3


def even_mixer(hn, w_in, w_out, e, lam_params, subln_g, lam_init, tables, batch, seq):
    d_model = hn.shape[1]
    seg = d_model // 2
    tn = min(1024, seg)
    proj = proj_rope(hn, w_in, e, 6 * seg, tables, seq, _even_tile_mode(tn, d_model), tn=tn)
    proj = proj.reshape(batch, seq, -1)
    o_diff = diff_attention(proj, lam_params, subln_g, lam_init, seg // DIFF_V_DIM, 0, seg, 2 * seg)
    o_moba = moba_attention(proj, seg // HEAD_DIM, 3 * seg, 4 * seg, 5 * seg)
    rows = batch * seq
    return matmul_cat(o_diff.reshape(rows, seg), o_moba.reshape(rows, seg), w_out, e, F32)


def _odd_tile_mode(tn, d_model):
    slab_tiles = (d_model // 8) // tn
    q_tiles = d_model // tn
    return lambda j: jnp.where(j < q_tiles, ROPE_SCALED, ROPE - ((j - q_tiles) // slab_tiles) % 2)


def odd_mixer(hn, w_in, w_out, o_idx, cmp_pe, cmp_w1, cmp_w2, tables, batch, seq):
    d_model = hn.shape[1]
    g = d_model // (NSA_GROUP * HEAD_DIM)
    kvd = g * HEAD_DIM
    n_main = d_model + 6 * kvd
    tn = min(512, kvd)
    proj = proj_rope(hn, w_in, o_idx, n_main, tables, seq, _odd_tile_mode(tn, d_model), tn=tn)
    proj = proj.reshape(batch, seq, n_main)
    n_gate = 3 * NSA_GROUP
    w_gate_t = w_in[o_idx, :, n_main:].T.reshape(g, n_gate, d_model)
    w_gate_t = jnp.pad(w_gate_t, ((0, 0), (0, NSA_GATE_PAD - n_gate), (0, 0))).reshape(g * NSA_GATE_PAD, d_model)
    gates_t = matmul_nt(w_gate_t, hn)
    nc = seq // NSA_CMP_STRIDE
    t16 = proj[:, :, d_model:d_model + 2 * kvd].reshape(batch, seq, 2 * g, HEAD_DIM)
    t16 = t16.transpose(0, 2, 1, 3).reshape(batch, 2 * g, nc, NSA_CMP_STRIDE * HEAD_DIM)
    cmp_kv = nsa_compress(t16, cmp_pe.reshape(2, 1, NSA_CMP_LEN * HEAD_DIM), cmp_w1, cmp_w2)
    o = nsa_attention(proj, gates_t, cmp_kv, g, 0, d_model + 2 * kvd, d_model + 3 * kvd, d_model + 4 * kvd,
                      d_model + 5 * kvd)
    return matmul(o.reshape(batch * seq, d_model), w_out, o_idx, F32)


def ffn(hn, w_gate, w_up, w_down, layer):
    return matmul(swiglu_up(hn, w_gate, w_up, layer), w_down, layer, F32, tn=256)


def kernel(x, norm_g, ffn_a_gate, ffn_a_up, ffn_a_down, ffn_b_gate, ffn_b_up, ffn_b_down, ev_w_in, ev_w_out, ev_lambda, ev_subln_g, od_w_in, od_w_out, od_cmp_pe, od_cmp_w1, od_cmp_w2):
    batch, seq, d_model = x.shape
    depth = norm_g.shape[0]
    tables = rope_tables(seq)
    h = x.reshape(batch * seq, d_model)
    hn = rmsnorm(h, norm_g[0, 0])
    for layer in range(depth):
        g = norm_g[layer]
        m = ffn(hn, ffn_a_gate, ffn_a_up, ffn_a_down, layer)
        h, hn = residual_norm(h, m, g[1], g[2], 0.5)
        if layer % 2 == 0:
            e = layer // 2
            lam_init = 0.8 - 0.6 * math.exp(-0.3 * layer)
            m = even_mixer(hn, ev_w_in, ev_w_out, e, ev_lambda[e], ev_subln_g[e], lam_init, tables, batch, seq)
        else:
            o = layer // 2
            m = odd_mixer(hn, od_w_in, od_w_out, o, od_cmp_pe[o], od_cmp_w1[o], od_cmp_w2[o], tables, batch, seq)
        h, hn = residual_norm(h, m, g[3], g[4], 1.0)
        m = ffn(hn, ffn_b_gate, ffn_b_up, ffn_b_down, layer)
        g_next = norm_g[layer + 1, 0] if layer + 1 < depth else None
        h, hn = residual_norm(h, m, g[5], g_next, 0.5)
    return h.reshape(batch, seq, d_model)
```

```python
import functools
import math

import numpy as np
import jax
import jax.numpy as jnp
from jax import lax
from jax.experimental import pallas as pl
from jax.experimental.pallas import tpu as pltpu

F32 = jnp.float32
BF16 = jnp.bfloat16

HEAD_DIM = 128
ROT_DIM = HEAD_DIM // 4
ROPE_THETA = 500000.0
RMS_EPS = 1e-6
NEG = -1e30
BIG = 1e30
Q_SCALE = HEAD_DIM ** -0.5 * math.log2(math.e)

DIFF_V_DIM = 2 * HEAD_DIM
MOBA_BLOCK = 256
MOBA_TOPK = 3
NSA_GROUP = 8
NSA_CMP_LEN = 32
NSA_CMP_STRIDE = 16
NSA_SEL_BLOCK = 64
NSA_SEL_TOPN = 16
NSA_WINDOW = 512
NSA_GATE_PAD = 128
NSA_LANE_SPLIT = 2

LANES = 128
SUBLANES = 8
VMEM_LIMIT_BYTES = 56 * 1024 * 1024


def _params(*semantics):
    return pltpu.CompilerParams(dimension_semantics=semantics, vmem_limit_bytes=VMEM_LIMIT_BYTES)


def _dot(a, b):
    return jnp.dot(a, b, preferred_element_type=F32)


def _dot_nt(a, b):
    return lax.dot_general(a, b, (((1,), (1,)), ((), ())), preferred_element_type=F32)


def _dot_tn(a, b):
    return lax.dot_general(a, b, (((0,), (0,)), ((), ())), preferred_element_type=F32)


def _split_bf16(x):
    hi = x.astype(BF16)
    lo = (x - hi.astype(F32)).astype(BF16)
    return hi, lo


def _rms(x, g):
    return x * lax.rsqrt(jnp.mean(x * x, axis=-1, keepdims=True) + RMS_EPS) * g


def _rmsnorm_kernel(x_ref, g_ref, o_ref):
    o_ref[...] = _rms(x_ref[...], g_ref[...]).astype(o_ref.dtype)


def rmsnorm(x, g, tm=256):
    m, d = x.shape
    tm = min(tm, m)
    return pl.pallas_call(
        _rmsnorm_kernel,
        grid=(m // tm,),
        in_specs=[pl.BlockSpec((tm, d), lambda i: (i, 0)), pl.BlockSpec((1, d), lambda i: (0, 0))],
        out_specs=pl.BlockSpec((tm, d), lambda i: (i, 0)),
        out_shape=jax.ShapeDtypeStruct((m, d), BF16),
        compiler_params=_params("parallel"),
    )(x, g.reshape(1, d))


def _residual_kernel(h_ref, m_ref, gpost_ref, gnext_ref, h_out_ref, hn_out_ref, *, coef):
    y = _rms(m_ref[...], gpost_ref[...])
    if coef != 1.0:
        y = coef * y
    h = h_ref[...] + y
    h_out_ref[...] = h
    hn_out_ref[...] = _rms(h, gnext_ref[...]).astype(hn_out_ref.dtype)


def _residual_last_kernel(h_ref, m_ref, gpost_ref, h_out_ref, *, coef):
    y = _rms(m_ref[...], gpost_ref[...])
    if coef != 1.0:
        y = coef * y
    h_out_ref[...] = h_ref[...] + y


def residual_norm(h, mix, g_post, g_next, coef, tm=256):
    m, d = h.shape
    tm = min(tm, m)
    row = pl.BlockSpec((tm, d), lambda i: (i, 0))
    vec = pl.BlockSpec((1, d), lambda i: (0, 0))
    if g_next is None:
        return pl.pallas_call(
            functools.partial(_residual_last_kernel, coef=coef),
            grid=(m // tm,),
            in_specs=[row, row, vec],
            out_specs=row,
            out_shape=jax.ShapeDtypeStruct((m, d), F32),
            compiler_params=_params("parallel"),
        )(h, mix, g_post.reshape(1, d)), None
    return pl.pallas_call(
        functools.partial(_residual_kernel, coef=coef),
        grid=(m // tm,),
        in_specs=[row, row, vec, vec],
        out_specs=[row, row],
        out_shape=[jax.ShapeDtypeStruct((m, d), F32), jax.ShapeDtypeStruct((m, d), BF16)],
        compiler_params=_params("parallel"),
    )(h, mix, g_post.reshape(1, d), g_next.reshape(1, d))


def _mm_kernel(x_ref, w_ref, o_ref):
    o_ref[...] = _dot(x_ref[...], w_ref[...].astype(BF16)).astype(o_ref.dtype)


def matmul(x, w, layer, out_dtype, tm=1024, tn=512):
    m, kd = x.shape
    n = w.shape[2]
    tm, tn = min(tm, m), min(tn, n)
    return pl.pallas_call(
        _mm_kernel,
        grid=(m // tm, n // tn),
        in_specs=[pl.BlockSpec((tm, kd), lambda i, j: (i, 0)),
                  pl.BlockSpec((None, kd, tn), lambda i, j: (layer, 0, j))],
        out_specs=pl.BlockSpec((tm, tn), lambda i, j: (i, j)),
        out_shape=jax.ShapeDtypeStruct((m, n), out_dtype),
        compiler_params=_params("parallel", "parallel"),
    )(x, w)


def _mm_cat_kernel(x1_ref, x2_ref, w_ref, o_ref):
    k1 = x1_ref.shape[1]
    o_ref[...] = (_dot(x1_ref[...], w_ref[:k1, :].astype(BF16))
                  + _dot(x2_ref[...], w_ref[k1:, :].astype(BF16))).astype(o_ref.dtype)


def matmul_cat(x1, x2, w, layer, out_dtype, tm=1024, tn=512):
    m, k1 = x1.shape
    k2 = x2.shape[1]
    n = w.shape[2]
    tm, tn = min(tm, m), min(tn, n)
    return pl.pallas_call(
        _mm_cat_kernel,
        grid=(m // tm, n // tn),
        in_specs=[pl.BlockSpec((tm, k1), lambda i, j: (i, 0)), pl.BlockSpec((tm, k2), lambda i, j: (i, 0)),
                  pl.BlockSpec((None, k1 + k2, tn), lambda i, j: (layer, 0, j))],
        out_specs=pl.BlockSpec((tm, tn), lambda i, j: (i, j)),
        out_shape=jax.ShapeDtypeStruct((m, n), out_dtype),
        compiler_params=_params("parallel", "parallel"),
    )(x1, x2, w)


def _mm_nt_kernel(w_ref, x_ref, o_ref):
    o_ref[...] = _dot_nt(w_ref[...].astype(BF16), x_ref[...])


def matmul_nt(w_t, x, tm=1024):
    n, kd = w_t.shape
    m = x.shape[0]
    tm = min(tm, m)
    return pl.pallas_call(
        _mm_nt_kernel,
        grid=(m // tm,),
        in_specs=[pl.BlockSpec((n, kd), lambda i: (0, 0)), pl.BlockSpec((tm, kd), lambda i: (i, 0))],
        out_specs=pl.BlockSpec((n, tm), lambda i: (0, i)),
        out_shape=jax.ShapeDtypeStruct((n, m), F32),
        compiler_params=_params("parallel"),
    )(w_t, x)


def _swiglu_kernel(x_ref, wg_ref, wu_ref, o_ref):
    x = x_ref[...]
    g = _dot(x, wg_ref[...].astype(BF16))
    u = _dot(x, wu_ref[...].astype(BF16))
    o_ref[...] = (g * jax.nn.sigmoid(g) * u).astype(o_ref.dtype)


def swiglu_up(x, w_gate, w_up, layer, tm=1024, tn=256):
    m, kd = x.shape
    n = w_gate.shape[2]
    tm, tn = min(tm, m), min(tn, n)
    wspec = pl.BlockSpec((None, kd, tn), lambda i, j: (layer, 0, j))
    return pl.pallas_call(
        _swiglu_kernel,
        grid=(m // tm, n // tn),
        in_specs=[pl.BlockSpec((tm, kd), lambda i, j: (i, 0)), wspec, wspec],
        out_specs=pl.BlockSpec((tm, tn), lambda i, j: (i, j)),
        out_shape=jax.ShapeDtypeStruct((m, n), BF16),
        compiler_params=_params("parallel", "parallel"),
    )(x, w_gate, w_up)


def rope_tables(seq):
    half = ROT_DIM // 2
    pos = jnp.arange(seq, dtype=F32)
    inv = ROPE_THETA ** (-jnp.arange(0, ROT_DIM, 2, dtype=F32) / ROT_DIM)
    ang = pos[:, None] * inv[None, :]
    cos, sin = jnp.cos(ang), jnp.sin(ang)
    zeros = jnp.zeros((seq, HEAD_DIM - ROT_DIM), F32)
    zhalf = jnp.zeros((seq, half), F32)
    c = jnp.concatenate([cos, cos, zeros + 1.0], axis=-1)
    s1 = jnp.concatenate([zhalf, sin, zeros], axis=-1)
    s2 = jnp.concatenate([-sin, zhalf, zeros], axis=-1)
    return c, s1, s2


PLAIN, ROPE, ROPE_SCALED = 0, 1, 2


def _proj_rope_kernel(x_ref, w_ref, c_ref, s1_ref, s2_ref, o_ref, *, tile_mode, w_rows_are_outputs):
    mode = tile_mode(pl.program_id(1))
    half = ROT_DIM // 2
    sub = min(2 * HEAD_DIM, o_ref.shape[1])

    def run(rope, scale):
        x = x_ref[...]
        for c0 in range(0, o_ref.shape[1], sub):
            if w_rows_are_outputs:
                acc = _dot_nt(x, w_ref[c0:c0 + sub, :].astype(BF16))
            else:
                acc = _dot(x, w_ref[:, c0:c0 + sub].astype(BF16))
            if not rope:
                o_ref[:, c0:c0 + sub] = acc.astype(o_ref.dtype)
                continue
            c, s1, s2 = c_ref[...], s1_ref[...], s2_ref[...]
            for h0 in range(0, sub, HEAD_DIM):
                xh = acc[:, h0:h0 + HEAD_DIM]
                r = xh * c + pltpu.roll(xh, half, 1) * s1 + pltpu.roll(xh, HEAD_DIM - half, 1) * s2
                if scale != 1.0:
                    r = r * scale
                o_ref[:, c0 + h0:c0 + h0 + HEAD_DIM] = r.astype(o_ref.dtype)

    @pl.when(mode == ROPE_SCALED)
    def _():
        run(True, Q_SCALE)

    @pl.when(mode == ROPE)
    def _():
        run(True, 1.0)

    @pl.when(mode == PLAIN)
    def _():
        run(False, 1.0)


def proj_rope(x, w, layer, n, tables, seq, tile_mode, tm=1024, tn=512, w_rows_are_outputs=False):
    m, kd = x.shape
    tm, tn = min(tm, seq), min(tn, n)
    assert seq % tm == 0 and m % tm == 0 and n % tn == 0
    pos_blocks = seq // tm
    row_bytes = tm * kd * 2 + 3 * tm * HEAD_DIM * 4
    sub = min(2 * HEAD_DIM, tn)
    stream_bytes = 2 * (kd * tn * 4 + tm * tn * 2) + kd * sub * 2 + tm * sub * 4
    rows_mode = {} if 2 * row_bytes + stream_bytes <= VMEM_LIMIT_BYTES else {"pipeline_mode": pl.Buffered(1)}
    tspec = pl.BlockSpec((tm, HEAD_DIM), lambda i, j: (i % pos_blocks, 0), **rows_mode)
    return pl.pallas_call(
        functools.partial(_proj_rope_kernel, tile_mode=tile_mode, w_rows_are_outputs=w_rows_are_outputs),
        grid=(m // tm, n // tn),
        in_specs=[pl.BlockSpec((tm, kd), lambda i, j: (i, 0), **rows_mode),
                  (pl.BlockSpec((None, tn, kd), lambda i, j: (layer, j, 0)) if w_rows_are_outputs
                   else pl.BlockSpec((None, kd, tn), lambda i, j: (layer, 0, j))),
                  tspec, tspec, tspec],
        out_specs=pl.BlockSpec((tm, tn), lambda i, j: (i, j)),
        out_shape=jax.ShapeDtypeStruct((m, n), BF16),
        compiler_params=_params("parallel", "parallel"),
    )(x, w, *tables)


def _flash_init_many(slots, s_list, v_list, m_ref, l_ref, acc_ref):
    m = [jnp.max(s, axis=0, keepdims=True) for s in s_list]
    p = [jnp.exp2(s - mm) for s, mm in zip(s_list, m)]
    pv = [_dot_tn(v, pp.astype(v.dtype)) for v, pp in zip(v_list, p)]
    for idx, c in enumerate(slots):
        m_ref[c] = m[idx]
        l_ref[c] = jnp.sum(p[idx], axis=0, keepdims=True)
        acc_ref[c] = pv[idx]


def _flash_update_many(slots, s_list, v_list, m_ref, l_ref, acc_ref):
    m_old = [m_ref[c] for c in slots]
    l_old = [l_ref[c] for c in slots]
    m_new = [jnp.maximum(mo, jnp.max(s, axis=0, keepdims=True)) for mo, s in zip(m_old, s_list)]
    p = [jnp.exp2(s - mn) for s, mn in zip(s_list, m_new)]
    pv = [_dot_tn(v, pp.astype(v.dtype)) for v, pp in zip(v_list, p)]
    for idx, c in enumerate(slots):
        alpha = jnp.exp2(m_old[idx] - m_new[idx])
        m_ref[c] = m_new[idx]
        l_ref[c] = alpha * l_old[idx] + jnp.sum(p[idx], axis=0, keepdims=True)
        acc_ref[c] = alpha * acc_ref[c] + pv[idx]


def _flash_result(slot, l_ref, acc_ref):
    return acc_ref[slot] * (1.0 / l_ref[slot])


def _softmax_pv(s_parts, v):
    e = [jnp.exp2(s - jnp.max(s, axis=0, keepdims=True)) for s in s_parts]
    pv = [_dot_tn(v, p.astype(v.dtype)) for p in e]
    return jnp.concatenate([o * (1.0 / jnp.sum(p, axis=0, keepdims=True)) for o, p in zip(pv, e)], axis=1)


def _pad_rows(n_rows):
    return -(-n_rows // SUBLANES) * SUBLANES


def _rank_before(v, n_rows):
    row = lax.broadcasted_iota(jnp.int32, v.shape, 0)
    rank = jnp.zeros(v.shape, jnp.int32)
    for j2 in range(n_rows):
        other = v[j2:j2 + 1, :]
        rank = rank + jnp.where(other > v, 1, jnp.where(other == v, (row > j2).astype(jnp.int32), 0))
    return rank


def _diff_kernel(lam_ref, g_ref, q_ref, k_ref, v_ref, o_ref, m_ref, l_ref, acc_ref, *, tq, hp, lam_init):
    i = pl.program_id(2)
    w = DIFF_V_DIM
    lp = lam_ref[...]
    lam = (jnp.exp(jnp.sum(lp[0:1] * lp[1:2], axis=-1, keepdims=True))
           - jnp.exp(jnp.sum(lp[2:3] * lp[3:4], axis=-1, keepdims=True)) + lam_init)
    key = lax.broadcasted_iota(jnp.int32, (tq, tq), 0)
    qry = lax.broadcasted_iota(jnp.int32, (tq, tq), 1)

    def scores(c, start):
        cols = slice(c * HEAD_DIM, (c + 1) * HEAD_DIM)
        return _dot_nt(k_ref[0, pl.ds(start, tq), cols], q_ref[0, :, cols])

    own = pl.multiple_of(i * tq, tq)
    s_own = [jnp.where(key <= qry, scores(c, own), NEG) for c in range(2 * hp)]
    v_own = [v_ref[0, pl.ds(own, tq), (c // 2) * w:(c // 2 + 1) * w] for c in range(2 * hp)]
    _flash_init_many(range(2 * hp), s_own, v_own, m_ref, l_ref, acc_ref)

    def body(j, carry):
        start = pl.multiple_of(j * tq, tq)
        s = [scores(c, start) for c in range(2 * hp)]
        v = [v_ref[0, pl.ds(start, tq), (c // 2) * w:(c // 2 + 1) * w] for c in range(2 * hp)]
        _flash_update_many(range(2 * hp), s, v, m_ref, l_ref, acc_ref)
        return carry

    lax.fori_loop(0, i, body, 0)
    for h in range(hp):
        o = (_flash_result(2 * h, l_ref, acc_ref) - lam * _flash_result(2 * h + 1, l_ref, acc_ref)).T
        o_ref[0, :, h * w:(h + 1) * w] = (_rms(o, g_ref[...]) * (1.0 - lam_init)).astype(o_ref.dtype)


def diff_attention(proj, lam_params, subln_g, lam_init, n_heads, q_col, k_col, v_col, tq=256, hp=4):
    b, s, _ = proj.shape
    tq = min(tq, s)
    w = DIFF_V_DIM
    hp = min(hp, n_heads)
    assert n_heads % hp == 0
    gw = hp * w
    qo, ko, vo = q_col // gw, k_col // gw, v_col // gw
    return pl.pallas_call(
        functools.partial(_diff_kernel, tq=tq, hp=hp, lam_init=lam_init),
        grid=(b, n_heads // hp, s // tq),
        in_specs=[pl.BlockSpec((4, HEAD_DIM), lambda bi, h, i: (0, 0)),
                  pl.BlockSpec((1, w), lambda bi, h, i: (0, 0)),
                  pl.BlockSpec((1, tq, gw), lambda bi, h, i: (bi, i, qo + h)),
                  pl.BlockSpec((1, s, gw), lambda bi, h, i: (bi, 0, ko + h)),
                  pl.BlockSpec((1, s, gw), lambda bi, h, i: (bi, 0, vo + h))],
        out_specs=pl.BlockSpec((1, tq, gw), lambda bi, h, i: (bi, i, h)),
        out_shape=jax.ShapeDtypeStruct((b, s, n_heads * w), BF16),
        scratch_shapes=[pltpu.VMEM((2 * hp, 1, tq), F32), pltpu.VMEM((2 * hp, 1, tq), F32),
                        pltpu.VMEM((2 * hp, w, tq), F32)],
        compiler_params=_params("parallel", "parallel", "arbitrary"),
    )(lam_params, subln_g.reshape(1, w), proj, proj, proj)


def _moba_kernel(q_ref, k_ref, v_ref, o_ref, kmean_ref, bias_ref, m_ref, l_ref, acc_ref, *, nb, hp):
    i = pl.program_id(2)
    blk = MOBA_BLOCK
    head = lambda h: slice(h * HEAD_DIM, (h + 1) * HEAD_DIM)

    @pl.when(i == 0)
    def _():
        kmean_ref[...] = jnp.zeros(kmean_ref.shape, F32)
        for h in range(hp):
            for n in range(nb):
                kb = k_ref[0, n * blk:(n + 1) * blk, head(h)].astype(F32)
                kmean_ref[h, n:n + 1, :] = jnp.mean(kb, axis=0, keepdims=True)

    key = lax.broadcasted_iota(jnp.int32, (blk, blk), 0)
    qry = lax.broadcasted_iota(jnp.int32, (blk, blk), 1)
    own = pl.multiple_of(i * blk, blk)
    qs = [q_ref[0, :, head(h)] for h in range(hp)]
    s_own = [jnp.where(key <= qry, _dot_nt(k_ref[0, pl.ds(own, blk), head(h)], qs[h]), NEG) for h in range(hp)]
    v_own = [v_ref[0, pl.ds(own, blk), head(h)] for h in range(hp)]
    biases = []
    for h in range(hp):
        km_hi, km_lo = _split_bf16(kmean_ref[h])
        gate = (_dot_nt(km_hi, qs[h]) + _dot_nt(km_lo, qs[h]))[:_pad_rows(nb)]
        past = lax.broadcasted_iota(jnp.int32, gate.shape, 0) < i
        rank = _rank_before(jnp.where(past, gate, NEG), nb)
        biases.append(jnp.where(past & (rank < MOBA_TOPK), 0.0, NEG))
    for h in range(hp):
        bias_ref[h] = biases[h]
    _flash_init_many(range(hp), s_own, v_own, m_ref, l_ref, acc_ref)

    def body(n, carry):
        start = pl.multiple_of(n * blk, blk)
        sn = [_dot_nt(k_ref[0, pl.ds(start, blk), head(h)], q_ref[0, :, head(h)]) + bias_ref[h, pl.ds(n, 1), :]
              for h in range(hp)]
        vn = [v_ref[0, pl.ds(start, blk), head(h)] for h in range(hp)]
        _flash_update_many(range(hp), sn, vn, m_ref, l_ref, acc_ref)
        return carry

    lax.fori_loop(0, i, body, 0)
    for h in range(hp):
        o_ref[0, :, head(h)] = _flash_result(h, l_ref, acc_ref).T.astype(o_ref.dtype)


def moba_attention(proj, n_heads, q_col, k_col, v_col, hp=8):
    b, s, _ = proj.shape
    blk = MOBA_BLOCK
    hp = min(hp, n_heads)
    assert s % blk == 0 and s // blk <= LANES and n_heads % hp == 0
    nb = s // blk
    gw = hp * HEAD_DIM
    qo, ko, vo = q_col // gw, k_col // gw, v_col // gw
    return pl.pallas_call(
        functools.partial(_moba_kernel, nb=nb, hp=hp),
        grid=(b, n_heads // hp, nb),
        in_specs=[pl.BlockSpec((1, blk, gw), lambda bi, h, i: (bi, i, qo + h)),
                  pl.BlockSpec((1, s, gw), lambda bi, h, i: (bi, 0, ko + h)),
                  pl.BlockSpec((1, s, gw), lambda bi, h, i: (bi, 0, vo + h))],
        out_specs=pl.BlockSpec((1, blk, gw), lambda bi, h, i: (bi, i, h)),
        out_shape=jax.ShapeDtypeStruct((b, s, n_heads * HEAD_DIM), BF16),
        scratch_shapes=[pltpu.VMEM((hp, LANES, HEAD_DIM), F32), pltpu.VMEM((hp, _pad_rows(nb), blk), F32),
                        pltpu.VMEM((hp, 1, blk), F32), pltpu.VMEM((hp, 1, blk), F32),
                        pltpu.VMEM((hp, HEAD_DIM, blk), F32)],
        compiler_params=_params("parallel", "parallel", "arbitrary"),
    )(proj, proj, proj)


def _compress_kernel(t_ref, pe_ref, w1_ref, w2_ref, o_ref):
    t = t_ref[0, 0].astype(F32)
    half = t.shape[1]
    top = (t + pe_ref[0, :, :half]).astype(BF16)
    bot = (t + pe_ref[0, :, half:]).astype(BF16)
    a = _dot(top, w1_ref[0, :half, :].astype(BF16))
    bm = _dot(bot, w1_ref[0, half:, :].astype(BF16))
    hidden = a + pltpu.roll(bm, bm.shape[0] - 1, 0)
    act = hidden * jax.nn.sigmoid(hidden)
    o_ref[0, 0] = _dot(act.astype(BF16), w2_ref[0].astype(BF16)).astype(o_ref.dtype)


def nsa_compress(t16, pe_flat, w1, w2):
    b, g2, nc, width = t16.shape
    g = g2 // 2
    hid = w1.shape[-1]
    return pl.pallas_call(
        _compress_kernel,
        grid=(b, g2),
        in_specs=[pl.BlockSpec((1, 1, nc, width), lambda bi, j: (bi, j, 0, 0)),
                  pl.BlockSpec((1, 1, 2 * width), lambda bi, j: (j // g, 0, 0)),
                  pl.BlockSpec((1, 2 * width, hid), lambda bi, j: (j // g, 0, 0)),
                  pl.BlockSpec((1, hid, HEAD_DIM), lambda bi, j: (j // g, 0, 0))],
        out_specs=pl.BlockSpec((1, 1, nc, HEAD_DIM), lambda bi, j: (bi, j, 0, 0)),
        out_shape=jax.ShapeDtypeStruct((b, g2, nc, HEAD_DIM), BF16),
        compiler_params=_params("parallel", "parallel"),
    )(t16, pe_flat, w1, w2)


def _lanes_x(a, n):
    return jnp.concatenate([a] * n, axis=1)


def _nsa_kernel(q_ref, ks_ref, vs_ref, kw_ref, vw_ref, kc_ref, vc_ref, gt_ref, cover_ref, expand_ref,
                o_ref, part_ref, m_ref, l_ref, acc_ref, *, tq, tk, n_sel):
    i = pl.program_id(2)
    z_heads = NSA_GROUP
    nc = kc_ref.shape[2]
    q = jnp.concatenate([q_ref[0, :, z * HEAD_DIM:(z + 1) * HEAD_DIM] for z in range(z_heads)], axis=0)
    qpos = i * tq + lax.broadcasted_iota(jnp.int32, (1, tq), 1)

    cmp_end = lax.broadcasted_iota(jnp.int32, (nc, tq), 0) * NSA_CMP_STRIDE + (NSA_CMP_LEN - 1)
    cmp_ok = _lanes_x(jnp.where(cmp_end <= qpos, 1.0, 0.0), z_heads)
    sc = jnp.where(cmp_ok > 0.5, _dot_nt(kc_ref[0, 0], q), NEG)
    pc = jnp.exp2(sc - jnp.max(sc, axis=0, keepdims=True))
    pc = pc * (1.0 / jnp.sum(pc, axis=0, keepdims=True)) * cmp_ok
    o_cmp = _dot_tn(vc_ref[0, 0], pc.astype(BF16))

    p_sum = pc[:, 0:tq]
    for z in range(1, z_heads):
        p_sum = p_sum + pc[:, z * tq:(z + 1) * tq]
    p_hi, p_lo = _split_bf16(p_sum)
    imp = (_dot(cover_ref[...], p_hi) + _dot(cover_ref[...], p_lo))[:_pad_rows(n_sel)]
    row = lax.broadcasted_iota(jnp.int32, imp.shape, 0)
    qblk = qpos // NSA_SEL_BLOCK
    forced = (row == 0) | (row == qblk) | (row == qblk - 1)
    allowed = row <= qblk
    imp = jnp.where(forced, BIG, jnp.where(allowed, imp, NEG))
    rank = _rank_before(imp, n_sel)
    block_bias = jnp.where(allowed & (rank < NSA_SEL_TOPN), 0.0, NEG)
    block_bias = jnp.concatenate(
        [block_bias, jnp.zeros((LANES - block_bias.shape[0], tq), F32)], axis=0).T.astype(BF16)
    q2 = jnp.concatenate([q, jnp.concatenate([block_bias] * z_heads, axis=0)], axis=1)
    span = NSA_WINDOW + tq
    w_start = pl.multiple_of(jnp.maximum(i * tq - NSA_WINDOW, 0), tq)
    dist = qpos - (w_start + lax.broadcasted_iota(jnp.int32, (span, tq), 0))
    w_bias = jnp.where((dist >= 0) & (dist < NSA_WINDOW), 0.0, NEG)
    kw = kw_ref[0, pl.ds(w_start, span), :]
    half_rows = z_heads * tq // 2
    sw = [_dot_nt(kw, q[p * half_rows:(p + 1) * half_rows]) + _lanes_x(w_bias, z_heads // 2) for p in range(2)]
    o_win = _softmax_pv(sw, vw_ref[0, pl.ds(w_start, span), :])
    gt = jax.nn.sigmoid(gt_ref[...])
    part_ref[...] = jnp.concatenate(
        [gt[3 * z:3 * z + 1] * o_cmp[:, z * tq:(z + 1) * tq] + gt[3 * z + 2:3 * z + 3] * o_win[:, z * tq:(z + 1) * tq]
         for z in range(z_heads)], axis=1)

    n_split = NSA_LANE_SPLIT
    part_w = z_heads * tq // n_split
    halves = tuple(slice(p * part_w, (p + 1) * part_w) for p in range(n_split))

    def sel_pair(t, causal, tiles=(0, 1)):
        s, v = [], []
        for c in tiles:
            start = (2 * t + c) * tk
            if not isinstance(start, int):
                start = pl.multiple_of(start, tk)
            k2 = jnp.concatenate([ks_ref[0, pl.ds(start, tk), :], expand_ref[pl.ds(start, tk), :]], axis=1)
            sc = _dot_nt(k2, q2)
            if causal:
                kpos = start + lax.broadcasted_iota(jnp.int32, (tk, tq), 0)
                sc = sc + _lanes_x(jnp.where(kpos <= qpos, 0.0, NEG), z_heads)
            vt = vs_ref[0, pl.ds(start, tk), :]
            for lanes in halves:
                s.append(sc[:, lanes])
                v.append(vt)
        return s, v

    n_pairs = ((i + 1) * tq + 2 * tk - 1) // (2 * tk)
    s0, v0 = sel_pair(0, True)
    _flash_init_many(range(2 * n_split), s0, v0, m_ref, l_ref, acc_ref)

    def sel_body(t, carry):
        st, vt = sel_pair(t, False)
        _flash_update_many(range(2 * n_split), st, vt, m_ref, l_ref, acc_ref)
        return carry

    lax.fori_loop(1, n_pairs - 1, sel_body, 0)

    odd_needed = (2 * n_pairs - 1) * tk < (i + 1) * tq

    @pl.when((n_pairs > 1) & odd_needed)
    def _():
        st, vt = sel_pair(n_pairs - 1, True)
        _flash_update_many(range(2 * n_split), st, vt, m_ref, l_ref, acc_ref)

    @pl.when((n_pairs > 1) & jnp.logical_not(odd_needed))
    def _():
        st, vt = sel_pair(n_pairs - 1, True, tiles=(0,))
        _flash_update_many(range(n_split), st, vt, m_ref, l_ref, acc_ref)

    merged = []
    for a in range(n_split):
        b = n_split + a
        m_all = jnp.maximum(m_ref[a], m_ref[b])
        w0, w1 = jnp.exp2(m_ref[a] - m_all), jnp.exp2(m_ref[b] - m_all)
        merged.append((w0 * acc_ref[a] + w1 * acc_ref[b]) * (1.0 / (w0 * l_ref[a] + w1 * l_ref[b])))
    o_sel = jnp.concatenate(merged, axis=1)

    for z in range(z_heads):
        ls = slice(z * tq, (z + 1) * tq)
        o = part_ref[:, ls] + jax.nn.sigmoid(gt_ref[3 * z + 1:3 * z + 2, :]) * o_sel[:, ls]
        o_ref[0, :, z * HEAD_DIM:(z + 1) * HEAD_DIM] = o.T.astype(o_ref.dtype)


def nsa_attention(proj, gates_t, cmp_kv, n_groups, q_col, ks_col, vs_col, kw_col, vw_col, tq=128, tk=256):
    b, s, _ = proj.shape
    tq, tk = min(tq, s), min(tk, s)
    assert s % (2 * tk) == 0 and tk % tq == 0 and NSA_WINDOW % tq == 0 and tq % NSA_SEL_BLOCK == 0
    assert s >= NSA_WINDOW + tq
    nc = cmp_kv.shape[2]
    n_sel = s // NSA_SEL_BLOCK
    assert nc * NSA_CMP_STRIDE == s and n_sel <= LANES
    c_lo = np.arange(nc) * NSA_CMP_STRIDE
    j_lo = np.arange(LANES) * NSA_SEL_BLOCK
    cover_t = ((c_lo[None, :] < j_lo[:, None] + NSA_SEL_BLOCK) & (c_lo[None, :] + NSA_CMP_LEN > j_lo[:, None])
               & (np.arange(LANES)[:, None] < n_sel) & (c_lo[None, :] + NSA_CMP_LEN <= s))
    expand_t = (np.arange(s)[:, None] // NSA_SEL_BLOCK == np.arange(LANES)[None, :])
    gw = NSA_GROUP * HEAD_DIM
    qo = q_col // gw
    kso, vso, kwo, vwo = (c // HEAD_DIM for c in (ks_col, vs_col, kw_col, vw_col))
    kv = lambda off: pl.BlockSpec((1, s, HEAD_DIM), lambda bi, g, i: (bi, 0, off + g))
    q_tiles = s // tq
    rows = NSA_GROUP * tq
    return pl.pallas_call(
        functools.partial(_nsa_kernel, tq=tq, tk=tk, n_sel=n_sel),
        grid=(b, n_groups, q_tiles),
        in_specs=[pl.BlockSpec((1, tq, gw), lambda bi, g, i: (bi, i, qo + g)),
                  kv(kso), kv(vso), kv(kwo), kv(vwo),
                  pl.BlockSpec((1, 1, nc, HEAD_DIM), lambda bi, g, i: (bi, g, 0, 0)),
                  pl.BlockSpec((1, 1, nc, HEAD_DIM), lambda bi, g, i: (bi, n_groups + g, 0, 0)),
                  pl.BlockSpec((NSA_GATE_PAD, tq), lambda bi, g, i: (g, bi * q_tiles + i)),
                  pl.BlockSpec((LANES, nc), lambda bi, g, i: (0, 0)),
                  pl.BlockSpec((s, LANES), lambda bi, g, i: (0, 0))],
        out_specs=pl.BlockSpec((1, tq, gw), lambda bi, g, i: (bi, i, g)),
        out_shape=jax.ShapeDtypeStruct((b, s, n_groups * gw), BF16),
        scratch_shapes=[pltpu.VMEM((HEAD_DIM, rows), F32),
                        pltpu.VMEM((2 * NSA_LANE_SPLIT, 1, rows // NSA_LANE_SPLIT), F32),
                        pltpu.VMEM((2 * NSA_LANE_SPLIT, 1, rows // NSA_LANE_SPLIT), F32),
                        pltpu.VMEM((2 * NSA_LANE_SPLIT, HEAD_DIM, rows // NSA_LANE_SPLIT), F32)],
        compiler_params=_params("parallel", "parallel", "arbitrary"),
    )(proj, proj, proj, proj, proj, cmp_kv, cmp_kv, gates_t, jnp.asarray(cover_t, BF16),
      jnp.asarray(expand_t, BF16))


def _even_tile_mode(tn, d_model):
    seg_tiles = (d_model // 2) // tn
    return lambda j: ROPE_SCALED - (j // seg_tiles) % 3


def even_mixer(hn, w_in, w_out, e, lam_params, subln_g, lam_init, tables, batch, seq):
    d_model = hn.shape[1]
    seg = d_model // 2
    tn = min(1024, seg)
    proj = proj_rope(hn, w_in, e, 6 * seg, tables, seq, _even_tile_mode(tn, d_model), tn=tn)
    proj = proj.reshape(batch, seq, -1)
    o_diff = diff_attention(proj, lam_params, subln_g, lam_init, seg // DIFF_V_DIM, 0, seg, 2 * seg)
    o_moba = moba_attention(proj, seg // HEAD_DIM, 3 * seg, 4 * seg, 5 * seg)
    rows = batch * seq
    return matmul_cat(o_diff.reshape(rows, seg), o_moba.reshape(rows, seg), w_out, e, F32)


def _odd_tile_mode(tn, d_model):
    slab_tiles = (d_model // 8) // tn
    q_tiles = d_model // tn
    return lambda j: jnp.where(j < q_tiles, ROPE_SCALED, ROPE - ((j - q_tiles) // slab_tiles) % 2)


def odd_mixer(hn, w_in, w_out, o_idx, cmp_pe, cmp_w1, cmp_w2, tables, batch, seq):
    d_model = hn.shape[1]
    g = d_model // (NSA_GROUP * HEAD_DIM)
    kvd = g * HEAD_DIM
    n_main = d_model + 6 * kvd
    tn = min(512, kvd)
    w_in_t = jnp.swapaxes(w_in, 1, 2)
    proj = proj_rope(hn, w_in_t, o_idx, n_main, tables, seq, _odd_tile_mode(tn, d_model), tn=tn,
                     w_rows_are_outputs=True)
    proj = proj.reshape(batch, seq, n_main)
    n_gate = 3 * NSA_GROUP
    w_gate_t = w_in_t[o_idx, n_main:, :].reshape(g, n_gate, d_model)
    w_gate_t = jnp.pad(w_gate_t, ((0, 0), (0, NSA_GATE_PAD - n_gate), (0, 0))).reshape(g * NSA_GATE_PAD, d_model)
    gates_t = matmul_nt(w_gate_t, hn)
    nc = seq // NSA_CMP_STRIDE
    t16 = proj[:, :, d_model:d_model + 2 * kvd].reshape(batch, seq, 2 * g, HEAD_DIM)
    t16 = t16.transpose(0, 2, 1, 3).reshape(batch, 2 * g, nc, NSA_CMP_STRIDE * HEAD_DIM)
    cmp_kv = nsa_compress(t16, cmp_pe.reshape(2, 1, NSA_CMP_LEN * HEAD_DIM), cmp_w1, cmp_w2)
    o = nsa_attention(proj, gates_t, cmp_kv, g, 0, d_model + 2 * kvd, d_model + 3 * kvd, d_model + 4 * kvd,
                      d_model + 5 * kvd)
    return matmul(o.reshape(batch * seq, d_model), w_out, o_idx, F32)


def ffn(hn, w_gate, w_up, w_down, layer):
    return matmul(swiglu_up(hn, w_gate, w_up, layer), w_down, layer, F32, tn=256)


def kernel(x, norm_g, ffn_a_gate, ffn_a_up, ffn_a_down, ffn_b_gate, ffn_b_up, ffn_b_down, ev_w_in, ev_w_out, ev_lambda, ev_subln_g, od_w_in, od_w_out, od_cmp_pe, od_cmp_w1, od_cmp_w2):
    batch, seq, d_model = x.shape
    depth = norm_g.shape[0]
    tables = rope_tables(seq)
    h = x.reshape(batch * seq, d_model)
    hn = rmsnorm(h, norm_g[0, 0])
    for layer in range(depth):
        g = norm_g[layer]
        m = ffn(hn, ffn_a_gate, ffn_a_up, ffn_a_down, layer)
        h, hn = residual_norm(h, m, g[1], g[2], 0.5)
        if layer % 2 == 0:
            e = layer // 2
            lam_init = 0.8 - 0.6 * math.exp(-0.3 * layer)
            m = even_mixer(hn, ev_w_in, ev_w_out, e, ev_lambda[e], ev_subln_g[e], lam_init, tables, batch, seq)
        else:
            o = layer // 2
            m = odd_mixer(hn, od_w_in, od_w_out, o, od_cmp_pe[o], od_cmp_w1[o], od_cmp_w2[o], tables, batch, seq)
        h, hn = residual_norm(h, m, g[3], g[4], 1.0)
        m = ffn(hn, ffn_b_gate, ffn_b_up, ffn_b_down, layer)
        g_next = norm_g[layer + 1, 0] if layer + 1 < depth else None
        h, hn = residual_norm(h, m, g[5], g_next, 0.5)
    return h.reshape(batch, seq, d_model)
```

```python
import functools
import math

import numpy as np
import jax
import jax.numpy as jnp
from jax import lax
from jax.experimental import pallas as pl
from jax.experimental.pallas import tpu as pltpu

F32 = jnp.float32
BF16 = jnp.bfloat16

HEAD_DIM = 128
ROT_DIM = HEAD_DIM // 4
ROPE_THETA = 500000.0
RMS_EPS = 1e-6
NEG = -1e30
BIG = 1e30
Q_SCALE = HEAD_DIM ** -0.5 * math.log2(math.e)

DIFF_V_DIM = 2 * HEAD_DIM
MOBA_BLOCK = 256
MOBA_TOPK = 3
NSA_GROUP = 8
NSA_CMP_LEN = 32
NSA_CMP_STRIDE = 16
NSA_SEL_BLOCK = 64
NSA_SEL_TOPN = 16
NSA_WINDOW = 512
NSA_GATE_PAD = 128
NSA_LANE_SPLIT = 2

LANES = 128
SUBLANES = 8
VMEM_LIMIT_BYTES = 56 * 1024 * 1024


def _params(*semantics):
    return pltpu.CompilerParams(dimension_semantics=semantics, vmem_limit_bytes=VMEM_LIMIT_BYTES)


def _dot(a, b):
    return jnp.dot(a, b, preferred_element_type=F32)


def _dot_nt(a, b):
    return lax.dot_general(a, b, (((1,), (1,)), ((), ())), preferred_element_type=F32)


def _dot_tn(a, b):
    return lax.dot_general(a, b, (((0,), (0,)), ((), ())), preferred_element_type=F32)


def _split_bf16(x):
    hi = x.astype(BF16)
    lo = (x - hi.astype(F32)).astype(BF16)
    return hi, lo


def _rms(x, g):
    return x * lax.rsqrt(jnp.mean(x * x, axis=-1, keepdims=True) + RMS_EPS) * g


def _rmsnorm_kernel(x_ref, g_ref, o_ref):
    o_ref[...] = _rms(x_ref[...], g_ref[...]).astype(o_ref.dtype)


def rmsnorm(x, g, tm=256):
    m, d = x.shape
    tm = min(tm, m)
    return pl.pallas_call(
        _rmsnorm_kernel,
        grid=(m // tm,),
        in_specs=[pl.BlockSpec((tm, d), lambda i: (i, 0)), pl.BlockSpec((1, d), lambda i: (0, 0))],
        out_specs=pl.BlockSpec((tm, d), lambda i: (i, 0)),
        out_shape=jax.ShapeDtypeStruct((m, d), BF16),
        compiler_params=_params("parallel"),
    )(x, g.reshape(1, d))


def _residual_kernel(h_ref, m_ref, gpost_ref, gnext_ref, h_out_ref, hn_out_ref, *, coef):
    y = _rms(m_ref[...], gpost_ref[...])
    if coef != 1.0:
        y = coef * y
    h = h_ref[...] + y
    h_out_ref[...] = h
    hn_out_ref[...] = _rms(h, gnext_ref[...]).astype(hn_out_ref.dtype)


def _residual_last_kernel(h_ref, m_ref, gpost_ref, h_out_ref, *, coef):
    y = _rms(m_ref[...], gpost_ref[...])
    if coef != 1.0:
        y = coef * y
    h_out_ref[...] = h_ref[...] + y


def residual_norm(h, mix, g_post, g_next, coef, tm=256):
    m, d = h.shape
    tm = min(tm, m)
    row = pl.BlockSpec((tm, d), lambda i: (i, 0))
    vec = pl.BlockSpec((1, d), lambda i: (0, 0))
    if g_next is None:
        return pl.pallas_call(
            functools.partial(_residual_last_kernel, coef=coef),
            grid=(m // tm,),
            in_specs=[row, row, vec],
            out_specs=row,
            out_shape=jax.ShapeDtypeStruct((m, d), F32),
            compiler_params=_params("parallel"),
        )(h, mix, g_post.reshape(1, d)), None
    return pl.pallas_call(
        functools.partial(_residual_kernel, coef=coef),
        grid=(m // tm,),
        in_specs=[row, row, vec, vec],
        out_specs=[row, row],
        out_shape=[jax.ShapeDtypeStruct((m, d), F32), jax.ShapeDtypeStruct((m, d), BF16)],
        compiler_params=_params("parallel"),
    )(h, mix, g_post.reshape(1, d), g_next.reshape(1, d))


def _mm_kernel(x_ref, w_ref, o_ref):
    o_ref[...] = _dot(x_ref[...], w_ref[...].astype(BF16)).astype(o_ref.dtype)


def matmul(x, w, layer, out_dtype, tm=1024, tn=512):
    m, kd = x.shape
    n = w.shape[2]
    tm, tn = min(tm, m), min(tn, n)
    return pl.pallas_call(
        _mm_kernel,
        grid=(m // tm, n // tn),
        in_specs=[pl.BlockSpec((tm, kd), lambda i, j: (i, 0)),
                  pl.BlockSpec((None, kd, tn), lambda i, j: (layer, 0, j))],
        out_specs=pl.BlockSpec((tm, tn), lambda i, j: (i, j)),
        out_shape=jax.ShapeDtypeStruct((m, n), out_dtype),
        compiler_params=_params("parallel", "parallel"),
    )(x, w)


def _mm_cat_kernel(x1_ref, x2_ref, w_ref, o_ref):
    k1 = x1_ref.shape[1]
    o_ref[...] = (_dot(x1_ref[...], w_ref[:k1, :].astype(BF16))
                  + _dot(x2_ref[...], w_ref[k1:, :].astype(BF16))).astype(o_ref.dtype)


def matmul_cat(x1, x2, w, layer, out_dtype, tm=1024, tn=512):
    m, k1 = x1.shape
    k2 = x2.shape[1]
    n = w.shape[2]
    tm, tn = min(tm, m), min(tn, n)
    return pl.pallas_call(
        _mm_cat_kernel,
        grid=(m // tm, n // tn),
        in_specs=[pl.BlockSpec((tm, k1), lambda i, j: (i, 0)), pl.BlockSpec((tm, k2), lambda i, j: (i, 0)),
                  pl.BlockSpec((None, k1 + k2, tn), lambda i, j: (layer, 0, j))],
        out_specs=pl.BlockSpec((tm, tn), lambda i, j: (i, j)),
        out_shape=jax.ShapeDtypeStruct((m, n), out_dtype),
        compiler_params=_params("parallel", "parallel"),
    )(x1, x2, w)


def _mm_nt_kernel(w_ref, x_ref, o_ref):
    o_ref[...] = _dot_nt(w_ref[...].astype(BF16), x_ref[...])


def matmul_nt(w_t, x, tm=1024):
    n, kd = w_t.shape
    m = x.shape[0]
    tm = min(tm, m)
    return pl.pallas_call(
        _mm_nt_kernel,
        grid=(m // tm,),
        in_specs=[pl.BlockSpec((n, kd), lambda i: (0, 0)), pl.BlockSpec((tm, kd), lambda i: (i, 0))],
        out_specs=pl.BlockSpec((n, tm), lambda i: (0, i)),
        out_shape=jax.ShapeDtypeStruct((n, m), F32),
        compiler_params=_params("parallel"),
    )(w_t, x)


def _swiglu_kernel(x_ref, wg_ref, wu_ref, wd_ref, o_ref, wd_bf_ref):
    x = x_ref[...]
    g = _dot(x, wg_ref[...].astype(BF16))
    u = _dot(x, wu_ref[...].astype(BF16))
    o_ref[...] = (g * jax.nn.sigmoid(g) * u).astype(o_ref.dtype)
    wd_bf_ref[...] = wd_ref[...].astype(BF16)


def swiglu_up(x, w_gate, w_up, w_down, layer, tm=1024, tn=256):
    m, kd = x.shape
    n = w_gate.shape[2]
    tm, tn = min(tm, m), min(tn, n)
    n_j = n // tn
    steps = (m // tm) * n_j
    d_rows, d_cols = w_down.shape[1:]
    slab = d_rows // steps
    assert slab * steps == d_rows and slab % 16 == 0
    wspec = pl.BlockSpec((None, kd, tn), lambda i, j: (layer, 0, j))
    return pl.pallas_call(
        _swiglu_kernel,
        grid=(m // tm, n_j),
        in_specs=[pl.BlockSpec((tm, kd), lambda i, j: (i, 0)), wspec, wspec,
                  pl.BlockSpec((None, slab, d_cols), lambda i, j: (layer, i * n_j + j, 0))],
        out_specs=[pl.BlockSpec((tm, tn), lambda i, j: (i, j)),
                   pl.BlockSpec((slab, d_cols), lambda i, j: (i * n_j + j, 0))],
        out_shape=[jax.ShapeDtypeStruct((m, n), BF16), jax.ShapeDtypeStruct((d_rows, d_cols), BF16)],
        compiler_params=_params("parallel", "parallel"),
    )(x, w_gate, w_up, w_down)


def _mm_ksplit_kernel(x_ref, w_ref, o_ref):
    k = pl.program_id(2)

    @pl.when(k == 0)
    def _():
        o_ref[...] = _dot(x_ref[...], w_ref[...])

    @pl.when(k > 0)
    def _():
        o_ref[...] += _dot(x_ref[...], w_ref[...])


def matmul_ksplit(x, w, k_splits, tm=1024, tn=1024):
    m, kd = x.shape
    n = w.shape[1]
    tm, tn, tk = min(tm, m), min(tn, n), kd // k_splits
    assert tk * k_splits == kd
    return pl.pallas_call(
        _mm_ksplit_kernel,
        grid=(m // tm, n // tn, k_splits),
        in_specs=[pl.BlockSpec((tm, tk), lambda i, j, k: (i, k)), pl.BlockSpec((tk, tn), lambda i, j, k: (k, j))],
        out_specs=pl.BlockSpec((tm, tn), lambda i, j, k: (i, j)),
        out_shape=jax.ShapeDtypeStruct((m, n), F32),
        compiler_params=_params("parallel", "parallel", "arbitrary"),
    )(x, w)


def rope_tables(seq):
    half = ROT_DIM // 2
    pos = jnp.arange(seq, dtype=F32)
    inv = ROPE_THETA ** (-jnp.arange(0, ROT_DIM, 2, dtype=F32) / ROT_DIM)
    ang = pos[:, None] * inv[None, :]
    cos, sin = jnp.cos(ang), jnp.sin(ang)
    zeros = jnp.zeros((seq, HEAD_DIM - ROT_DIM), F32)
    zhalf = jnp.zeros((seq, half), F32)
    c = jnp.concatenate([cos, cos, zeros + 1.0], axis=-1)
    s1 = jnp.concatenate([zhalf, sin, zeros], axis=-1)
    s2 = jnp.concatenate([-sin, zhalf, zeros], axis=-1)
    return c, s1, s2


PLAIN, ROPE, ROPE_SCALED = 0, 1, 2


def _proj_rope_kernel(x_ref, w_ref, c_ref, s1_ref, s2_ref, o_ref, *, tile_mode, w_rows_are_outputs):
    mode = tile_mode(pl.program_id(1))
    half = ROT_DIM // 2
    sub = min(2 * HEAD_DIM, o_ref.shape[1])

    def run(rope, scale):
        x = x_ref[...]
        for c0 in range(0, o_ref.shape[1], sub):
            if w_rows_are_outputs:
                acc = _dot_nt(x, w_ref[c0:c0 + sub, :].astype(BF16))
            else:
                acc = _dot(x, w_ref[:, c0:c0 + sub].astype(BF16))
            if not rope:
                o_ref[:, c0:c0 + sub] = acc.astype(o_ref.dtype)
                continue
            c, s1, s2 = c_ref[...], s1_ref[...], s2_ref[...]
            for h0 in range(0, sub, HEAD_DIM):
                xh = acc[:, h0:h0 + HEAD_DIM]
                r = xh * c + pltpu.roll(xh, half, 1) * s1 + pltpu.roll(xh, HEAD_DIM - half, 1) * s2
                if scale != 1.0:
                    r = r * scale
                o_ref[:, c0 + h0:c0 + h0 + HEAD_DIM] = r.astype(o_ref.dtype)

    @pl.when(mode == ROPE_SCALED)
    def _():
        run(True, Q_SCALE)

    @pl.when(mode == ROPE)
    def _():
        run(True, 1.0)

    @pl.when(mode == PLAIN)
    def _():
        run(False, 1.0)


def proj_rope(x, w, layer, n, tables, seq, tile_mode, tm=1024, tn=512, w_rows_are_outputs=False):
    m, kd = x.shape
    tm, tn = min(tm, seq), min(tn, n)
    assert seq % tm == 0 and m % tm == 0 and n % tn == 0
    pos_blocks = seq // tm
    row_bytes = tm * kd * 2 + 3 * tm * HEAD_DIM * 4
    sub = min(2 * HEAD_DIM, tn)
    stream_bytes = 2 * (kd * tn * 4 + tm * tn * 2) + kd * sub * 2 + tm * sub * 4
    rows_mode = {} if 2 * row_bytes + stream_bytes <= VMEM_LIMIT_BYTES else {"pipeline_mode": pl.Buffered(1)}
    tspec = pl.BlockSpec((tm, HEAD_DIM), lambda i, j: (i % pos_blocks, 0), **rows_mode)
    return pl.pallas_call(
        functools.partial(_proj_rope_kernel, tile_mode=tile_mode, w_rows_are_outputs=w_rows_are_outputs),
        grid=(m // tm, n // tn),
        in_specs=[pl.BlockSpec((tm, kd), lambda i, j: (i, 0), **rows_mode),
                  (pl.BlockSpec((None, tn, kd), lambda i, j: (layer, j, 0)) if w_rows_are_outputs
                   else pl.BlockSpec((None, kd, tn), lambda i, j: (layer, 0, j))),
                  tspec, tspec, tspec],
        out_specs=pl.BlockSpec((tm, tn), lambda i, j: (i, j)),
        out_shape=jax.ShapeDtypeStruct((m, n), BF16),
        compiler_params=_params("parallel", "parallel"),
    )(x, w, *tables)


def _flash_init_many(slots, s_list, v_list, m_ref, l_ref, acc_ref):
    m = [jnp.max(s, axis=0, keepdims=True) for s in s_list]
    p = [jnp.exp2(s - mm) for s, mm in zip(s_list, m)]
    pv = [_dot_tn(v, pp.astype(v.dtype)) for v, pp in zip(v_list, p)]
    for idx, c in enumerate(slots):
        m_ref[c] = m[idx]
        l_ref[c] = jnp.sum(p[idx], axis=0, keepdims=True)
        acc_ref[c] = pv[idx]


def _flash_update_many(slots, s_list, v_list, m_ref, l_ref, acc_ref):
    m_old = [m_ref[c] for c in slots]
    l_old = [l_ref[c] for c in slots]
    m_new = [jnp.maximum(mo, jnp.max(s, axis=0, keepdims=True)) for mo, s in zip(m_old, s_list)]
    p = [jnp.exp2(s - mn) for s, mn in zip(s_list, m_new)]
    pv = [_dot_tn(v, pp.astype(v.dtype)) for v, pp in zip(v_list, p)]
    for idx, c in enumerate(slots):
        alpha = jnp.exp2(m_old[idx] - m_new[idx])
        m_ref[c] = m_new[idx]
        l_ref[c] = alpha * l_old[idx] + jnp.sum(p[idx], axis=0, keepdims=True)
        acc_ref[c] = alpha * acc_ref[c] + pv[idx]


def _flash_result(slot, l_ref, acc_ref):
    return acc_ref[slot] * (1.0 / l_ref[slot])


def _softmax_pv(s_parts, v):
    e = [jnp.exp2(s - jnp.max(s, axis=0, keepdims=True)) for s in s_parts]
    pv = [_dot_tn(v, p.astype(v.dtype)) for p in e]
    return jnp.concatenate([o * (1.0 / jnp.sum(p, axis=0, keepdims=True)) for o, p in zip(pv, e)], axis=1)


def _pad_rows(n_rows):
    return -(-n_rows // SUBLANES) * SUBLANES


def _rank_before(v, n_rows):
    row = lax.broadcasted_iota(jnp.int32, v.shape, 0)
    rank = jnp.zeros(v.shape, jnp.int32)
    for j2 in range(n_rows):
        other = v[j2:j2 + 1, :]
        rank = rank + jnp.where(other > v, 1, jnp.where(other == v, (row > j2).astype(jnp.int32), 0))
    return rank


def _diff_kernel(lam_ref, g_ref, q_ref, k_ref, v_ref, o_ref, m_ref, l_ref, acc_ref, *, tq, hp, lam_init):
    i = pl.program_id(2)
    w = DIFF_V_DIM
    lp = lam_ref[...]
    lam = (jnp.exp(jnp.sum(lp[0:1] * lp[1:2], axis=-1, keepdims=True))
           - jnp.exp(jnp.sum(lp[2:3] * lp[3:4], axis=-1, keepdims=True)) + lam_init)
    key = lax.broadcasted_iota(jnp.int32, (tq, tq), 0)
    qry = lax.broadcasted_iota(jnp.int32, (tq, tq), 1)

    def scores(c, start):
        cols = slice(c * HEAD_DIM, (c + 1) * HEAD_DIM)
        return _dot_nt(k_ref[0, pl.ds(start, tq), cols], q_ref[0, :, cols])

    own = pl.multiple_of(i * tq, tq)
    s_own = [jnp.where(key <= qry, scores(c, own), NEG) for c in range(2 * hp)]
    v_own = [v_ref[0, pl.ds(own, tq), (c // 2) * w:(c // 2 + 1) * w] for c in range(2 * hp)]
    _flash_init_many(range(2 * hp), s_own, v_own, m_ref, l_ref, acc_ref)

    def body(j, carry):
        start = pl.multiple_of(j * tq, tq)
        s = [scores(c, start) for c in range(2 * hp)]
        v = [v_ref[0, pl.ds(start, tq), (c // 2) * w:(c // 2 + 1) * w] for c in range(2 * hp)]
        _flash_update_many(range(2 * hp), s, v, m_ref, l_ref, acc_ref)
        return carry

    lax.fori_loop(0, i, body, 0)
    for h in range(hp):
        o = (_flash_result(2 * h, l_ref, acc_ref) - lam * _flash_result(2 * h + 1, l_ref, acc_ref)).T
        o_ref[0, :, h * w:(h + 1) * w] = (_rms(o, g_ref[...]) * (1.0 - lam_init)).astype(o_ref.dtype)


def diff_attention(proj, lam_params, subln_g, lam_init, n_heads, q_col, k_col, v_col, tq=256, hp=4):
    b, s, _ = proj.shape
    tq = min(tq, s)
    w = DIFF_V_DIM
    hp = min(hp, n_heads)
    assert n_heads % hp == 0
    gw = hp * w
    qo, ko, vo = q_col // gw, k_col // gw, v_col // gw
    return pl.pallas_call(
        functools.partial(_diff_kernel, tq=tq, hp=hp, lam_init=lam_init),
        grid=(b, n_heads // hp, s // tq),
        in_specs=[pl.BlockSpec((4, HEAD_DIM), lambda bi, h, i: (0, 0)),
                  pl.BlockSpec((1, w), lambda bi, h, i: (0, 0)),
                  pl.BlockSpec((1, tq, gw), lambda bi, h, i: (bi, i, qo + h)),
                  pl.BlockSpec((1, s, gw), lambda bi, h, i: (bi, 0, ko + h)),
                  pl.BlockSpec((1, s, gw), lambda bi, h, i: (bi, 0, vo + h))],
        out_specs=pl.BlockSpec((1, tq, gw), lambda bi, h, i: (bi, i, h)),
        out_shape=jax.ShapeDtypeStruct((b, s, n_heads * w), BF16),
        scratch_shapes=[pltpu.VMEM((2 * hp, 1, tq), F32), pltpu.VMEM((2 * hp, 1, tq), F32),
                        pltpu.VMEM((2 * hp, w, tq), F32)],
        compiler_params=_params("parallel", "parallel", "arbitrary"),
    )(lam_params, subln_g.reshape(1, w), proj, proj, proj)


def _moba_kernel(q_ref, k_ref, v_ref, o_ref, kmean_ref, bias_ref, m_ref, l_ref, acc_ref, *, nb, hp):
    i = pl.program_id(2)
    blk = MOBA_BLOCK
    head = lambda h: slice(h * HEAD_DIM, (h + 1) * HEAD_DIM)

    @pl.when(i == 0)
    def _():
        kmean_ref[...] = jnp.zeros(kmean_ref.shape, F32)
        for h in range(hp):
            for n in range(nb):
                kb = k_ref[0, n * blk:(n + 1) * blk, head(h)].astype(F32)
                kmean_ref[h, n:n + 1, :] = jnp.mean(kb, axis=0, keepdims=True)

    key = lax.broadcasted_iota(jnp.int32, (blk, blk), 0)
    qry = lax.broadcasted_iota(jnp.int32, (blk, blk), 1)
    own = pl.multiple_of(i * blk, blk)
    qs = [q_ref[0, :, head(h)] for h in range(hp)]
    s_own = [jnp.where(key <= qry, _dot_nt(k_ref[0, pl.ds(own, blk), head(h)], qs[h]), NEG) for h in range(hp)]
    v_own = [v_ref[0, pl.ds(own, blk), head(h)] for h in range(hp)]
    biases = []
    for h in range(hp):
        km_hi, km_lo = _split_bf16(kmean_ref[h])
        gate = (_dot_nt(km_hi, qs[h]) + _dot_nt(km_lo, qs[h]))[:_pad_rows(nb)]
        past = lax.broadcasted_iota(jnp.int32, gate.shape, 0) < i
        rank = _rank_before(jnp.where(past, gate, NEG), nb)
        biases.append(jnp.where(past & (rank < MOBA_TOPK), 0.0, NEG))
    for h in range(hp):
        bias_ref[h] = biases[h]
    _flash_init_many(range(hp), s_own, v_own, m_ref, l_ref, acc_ref)

    def body(n, carry):
        start = pl.multiple_of(n * blk, blk)
        sn = [_dot_nt(k_ref[0, pl.ds(start, blk), head(h)], q_ref[0, :, head(h)]) + bias_ref[h, pl.ds(n, 1), :]
              for h in range(hp)]
        vn = [v_ref[0, pl.ds(start, blk), head(h)] for h in range(hp)]
        _flash_update_many(range(hp), sn, vn, m_ref, l_ref, acc_ref)
        return carry

    lax.fori_loop(0, i, body, 0)
    for h in range(hp):
        o_ref[0, :, head(h)] = _flash_result(h, l_ref, acc_ref).T.astype(o_ref.dtype)


def moba_attention(proj, n_heads, q_col, k_col, v_col, hp=8):
    b, s, _ = proj.shape
    blk = MOBA_BLOCK
    hp = min(hp, n_heads)
    assert s % blk == 0 and s // blk <= LANES and n_heads % hp == 0
    nb = s // blk
    gw = hp * HEAD_DIM
    qo, ko, vo = q_col // gw, k_col // gw, v_col // gw
    return pl.pallas_call(
        functools.partial(_moba_kernel, nb=nb, hp=hp),
        grid=(b, n_heads // hp, nb),
        in_specs=[pl.BlockSpec((1, blk, gw), lambda bi, h, i: (bi, i, qo + h)),
                  pl.BlockSpec((1, s, gw), lambda bi, h, i: (bi, 0, ko + h)),
                  pl.BlockSpec((1, s, gw), lambda bi, h, i: (bi, 0, vo + h))],
        out_specs=pl.BlockSpec((1, blk, gw), lambda bi, h, i: (bi, i, h)),
        out_shape=jax.ShapeDtypeStruct((b, s, n_heads * HEAD_DIM), BF16),
        scratch_shapes=[pltpu.VMEM((hp, LANES, HEAD_DIM), F32), pltpu.VMEM((hp, _pad_rows(nb), blk), F32),
                        pltpu.VMEM((hp, 1, blk), F32), pltpu.VMEM((hp, 1, blk), F32),
                        pltpu.VMEM((hp, HEAD_DIM, blk), F32)],
        compiler_params=_params("parallel", "parallel", "arbitrary"),
    )(proj, proj, proj)


def _compress_kernel(t_ref, pe_ref, w1_ref, w2_ref, o_ref):
    t = t_ref[0, 0].astype(F32)
    half = t.shape[1]
    top = (t + pe_ref[0, :, :half]).astype(BF16)
    bot = (t + pe_ref[0, :, half:]).astype(BF16)
    a = _dot(top, w1_ref[0, :half, :].astype(BF16))
    bm = _dot(bot, w1_ref[0, half:, :].astype(BF16))
    hidden = a + pltpu.roll(bm, bm.shape[0] - 1, 0)
    act = hidden * jax.nn.sigmoid(hidden)
    o_ref[0, 0] = _dot(act.astype(BF16), w2_ref[0].astype(BF16)).astype(o_ref.dtype)


def nsa_compress(t16, pe_flat, w1, w2):
    b, g2, nc, width = t16.shape
    g = g2 // 2
    hid = w1.shape[-1]
    return pl.pallas_call(
        _compress_kernel,
        grid=(b, g2),
        in_specs=[pl.BlockSpec((1, 1, nc, width), lambda bi, j: (bi, j, 0, 0)),
                  pl.BlockSpec((1, 1, 2 * width), lambda bi, j: (j // g, 0, 0)),
                  pl.BlockSpec((1, 2 * width, hid), lambda bi, j: (j // g, 0, 0)),
                  pl.BlockSpec((1, hid, HEAD_DIM), lambda bi, j: (j // g, 0, 0))],
        out_specs=pl.BlockSpec((1, 1, nc, HEAD_DIM), lambda bi, j: (bi, j, 0, 0)),
        out_shape=jax.ShapeDtypeStruct((b, g2, nc, HEAD_DIM), BF16),
        compiler_params=_params("parallel", "parallel"),
    )(t16, pe_flat, w1, w2)


def _lanes_x(a, n):
    return jnp.concatenate([a] * n, axis=1)


def _nsa_kernel(q_ref, ks_ref, vs_ref, kw_ref, vw_ref, kc_ref, vc_ref, gt_ref, cover_ref, expand_ref,
                o_ref, part_ref, m_ref, l_ref, acc_ref, *, tq, tk, n_sel):
    i = pl.program_id(2)
    z_heads = NSA_GROUP
    nc = kc_ref.shape[2]
    q = jnp.concatenate([q_ref[0, :, z * HEAD_DIM:(z + 1) * HEAD_DIM] for z in range(z_heads)], axis=0)
    qpos = i * tq + lax.broadcasted_iota(jnp.int32, (1, tq), 1)

    cmp_end = lax.broadcasted_iota(jnp.int32, (nc, tq), 0) * NSA_CMP_STRIDE + (NSA_CMP_LEN - 1)
    cmp_ok = _lanes_x(jnp.where(cmp_end <= qpos, 1.0, 0.0), z_heads)
    sc = jnp.where(cmp_ok > 0.5, _dot_nt(kc_ref[0, 0], q), NEG)
    pc = jnp.exp2(sc - jnp.max(sc, axis=0, keepdims=True))
    pc = pc * (1.0 / jnp.sum(pc, axis=0, keepdims=True)) * cmp_ok
    o_cmp = _dot_tn(vc_ref[0, 0], pc.astype(BF16))

    p_sum = pc[:, 0:tq]
    for z in range(1, z_heads):
        p_sum = p_sum + pc[:, z * tq:(z + 1) * tq]
    p_hi, p_lo = _split_bf16(p_sum)
    imp = (_dot(cover_ref[...], p_hi) + _dot(cover_ref[...], p_lo))[:_pad_rows(n_sel)]
    row = lax.broadcasted_iota(jnp.int32, imp.shape, 0)
    qblk = qpos // NSA_SEL_BLOCK
    forced = (row == 0) | (row == qblk) | (row == qblk - 1)
    allowed = row <= qblk
    imp = jnp.where(forced, BIG, jnp.where(allowed, imp, NEG))
    rank = _rank_before(imp, n_sel)
    block_bias = jnp.where(allowed & (rank < NSA_SEL_TOPN), 0.0, NEG)
    block_bias = jnp.concatenate(
        [block_bias, jnp.zeros((LANES - block_bias.shape[0], tq), F32)], axis=0).T.astype(BF16)
    q2 = jnp.concatenate([q, jnp.concatenate([block_bias] * z_heads, axis=0)], axis=1)
    span = NSA_WINDOW + tq
    w_start = pl.multiple_of(jnp.maximum(i * tq - NSA_WINDOW, 0), tq)
    dist = qpos - (w_start + lax.broadcasted_iota(jnp.int32, (span, tq), 0))
    w_bias = jnp.where((dist >= 0) & (dist < NSA_WINDOW), 0.0, NEG)
    kw = kw_ref[0, pl.ds(w_start, span), :]
    half_rows = z_heads * tq // 2
    sw = [_dot_nt(kw, q[p * half_rows:(p + 1) * half_rows]) + _lanes_x(w_bias, z_heads // 2) for p in range(2)]
    o_win = _softmax_pv(sw, vw_ref[0, pl.ds(w_start, span), :])
    gt = jax.nn.sigmoid(gt_ref[...])
    part_ref[...] = jnp.concatenate(
        [gt[3 * z:3 * z + 1] * o_cmp[:, z * tq:(z + 1) * tq] + gt[3 * z + 2:3 * z + 3] * o_win[:, z * tq:(z + 1) * tq]
         for z in range(z_heads)], axis=1)

    n_split = NSA_LANE_SPLIT
    part_w = z_heads * tq // n_split
    halves = tuple(slice(p * part_w, (p + 1) * part_w) for p in range(n_split))

    def sel_pair(t, causal, tiles=(0, 1)):
        s, v = [], []
        for c in tiles:
            start = (2 * t + c) * tk
            if not isinstance(start, int):
                start = pl.multiple_of(start, tk)
            k2 = jnp.concatenate([ks_ref[0, pl.ds(start, tk), :], expand_ref[pl.ds(start, tk), :]], axis=1)
            sc = _dot_nt(k2, q2)
            if causal:
                kpos = start + lax.broadcasted_iota(jnp.int32, (tk, tq), 0)
                sc = sc + _lanes_x(jnp.where(kpos <= qpos, 0.0, NEG), z_heads)
            vt = vs_ref[0, pl.ds(start, tk), :]
            for lanes in halves:
                s.append(sc[:, lanes])
                v.append(vt)
        return s, v

    n_pairs = ((i + 1) * tq + 2 * tk - 1) // (2 * tk)
    s0, v0 = sel_pair(0, True)
    _flash_init_many(range(2 * n_split), s0, v0, m_ref, l_ref, acc_ref)

    def sel_body(t, carry):
        st, vt = sel_pair(t, False)
        _flash_update_many(range(2 * n_split), st, vt, m_ref, l_ref, acc_ref)
        return carry

    lax.fori_loop(1, n_pairs - 1, sel_body, 0)

    odd_needed = (2 * n_pairs - 1) * tk < (i + 1) * tq

    @pl.when((n_pairs > 1) & odd_needed)
    def _():
        st, vt = sel_pair(n_pairs - 1, True)
        _flash_update_many(range(2 * n_split), st, vt, m_ref, l_ref, acc_ref)

    @pl.when((n_pairs > 1) & jnp.logical_not(odd_needed))
    def _():
        st, vt = sel_pair(n_pairs - 1, True, tiles=(0,))
        _flash_update_many(range(n_split), st, vt, m_ref, l_ref, acc_ref)

    merged = []
    for a in range(n_split):
        b = n_split + a
        m_all = jnp.maximum(m_ref[a], m_ref[b])
        w0, w1 = jnp.exp2(m_ref[a] - m_all), jnp.exp2(m_ref[b] - m_all)
        merged.append((w0 * acc_ref[a] + w1 * acc_ref[b]) * (1.0 / (w0 * l_ref[a] + w1 * l_ref[b])))
    o_sel = jnp.concatenate(merged, axis=1)

    for z in range(z_heads):
        ls = slice(z * tq, (z + 1) * tq)
        o = part_ref[:, ls] + jax.nn.sigmoid(gt_ref[3 * z + 1:3 * z + 2, :]) * o_sel[:, ls]
        o_ref[0, :, z * HEAD_DIM:(z + 1) * HEAD_DIM] = o.T.astype(o_ref.dtype)


def nsa_attention(proj, gates_t, cmp_kv, n_groups, q_col, ks_col, vs_col, kw_col, vw_col, tq=128, tk=256):
    b, s, _ = proj.shape
    tq, tk = min(tq, s), min(tk, s)
    assert s % (2 * tk) == 0 and tk % tq == 0 and NSA_WINDOW % tq == 0 and tq % NSA_SEL_BLOCK == 0
    assert s >= NSA_WINDOW + tq
    nc = cmp_kv.shape[2]
    n_sel = s // NSA_SEL_BLOCK
    assert nc * NSA_CMP_STRIDE == s and n_sel <= LANES
    c_lo = np.arange(nc) * NSA_CMP_STRIDE
    j_lo = np.arange(LANES) * NSA_SEL_BLOCK
    cover_t = ((c_lo[None, :] < j_lo[:, None] + NSA_SEL_BLOCK) & (c_lo[None, :] + NSA_CMP_LEN > j_lo[:, None])
               & (np.arange(LANES)[:, None] < n_sel) & (c_lo[None, :] + NSA_CMP_LEN <= s))
    expand_t = (np.arange(s)[:, None] // NSA_SEL_BLOCK == np.arange(LANES)[None, :])
    gw = NSA_GROUP * HEAD_DIM
    qo = q_col // gw
    kso, vso, kwo, vwo = (c // HEAD_DIM for c in (ks_col, vs_col, kw_col, vw_col))
    kv = lambda off: pl.BlockSpec((1, s, HEAD_DIM), lambda bi, g, i: (bi, 0, off + g))
    q_tiles = s // tq
    rows = NSA_GROUP * tq
    return pl.pallas_call(
        functools.partial(_nsa_kernel, tq=tq, tk=tk, n_sel=n_sel),
        grid=(b, n_groups, q_tiles),
        in_specs=[pl.BlockSpec((1, tq, gw), lambda bi, g, i: (bi, i, qo + g)),
                  kv(kso), kv(vso), kv(kwo), kv(vwo),
                  pl.BlockSpec((1, 1, nc, HEAD_DIM), lambda bi, g, i: (bi, g, 0, 0)),
                  pl.BlockSpec((1, 1, nc, HEAD_DIM), lambda bi, g, i: (bi, n_groups + g, 0, 0)),
                  pl.BlockSpec((NSA_GATE_PAD, tq), lambda bi, g, i: (g, bi * q_tiles + i)),
                  pl.BlockSpec((LANES, nc), lambda bi, g, i: (0, 0)),
                  pl.BlockSpec((s, LANES), lambda bi, g, i: (0, 0))],
        out_specs=pl.BlockSpec((1, tq, gw), lambda bi, g, i: (bi, i, g)),
        out_shape=jax.ShapeDtypeStruct((b, s, n_groups * gw), BF16),
        scratch_shapes=[pltpu.VMEM((HEAD_DIM, rows), F32),
                        pltpu.VMEM((2 * NSA_LANE_SPLIT, 1, rows // NSA_LANE_SPLIT), F32),
                        pltpu.VMEM((2 * NSA_LANE_SPLIT, 1, rows // NSA_LANE_SPLIT), F32),
                        pltpu.VMEM((2 * NSA_LANE_SPLIT, HEAD_DIM, rows // NSA_LANE_SPLIT), F32)],
        compiler_params=_params("parallel", "parallel", "arbitrary"),
    )(proj, proj, proj, proj, proj, cmp_kv, cmp_kv, gates_t, jnp.asarray(cover_t, BF16),
      jnp.asarray(expand_t, BF16))


def _even_tile_mode(tn, d_model):
    seg_tiles = (d_model // 2) // tn
    return lambda j: ROPE_SCALED - (j // seg_tiles) % 3


def even_mixer(hn, w_in, w_out, e, lam_params, subln_g, lam_init, tables, batch, seq):
    d_model = hn.shape[1]
    seg = d_model // 2
    tn = min(1024, seg)
    proj = proj_rope(hn, w_in, e, 6 * seg, tables, seq, _even_tile_mode(tn, d_model), tn=tn)
    proj = proj.reshape(batch, seq, -1)
    o_diff = diff_attention(proj, lam_params, subln_g, lam_init, seg // DIFF_V_DIM, 0, seg, 2 * seg)
    o_moba = moba_attention(proj, seg // HEAD_DIM, 3 * seg, 4 * seg, 5 * seg)
    rows = batch * seq
    return matmul_cat(o_diff.reshape(rows, seg), o_moba.reshape(rows, seg), w_out, e, F32)


def _odd_tile_mode(tn, d_model):
    slab_tiles = (d_model // 8) // tn
    q_tiles = d_model // tn
    return lambda j: jnp.where(j < q_tiles, ROPE_SCALED, ROPE - ((j - q_tiles) // slab_tiles) % 2)


def odd_mixer(hn, w_in, w_out, o_idx, cmp_pe, cmp_w1, cmp_w2, tables, batch, seq):
    d_model = hn.shape[1]
    g = d_model // (NSA_GROUP * HEAD_DIM)
    kvd = g * HEAD_DIM
    n_main = d_model + 6 * kvd
    tn = min(512, kvd)
    w_in_t = jnp.swapaxes(w_in, 1, 2)
    proj = proj_rope(hn, w_in_t, o_idx, n_main, tables, seq, _odd_tile_mode(tn, d_model), tn=tn,
                     w_rows_are_outputs=True)
    proj = proj.reshape(batch, seq, n_main)
    n_gate = 3 * NSA_GROUP
    w_gate_t = w_in_t[o_idx, n_main:, :].reshape(g, n_gate, d_model)
    w_gate_t = jnp.pad(w_gate_t, ((0, 0), (0, NSA_GATE_PAD - n_gate), (0, 0))).reshape(g * NSA_GATE_PAD, d_model)
    gates_t = matmul_nt(w_gate_t, hn)
    nc = seq // NSA_CMP_STRIDE
    t16 = proj[:, :, d_model:d_model + 2 * kvd].reshape(batch, seq, 2 * g, HEAD_DIM)
    t16 = t16.transpose(0, 2, 1, 3).reshape(batch, 2 * g, nc, NSA_CMP_STRIDE * HEAD_DIM)
    cmp_kv = nsa_compress(t16, cmp_pe.reshape(2, 1, NSA_CMP_LEN * HEAD_DIM), cmp_w1, cmp_w2)
    o = nsa_attention(proj, gates_t, cmp_kv, g, 0, d_model + 2 * kvd, d_model + 3 * kvd, d_model + 4 * kvd,
                      d_model + 5 * kvd)
    return matmul(o.reshape(batch * seq, d_model), w_out, o_idx, F32)


def ffn(hn, w_gate, w_up, w_down, layer):
    a, w_down_bf16 = swiglu_up(hn, w_gate, w_up, w_down, layer)
    return matmul_ksplit(a, w_down_bf16, 2)


def kernel(x, norm_g, ffn_a_gate, ffn_a_up, ffn_a_down, ffn_b_gate, ffn_b_up, ffn_b_down, ev_w_in, ev_w_out, ev_lambda, ev_subln_g, od_w_in, od_w_out, od_cmp_pe, od_cmp_w1, od_cmp_w2):
    batch, seq, d_model = x.shape
    depth = norm_g.shape[0]
    tables = rope_tables(seq)
    h = x.reshape(batch * seq, d_model)
    hn = rmsnorm(h, norm_g[0, 0])
    for layer in range(depth):
        g = norm_g[layer]
        m = ffn(hn, ffn_a_gate, ffn_a_up, ffn_a_down, layer)
        h, hn = residual_norm(h, m, g[1], g[2], 0.5)
        if layer % 2 == 0:
            e = layer // 2
            lam_init = 0.8 - 0.6 * math.exp(-0.3 * layer)
            m = even_mixer(hn, ev_w_in, ev_w_out, e, ev_lambda[e], ev_subln_g[e], lam_init, tables, batch, seq)
        else:
            o = layer // 2
            m = odd_mixer(hn, od_w_in, od_w_out, o, od_cmp_pe[o], od_cmp_w1[o], od_cmp_w2[o], tables, batch, seq)
        h, hn = residual_norm(h, m, g[3], g[4], 1.0)
        m = ffn(hn, ffn_b_gate, ffn_b_up, ffn_b_down, layer)
        g_next = norm_g[layer + 1, 0] if layer + 1 < depth else None
        h, hn = residual_norm(h, m, g[5], g_next, 0.5)
    return h.reshape(batch, seq, d_model)
```

```python
import functools
import math

import numpy as np
import jax
import jax.numpy as jnp
from jax import lax
from jax.experimental import pallas as pl
from jax.experimental.pallas import tpu as pltpu

F32 = jnp.float32
BF16 = jnp.bfloat16

HEAD_DIM = 128
ROT_DIM = HEAD_DIM // 4
ROPE_THETA = 500000.0
RMS_EPS = 1e-6
NEG = -1e30
BIG = 1e30
Q_SCALE = HEAD_DIM ** -0.5 * math.log2(math.e)

DIFF_V_DIM = 2 * HEAD_DIM
MOBA_BLOCK = 256
MOBA_TOPK = 3
NSA_GROUP = 8
NSA_CMP_LEN = 32
NSA_CMP_STRIDE = 16
NSA_SEL_BLOCK = 64
NSA_SEL_TOPN = 16
NSA_WINDOW = 512
NSA_GATE_PAD = 128
NSA_LANE_SPLIT = 2

LANES = 128
SUBLANES = 8
VMEM_LIMIT_BYTES = 56 * 1024 * 1024


def _params(*semantics):
    return pltpu.CompilerParams(dimension_semantics=semantics, vmem_limit_bytes=VMEM_LIMIT_BYTES)


def _dot(a, b):
    return jnp.dot(a, b, preferred_element_type=F32)


def _dot_nt(a, b):
    return lax.dot_general(a, b, (((1,), (1,)), ((), ())), preferred_element_type=F32)


def _dot_tn(a, b):
    return lax.dot_general(a, b, (((0,), (0,)), ((), ())), preferred_element_type=F32)


def _split_bf16(x):
    hi = x.astype(BF16)
    lo = (x - hi.astype(F32)).astype(BF16)
    return hi, lo


def _rms(x, g):
    return x * lax.rsqrt(jnp.mean(x * x, axis=-1, keepdims=True) + RMS_EPS) * g


def _rmsnorm_kernel(x_ref, g_ref, o_ref):
    o_ref[...] = _rms(x_ref[...], g_ref[...]).astype(o_ref.dtype)


def rmsnorm(x, g, tm=256):
    m, d = x.shape
    tm = min(tm, m)
    return pl.pallas_call(
        _rmsnorm_kernel,
        grid=(m // tm,),
        in_specs=[pl.BlockSpec((tm, d), lambda i: (i, 0)), pl.BlockSpec((1, d), lambda i: (0, 0))],
        out_specs=pl.BlockSpec((tm, d), lambda i: (i, 0)),
        out_shape=jax.ShapeDtypeStruct((m, d), BF16),
        compiler_params=_params("parallel"),
    )(x, g.reshape(1, d))


def _residual_kernel(h_ref, m_ref, gpost_ref, gnext_ref, h_out_ref, hn_out_ref, *, coef):
    y = _rms(m_ref[...], gpost_ref[...])
    if coef != 1.0:
        y = coef * y
    h = h_ref[...] + y
    h_out_ref[...] = h
    hn_out_ref[...] = _rms(h, gnext_ref[...]).astype(hn_out_ref.dtype)


def _residual_last_kernel(h_ref, m_ref, gpost_ref, h_out_ref, *, coef):
    y = _rms(m_ref[...], gpost_ref[...])
    if coef != 1.0:
        y = coef * y
    h_out_ref[...] = h_ref[...] + y


def residual_norm(h, mix, g_post, g_next, coef, tm=256):
    m, d = h.shape
    tm = min(tm, m)
    row = pl.BlockSpec((tm, d), lambda i: (i, 0))
    vec = pl.BlockSpec((1, d), lambda i: (0, 0))
    if g_next is None:
        return pl.pallas_call(
            functools.partial(_residual_last_kernel, coef=coef),
            grid=(m // tm,),
            in_specs=[row, row, vec],
            out_specs=row,
            out_shape=jax.ShapeDtypeStruct((m, d), F32),
            compiler_params=_params("parallel"),
        )(h, mix, g_post.reshape(1, d)), None
    return pl.pallas_call(
        functools.partial(_residual_kernel, coef=coef),
        grid=(m // tm,),
        in_specs=[row, row, vec, vec],
        out_specs=[row, row],
        out_shape=[jax.ShapeDtypeStruct((m, d), F32), jax.ShapeDtypeStruct((m, d), BF16)],
        compiler_params=_params("parallel"),
    )(h, mix, g_post.reshape(1, d), g_next.reshape(1, d))


def _mm_kernel(x_ref, w_ref, o_ref):
    o_ref[...] = _dot(x_ref[...], w_ref[...].astype(BF16)).astype(o_ref.dtype)


def matmul(x, w, layer, out_dtype, tm=1024, tn=512):
    m, kd = x.shape
    n = w.shape[2]
    tm, tn = min(tm, m), min(tn, n)
    return pl.pallas_call(
        _mm_kernel,
        grid=(m // tm, n // tn),
        in_specs=[pl.BlockSpec((tm, kd), lambda i, j: (i, 0)),
                  pl.BlockSpec((None, kd, tn), lambda i, j: (layer, 0, j))],
        out_specs=pl.BlockSpec((tm, tn), lambda i, j: (i, j)),
        out_shape=jax.ShapeDtypeStruct((m, n), out_dtype),
        compiler_params=_params("parallel", "parallel"),
    )(x, w)


def _mm_cat_kernel(x1_ref, x2_ref, w_ref, o_ref):
    k1 = x1_ref.shape[1]
    o_ref[...] = (_dot(x1_ref[...], w_ref[:k1, :].astype(BF16))
                  + _dot(x2_ref[...], w_ref[k1:, :].astype(BF16))).astype(o_ref.dtype)


def matmul_cat(x1, x2, w, layer, out_dtype, tm=1024, tn=512):
    m, k1 = x1.shape
    k2 = x2.shape[1]
    n = w.shape[2]
    tm, tn = min(tm, m), min(tn, n)
    return pl.pallas_call(
        _mm_cat_kernel,
        grid=(m // tm, n // tn),
        in_specs=[pl.BlockSpec((tm, k1), lambda i, j: (i, 0)), pl.BlockSpec((tm, k2), lambda i, j: (i, 0)),
                  pl.BlockSpec((None, k1 + k2, tn), lambda i, j: (layer, 0, j))],
        out_specs=pl.BlockSpec((tm, tn), lambda i, j: (i, j)),
        out_shape=jax.ShapeDtypeStruct((m, n), out_dtype),
        compiler_params=_params("parallel", "parallel"),
    )(x1, x2, w)


def _mm_nt_kernel(w_ref, x_ref, o_ref):
    o_ref[...] = _dot_nt(w_ref[...].astype(BF16), x_ref[...])


def matmul_nt(w_t, x, tm=1024):
    n, kd = w_t.shape
    m = x.shape[0]
    tm = min(tm, m)
    return pl.pallas_call(
        _mm_nt_kernel,
        grid=(m // tm,),
        in_specs=[pl.BlockSpec((n, kd), lambda i: (0, 0)), pl.BlockSpec((tm, kd), lambda i: (i, 0))],
        out_specs=pl.BlockSpec((n, tm), lambda i: (0, i)),
        out_shape=jax.ShapeDtypeStruct((n, m), F32),
        compiler_params=_params("parallel"),
    )(w_t, x)


def _swiglu_kernel(x_ref, wg_ref, wu_ref, wd_ref, *rest, n_next):
    next_refs, (o_ref, wd_bf_ref), next_bf_refs = rest[:n_next], rest[n_next:n_next + 2], rest[n_next + 2:]
    x = x_ref[...]
    g = _dot(x, wg_ref[...].astype(BF16))
    u = _dot(x, wu_ref[...].astype(BF16))
    o_ref[...] = (g * jax.nn.sigmoid(g) * u).astype(o_ref.dtype)
    wd_bf_ref[...] = wd_ref[...].astype(BF16)
    for src, dst in zip(next_refs, next_bf_refs):
        dst[...] = src[...].astype(BF16)


def swiglu_up(x, w_gate, w_up, w_down, layer, next_weights=None, tm=1024):
    m, kd = x.shape
    pre_rounded = w_gate.ndim == 2
    n = w_gate.shape[-1]
    tm = min(tm, m)
    tn = min(512 if pre_rounded else 256, n)
    n_j = n // tn
    steps = (m // tm) * n_j
    step = lambda i, j: i * n_j + j
    if pre_rounded:
        wspec = pl.BlockSpec((kd, tn), lambda i, j: (0, j))
    else:
        wspec = pl.BlockSpec((None, kd, tn), lambda i, j: (layer, 0, j))
    d_rows, d_cols = w_down.shape[1:]
    slab = d_rows // steps
    assert slab * steps == d_rows and slab % 16 == 0
    in_specs = [pl.BlockSpec((tm, kd), lambda i, j: (i, 0)), wspec, wspec,
                pl.BlockSpec((None, slab, d_cols), lambda i, j: (layer, step(i, j), 0))]
    out_specs = [pl.BlockSpec((tm, tn), lambda i, j: (i, j)),
                 pl.BlockSpec((slab, d_cols), lambda i, j: (step(i, j), 0))]
    out_shape = [jax.ShapeDtypeStruct((m, n), BF16), jax.ShapeDtypeStruct((d_rows, d_cols), BF16)]
    args = [x, w_gate, w_up, w_down]
    n_next = 0
    if next_weights is not None:
        n_gate, n_up, n_layer = next_weights
        n_blocks = 2 * (n // LANES)
        rep = steps // n_blocks
        assert rep * n_blocks == steps and kd % 2 == 0
        blk = lambda i, j: ((step(i, j) // rep) % 2, (step(i, j) // rep) // 2)
        for w_next in (n_gate, n_up):
            in_specs.append(pl.BlockSpec((None, kd // 2, LANES), lambda i, j: (n_layer, *blk(i, j))))
            out_specs.append(pl.BlockSpec((kd // 2, LANES), blk))
            out_shape.append(jax.ShapeDtypeStruct((kd, n), BF16))
            args.append(w_next)
        n_next = 2
    outs = pl.pallas_call(
        functools.partial(_swiglu_kernel, n_next=n_next),
        grid=(m // tm, n_j),
        in_specs=in_specs, out_specs=out_specs, out_shape=out_shape,
        compiler_params=_params("arbitrary", "arbitrary"),
    )(*args)
    return outs[0], outs[1], (tuple(outs[2:]) if n_next else None)


def _mm_ksplit_kernel(x_ref, w_ref, o_ref):
    k = pl.program_id(2)

    @pl.when(k == 0)
    def _():
        o_ref[...] = _dot(x_ref[...], w_ref[...])

    @pl.when(k > 0)
    def _():
        o_ref[...] += _dot(x_ref[...], w_ref[...])


def matmul_ksplit(x, w, k_splits, tm=1024, tn=1024):
    m, kd = x.shape
    n = w.shape[1]
    tm, tn, tk = min(tm, m), min(tn, n), kd // k_splits
    assert tk * k_splits == kd
    return pl.pallas_call(
        _mm_ksplit_kernel,
        grid=(m // tm, n // tn, k_splits),
        in_specs=[pl.BlockSpec((tm, tk), lambda i, j, k: (i, k)), pl.BlockSpec((tk, tn), lambda i, j, k: (k, j))],
        out_specs=pl.BlockSpec((tm, tn), lambda i, j, k: (i, j)),
        out_shape=jax.ShapeDtypeStruct((m, n), F32),
        compiler_params=_params("parallel", "parallel", "arbitrary"),
    )(x, w)


def rope_tables(seq):
    half = ROT_DIM // 2
    pos = jnp.arange(seq, dtype=F32)
    inv = ROPE_THETA ** (-jnp.arange(0, ROT_DIM, 2, dtype=F32) / ROT_DIM)
    ang = pos[:, None] * inv[None, :]
    cos, sin = jnp.cos(ang), jnp.sin(ang)
    zeros = jnp.zeros((seq, HEAD_DIM - ROT_DIM), F32)
    zhalf = jnp.zeros((seq, half), F32)
    c = jnp.concatenate([cos, cos, zeros + 1.0], axis=-1)
    s1 = jnp.concatenate([zhalf, sin, zeros], axis=-1)
    s2 = jnp.concatenate([-sin, zhalf, zeros], axis=-1)
    return c, s1, s2


PLAIN, ROPE, ROPE_SCALED = 0, 1, 2


def _proj_rope_kernel(x_ref, w_ref, c_ref, s1_ref, s2_ref, o_ref, *, tile_mode, w_rows_are_outputs):
    mode = tile_mode(pl.program_id(1))
    half = ROT_DIM // 2
    sub = min(2 * HEAD_DIM, o_ref.shape[1])

    def run(rope, scale):
        x = x_ref[...]
        for c0 in range(0, o_ref.shape[1], sub):
            if w_rows_are_outputs:
                acc = _dot_nt(x, w_ref[c0:c0 + sub, :].astype(BF16))
            else:
                acc = _dot(x, w_ref[:, c0:c0 + sub].astype(BF16))
            if not rope:
                o_ref[:, c0:c0 + sub] = acc.astype(o_ref.dtype)
                continue
            c, s1, s2 = c_ref[...], s1_ref[...], s2_ref[...]
            for h0 in range(0, sub, HEAD_DIM):
                xh = acc[:, h0:h0 + HEAD_DIM]
                r = xh * c + pltpu.roll(xh, half, 1) * s1 + pltpu.roll(xh, HEAD_DIM - half, 1) * s2
                if scale != 1.0:
                    r = r * scale
                o_ref[:, c0 + h0:c0 + h0 + HEAD_DIM] = r.astype(o_ref.dtype)

    @pl.when(mode == ROPE_SCALED)
    def _():
        run(True, Q_SCALE)

    @pl.when(mode == ROPE)
    def _():
        run(True, 1.0)

    @pl.when(mode == PLAIN)
    def _():
        run(False, 1.0)


def proj_rope(x, w, layer, n, tables, seq, tile_mode, tm=1024, tn=512, w_rows_are_outputs=False):
    m, kd = x.shape
    tm, tn = min(tm, seq), min(tn, n)
    assert seq % tm == 0 and m % tm == 0 and n % tn == 0
    pos_blocks = seq // tm
    row_bytes = tm * kd * 2 + 3 * tm * HEAD_DIM * 4
    sub = min(2 * HEAD_DIM, tn)
    stream_bytes = 2 * (kd * tn * 4 + tm * tn * 2) + kd * sub * 2 + tm * sub * 4
    rows_mode = {} if 2 * row_bytes + stream_bytes <= VMEM_LIMIT_BYTES else {"pipeline_mode": pl.Buffered(1)}
    tspec = pl.BlockSpec((tm, HEAD_DIM), lambda i, j: (i % pos_blocks, 0), **rows_mode)
    return pl.pallas_call(
        functools.partial(_proj_rope_kernel, tile_mode=tile_mode, w_rows_are_outputs=w_rows_are_outputs),
        grid=(m // tm, n // tn),
        in_specs=[pl.BlockSpec((tm, kd), lambda i, j: (i, 0), **rows_mode),
                  (pl.BlockSpec((None, tn, kd), lambda i, j: (layer, j, 0)) if w_rows_are_outputs
                   else pl.BlockSpec((None, kd, tn), lambda i, j: (layer, 0, j))),
                  tspec, tspec, tspec],
        out_specs=pl.BlockSpec((tm, tn), lambda i, j: (i, j)),
        out_shape=jax.ShapeDtypeStruct((m, n), BF16),
        compiler_params=_params("parallel", "parallel"),
    )(x, w, *tables)


def _flash_init_many(slots, s_list, v_list, m_ref, l_ref, acc_ref):
    m = [jnp.max(s, axis=0, keepdims=True) for s in s_list]
    p = [jnp.exp2(s - mm) for s, mm in zip(s_list, m)]
    pv = [_dot_tn(v, pp.astype(v.dtype)) for v, pp in zip(v_list, p)]
    for idx, c in enumerate(slots):
        m_ref[c] = m[idx]
        l_ref[c] = jnp.sum(p[idx], axis=0, keepdims=True)
        acc_ref[c] = pv[idx]


def _flash_update_many(slots, s_list, v_list, m_ref, l_ref, acc_ref):
    m_old = [m_ref[c] for c in slots]
    l_old = [l_ref[c] for c in slots]
    m_new = [jnp.maximum(mo, jnp.max(s, axis=0, keepdims=True)) for mo, s in zip(m_old, s_list)]
    p = [jnp.exp2(s - mn) for s, mn in zip(s_list, m_new)]
    pv = [_dot_tn(v, pp.astype(v.dtype)) for v, pp in zip(v_list, p)]
    for idx, c in enumerate(slots):
        alpha = jnp.exp2(m_old[idx] - m_new[idx])
        m_ref[c] = m_new[idx]
        l_ref[c] = alpha * l_old[idx] + jnp.sum(p[idx], axis=0, keepdims=True)
        acc_ref[c] = alpha * acc_ref[c] + pv[idx]


def _flash_result(slot, l_ref, acc_ref):
    return acc_ref[slot] * (1.0 / l_ref[slot])


def _softmax_pv(s_parts, v):
    e = [jnp.exp2(s - jnp.max(s, axis=0, keepdims=True)) for s in s_parts]
    pv = [_dot_tn(v, p.astype(v.dtype)) for p in e]
    return jnp.concatenate([o * (1.0 / jnp.sum(p, axis=0, keepdims=True)) for o, p in zip(pv, e)], axis=1)


def _pad_rows(n_rows):
    return -(-n_rows // SUBLANES) * SUBLANES


def _rank_before(v, n_rows):
    row = lax.broadcasted_iota(jnp.int32, v.shape, 0)
    rank = jnp.zeros(v.shape, jnp.int32)
    for j2 in range(n_rows):
        other = v[j2:j2 + 1, :]
        rank = rank + jnp.where(other > v, 1, jnp.where(other == v, (row > j2).astype(jnp.int32), 0))
    return rank


def _diff_kernel(lam_ref, g_ref, q_ref, k_ref, v_ref, o_ref, m_ref, l_ref, acc_ref, *, tq, hp, lam_init):
    i = pl.program_id(2)
    w = DIFF_V_DIM
    lp = lam_ref[...]
    lam = (jnp.exp(jnp.sum(lp[0:1] * lp[1:2], axis=-1, keepdims=True))
           - jnp.exp(jnp.sum(lp[2:3] * lp[3:4], axis=-1, keepdims=True)) + lam_init)
    key = lax.broadcasted_iota(jnp.int32, (tq, tq), 0)
    qry = lax.broadcasted_iota(jnp.int32, (tq, tq), 1)

    def scores(c, start):
        cols = slice(c * HEAD_DIM, (c + 1) * HEAD_DIM)
        return _dot_nt(k_ref[0, pl.ds(start, tq), cols], q_ref[0, :, cols])

    own = pl.multiple_of(i * tq, tq)
    s_own = [jnp.where(key <= qry, scores(c, own), NEG) for c in range(2 * hp)]
    v_own = [v_ref[0, pl.ds(own, tq), (c // 2) * w:(c // 2 + 1) * w] for c in range(2 * hp)]
    _flash_init_many(range(2 * hp), s_own, v_own, m_ref, l_ref, acc_ref)

    def body(j, carry):
        start = pl.multiple_of(j * tq, tq)
        s = [scores(c, start) for c in range(2 * hp)]
        v = [v_ref[0, pl.ds(start, tq), (c // 2) * w:(c // 2 + 1) * w] for c in range(2 * hp)]
        _flash_update_many(range(2 * hp), s, v, m_ref, l_ref, acc_ref)
        return carry

    lax.fori_loop(0, i, body, 0)
    for h in range(hp):
        o = (_flash_result(2 * h, l_ref, acc_ref) - lam * _flash_result(2 * h + 1, l_ref, acc_ref)).T
        o_ref[0, :, h * w:(h + 1) * w] = (_rms(o, g_ref[...]) * (1.0 - lam_init)).astype(o_ref.dtype)


def diff_attention(proj, lam_params, subln_g, lam_init, n_heads, q_col, k_col, v_col, tq=256, hp=4):
    b, s, _ = proj.shape
    tq = min(tq, s)
    w = DIFF_V_DIM
    hp = min(hp, n_heads)
    assert n_heads % hp == 0
    gw = hp * w
    qo, ko, vo = q_col // gw, k_col // gw, v_col // gw
    return pl.pallas_call(
        functools.partial(_diff_kernel, tq=tq, hp=hp, lam_init=lam_init),
        grid=(b, n_heads // hp, s // tq),
        in_specs=[pl.BlockSpec((4, HEAD_DIM), lambda bi, h, i: (0, 0)),
                  pl.BlockSpec((1, w), lambda bi, h, i: (0, 0)),
                  pl.BlockSpec((1, tq, gw), lambda bi, h, i: (bi, i, qo + h)),
                  pl.BlockSpec((1, s, gw), lambda bi, h, i: (bi, 0, ko + h)),
                  pl.BlockSpec((1, s, gw), lambda bi, h, i: (bi, 0, vo + h))],
        out_specs=pl.BlockSpec((1, tq, gw), lambda bi, h, i: (bi, i, h)),
        out_shape=jax.ShapeDtypeStruct((b, s, n_heads * w), BF16),
        scratch_shapes=[pltpu.VMEM((2 * hp, 1, tq), F32), pltpu.VMEM((2 * hp, 1, tq), F32),
                        pltpu.VMEM((2 * hp, w, tq), F32)],
        compiler_params=_params("parallel", "parallel", "arbitrary"),
    )(lam_params, subln_g.reshape(1, w), proj, proj, proj)


def _moba_kernel(q_ref, k_ref, v_ref, o_ref, kmean_ref, bias_ref, m_ref, l_ref, acc_ref, *, nb, hp):
    i = pl.program_id(2)
    blk = MOBA_BLOCK
    head = lambda h: slice(h * HEAD_DIM, (h + 1) * HEAD_DIM)

    @pl.when(i == 0)
    def _():
        kmean_ref[...] = jnp.zeros(kmean_ref.shape, F32)
        for h in range(hp):
            for n in range(nb):
                kb = k_ref[0, n * blk:(n + 1) * blk, head(h)].astype(F32)
                kmean_ref[h, n:n + 1, :] = jnp.mean(kb, axis=0, keepdims=True)

    key = lax.broadcasted_iota(jnp.int32, (blk, blk), 0)
    qry = lax.broadcasted_iota(jnp.int32, (blk, blk), 1)
    own = pl.multiple_of(i * blk, blk)
    qs = [q_ref[0, :, head(h)] for h in range(hp)]
    s_own = [jnp.where(key <= qry, _dot_nt(k_ref[0, pl.ds(own, blk), head(h)], qs[h]), NEG) for h in range(hp)]
    v_own = [v_ref[0, pl.ds(own, blk), head(h)] for h in range(hp)]
    biases = []
    for h in range(hp):
        km_hi, km_lo = _split_bf16(kmean_ref[h])
        gate = (_dot_nt(km_hi, qs[h]) + _dot_nt(km_lo, qs[h]))[:_pad_rows(nb)]
        past = lax.broadcasted_iota(jnp.int32, gate.shape, 0) < i
        rank = _rank_before(jnp.where(past, gate, NEG), nb)
        biases.append(jnp.where(past & (rank < MOBA_TOPK), 0.0, NEG))
    for h in range(hp):
        bias_ref[h] = biases[h]
    _flash_init_many(range(hp), s_own, v_own, m_ref, l_ref, acc_ref)

    def body(n, carry):
        start = pl.multiple_of(n * blk, blk)
        sn = [_dot_nt(k_ref[0, pl.ds(start, blk), head(h)], q_ref[0, :, head(h)]) + bias_ref[h, pl.ds(n, 1), :]
              for h in range(hp)]
        vn = [v_ref[0, pl.ds(start, blk), head(h)] for h in range(hp)]
        _flash_update_many(range(hp), sn, vn, m_ref, l_ref, acc_ref)
        return carry

    lax.fori_loop(0, i, body, 0)
    for h in range(hp):
        o_ref[0, :, head(h)] = _flash_result(h, l_ref, acc_ref).T.astype(o_ref.dtype)


def moba_attention(proj, n_heads, q_col, k_col, v_col, hp=8):
    b, s, _ = proj.shape
    blk = MOBA_BLOCK
    hp = min(hp, n_heads)
    assert s % blk == 0 and s // blk <= LANES and n_heads % hp == 0
    nb = s // blk
    gw = hp * HEAD_DIM
    qo, ko, vo = q_col // gw, k_col // gw, v_col // gw
    return pl.pallas_call(
        functools.partial(_moba_kernel, nb=nb, hp=hp),
        grid=(b, n_heads // hp, nb),
        in_specs=[pl.BlockSpec((1, blk, gw), lambda bi, h, i: (bi, i, qo + h)),
                  pl.BlockSpec((1, s, gw), lambda bi, h, i: (bi, 0, ko + h)),
                  pl.BlockSpec((1, s, gw), lambda bi, h, i: (bi, 0, vo + h))],
        out_specs=pl.BlockSpec((1, blk, gw), lambda bi, h, i: (bi, i, h)),
        out_shape=jax.ShapeDtypeStruct((b, s, n_heads * HEAD_DIM), BF16),
        scratch_shapes=[pltpu.VMEM((hp, LANES, HEAD_DIM), F32), pltpu.VMEM((hp, _pad_rows(nb), blk), F32),
                        pltpu.VMEM((hp, 1, blk), F32), pltpu.VMEM((hp, 1, blk), F32),
                        pltpu.VMEM((hp, HEAD_DIM, blk), F32)],
        compiler_params=_params("parallel", "parallel", "arbitrary"),
    )(proj, proj, proj)


def _compress_kernel(t_ref, pe_ref, w1_ref, w2_ref, o_ref):
    t = t_ref[0, 0].astype(F32)
    half = t.shape[1]
    top = (t + pe_ref[0, :, :half]).astype(BF16)
    bot = (t + pe_ref[0, :, half:]).astype(BF16)
    a = _dot(top, w1_ref[0, :half, :].astype(BF16))
    bm = _dot(bot, w1_ref[0, half:, :].astype(BF16))
    hidden = a + pltpu.roll(bm, bm.shape[0] - 1, 0)
    act = hidden * jax.nn.sigmoid(hidden)
    o_ref[0, 0] = _dot(act.astype(BF16), w2_ref[0].astype(BF16)).astype(o_ref.dtype)


def nsa_compress(t16, pe_flat, w1, w2):
    b, g2, nc, width = t16.shape
    g = g2 // 2
    hid = w1.shape[-1]
    return pl.pallas_call(
        _compress_kernel,
        grid=(b, g2),
        in_specs=[pl.BlockSpec((1, 1, nc, width), lambda bi, j: (bi, j, 0, 0)),
                  pl.BlockSpec((1, 1, 2 * width), lambda bi, j: (j // g, 0, 0)),
                  pl.BlockSpec((1, 2 * width, hid), lambda bi, j: (j // g, 0, 0)),
                  pl.BlockSpec((1, hid, HEAD_DIM), lambda bi, j: (j // g, 0, 0))],
        out_specs=pl.BlockSpec((1, 1, nc, HEAD_DIM), lambda bi, j: (bi, j, 0, 0)),
        out_shape=jax.ShapeDtypeStruct((b, g2, nc, HEAD_DIM), BF16),
        compiler_params=_params("parallel", "parallel"),
    )(t16, pe_flat, w1, w2)


def _lanes_x(a, n):
    return jnp.concatenate([a] * n, axis=1)


def _nsa_kernel(q_ref, ks_ref, vs_ref, kw_ref, vw_ref, kc_ref, vc_ref, gt_ref, cover_ref, expand_ref,
                o_ref, part_ref, m_ref, l_ref, acc_ref, *, tq, tk, n_sel):
    i = pl.program_id(2)
    z_heads = NSA_GROUP
    nc = kc_ref.shape[2]
    q = jnp.concatenate([q_ref[0, :, z * HEAD_DIM:(z + 1) * HEAD_DIM] for z in range(z_heads)], axis=0)
    qpos = i * tq + lax.broadcasted_iota(jnp.int32, (1, tq), 1)

    cmp_end = lax.broadcasted_iota(jnp.int32, (nc, tq), 0) * NSA_CMP_STRIDE + (NSA_CMP_LEN - 1)
    cmp_ok = _lanes_x(jnp.where(cmp_end <= qpos, 1.0, 0.0), z_heads)
    sc = jnp.where(cmp_ok > 0.5, _dot_nt(kc_ref[0, 0], q), NEG)
    pc = jnp.exp2(sc - jnp.max(sc, axis=0, keepdims=True))
    pc = pc * (1.0 / jnp.sum(pc, axis=0, keepdims=True)) * cmp_ok
    o_cmp = _dot_tn(vc_ref[0, 0], pc.astype(BF16))

    p_sum = pc[:, 0:tq]
    for z in range(1, z_heads):
        p_sum = p_sum + pc[:, z * tq:(z + 1) * tq]
    p_hi, p_lo = _split_bf16(p_sum)
    imp = (_dot(cover_ref[...], p_hi) + _dot(cover_ref[...], p_lo))[:_pad_rows(n_sel)]
    row = lax.broadcasted_iota(jnp.int32, imp.shape, 0)
    qblk = qpos // NSA_SEL_BLOCK
    forced = (row == 0) | (row == qblk) | (row == qblk - 1)
    allowed = row <= qblk
    imp = jnp.where(forced, BIG, jnp.where(allowed, imp, NEG))
    rank = _rank_before(imp, n_sel)
    block_bias = jnp.where(allowed & (rank < NSA_SEL_TOPN), 0.0, NEG)
    block_bias = jnp.concatenate(
        [block_bias, jnp.zeros((LANES - block_bias.shape[0], tq), F32)], axis=0).T.astype(BF16)
    q2 = jnp.concatenate([q, jnp.concatenate([block_bias] * z_heads, axis=0)], axis=1)
    span = NSA_WINDOW + tq
    w_start = pl.multiple_of(jnp.maximum(i * tq - NSA_WINDOW, 0), tq)
    dist = qpos - (w_start + lax.broadcasted_iota(jnp.int32, (span, tq), 0))
    w_bias = jnp.where((dist >= 0) & (dist < NSA_WINDOW), 0.0, NEG)
    kw = kw_ref[0, pl.ds(w_start, span), :]
    half_rows = z_heads * tq // 2
    sw = [_dot_nt(kw, q[p * half_rows:(p + 1) * half_rows]) + _lanes_x(w_bias, z_heads // 2) for p in range(2)]
    o_win = _softmax_pv(sw, vw_ref[0, pl.ds(w_start, span), :])
    gt = jax.nn.sigmoid(gt_ref[...])
    part_ref[...] = jnp.concatenate(
        [gt[3 * z:3 * z + 1] * o_cmp[:, z * tq:(z + 1) * tq] + gt[3 * z + 2:3 * z + 3] * o_win[:, z * tq:(z + 1) * tq]
         for z in range(z_heads)], axis=1)

    n_split = NSA_LANE_SPLIT
    part_w = z_heads * tq // n_split
    halves = tuple(slice(p * part_w, (p + 1) * part_w) for p in range(n_split))

    def sel_pair(t, causal, tiles=(0, 1)):
        s, v = [], []
        for c in tiles:
            start = (2 * t + c) * tk
            if not isinstance(start, int):
                start = pl.multiple_of(start, tk)
            k2 = jnp.concatenate([ks_ref[0, pl.ds(start, tk), :], expand_ref[pl.ds(start, tk), :]], axis=1)
            sc = _dot_nt(k2, q2)
            if causal:
                kpos = start + lax.broadcasted_iota(jnp.int32, (tk, tq), 0)
                sc = sc + _lanes_x(jnp.where(kpos <= qpos, 0.0, NEG), z_heads)
            vt = vs_ref[0, pl.ds(start, tk), :]
            for lanes in halves:
                s.append(sc[:, lanes])
                v.append(vt)
        return s, v

    n_pairs = ((i + 1) * tq + 2 * tk - 1) // (2 * tk)
    s0, v0 = sel_pair(0, True)
    _flash_init_many(range(2 * n_split), s0, v0, m_ref, l_ref, acc_ref)

    def sel_body(t, carry):
        st, vt = sel_pair(t, False)
        _flash_update_many(range(2 * n_split), st, vt, m_ref, l_ref, acc_ref)
        return carry

    lax.fori_loop(1, n_pairs - 1, sel_body, 0)

    odd_needed = (2 * n_pairs - 1) * tk < (i + 1) * tq

    @pl.when((n_pairs > 1) & odd_needed)
    def _():
        st, vt = sel_pair(n_pairs - 1, True)
        _flash_update_many(range(2 * n_split), st, vt, m_ref, l_ref, acc_ref)

    @pl.when((n_pairs > 1) & jnp.logical_not(odd_needed))
    def _():
        st, vt = sel_pair(n_pairs - 1, True, tiles=(0,))
        _flash_update_many(range(n_split), st, vt, m_ref, l_ref, acc_ref)

    merged = []
    for a in range(n_split):
        b = n_split + a
        m_all = jnp.maximum(m_ref[a], m_ref[b])
        w0, w1 = jnp.exp2(m_ref[a] - m_all), jnp.exp2(m_ref[b] - m_all)
        merged.append((w0 * acc_ref[a] + w1 * acc_ref[b]) * (1.0 / (w0 * l_ref[a] + w1 * l_ref[b])))
    o_sel = jnp.concatenate(merged, axis=1)

    for z in range(z_heads):
        ls = slice(z * tq, (z + 1) * tq)
        o = part_ref[:, ls] + jax.nn.sigmoid(gt_ref[3 * z + 1:3 * z + 2, :]) * o_sel[:, ls]
        o_ref[0, :, z * HEAD_DIM:(z + 1) * HEAD_DIM] = o.T.astype(o_ref.dtype)


def nsa_attention(proj, gates_t, cmp_kv, n_groups, q_col, ks_col, vs_col, kw_col, vw_col, tq=128, tk=256):
    b, s, _ = proj.shape
    tq, tk = min(tq, s), min(tk, s)
    assert s % (2 * tk) == 0 and tk % tq == 0 and NSA_WINDOW % tq == 0 and tq % NSA_SEL_BLOCK == 0
    assert s >= NSA_WINDOW + tq
    nc = cmp_kv.shape[2]
    n_sel = s // NSA_SEL_BLOCK
    assert nc * NSA_CMP_STRIDE == s and n_sel <= LANES
    c_lo = np.arange(nc) * NSA_CMP_STRIDE
    j_lo = np.arange(LANES) * NSA_SEL_BLOCK
    cover_t = ((c_lo[None, :] < j_lo[:, None] + NSA_SEL_BLOCK) & (c_lo[None, :] + NSA_CMP_LEN > j_lo[:, None])
               & (np.arange(LANES)[:, None] < n_sel) & (c_lo[None, :] + NSA_CMP_LEN <= s))
    expand_t = (np.arange(s)[:, None] // NSA_SEL_BLOCK == np.arange(LANES)[None, :])
    gw = NSA_GROUP * HEAD_DIM
    qo = q_col // gw
    kso, vso, kwo, vwo = (c // HEAD_DIM for c in (ks_col, vs_col, kw_col, vw_col))
    kv = lambda off: pl.BlockSpec((1, s, HEAD_DIM), lambda bi, g, i: (bi, 0, off + g))
    q_tiles = s // tq
    rows = NSA_GROUP * tq
    return pl.pallas_call(
        functools.partial(_nsa_kernel, tq=tq, tk=tk, n_sel=n_sel),
        grid=(b, n_groups, q_tiles),
        in_specs=[pl.BlockSpec((1, tq, gw), lambda bi, g, i: (bi, i, qo + g)),
                  kv(kso), kv(vso), kv(kwo), kv(vwo),
                  pl.BlockSpec((1, 1, nc, HEAD_DIM), lambda bi, g, i: (bi, g, 0, 0)),
                  pl.BlockSpec((1, 1, nc, HEAD_DIM), lambda bi, g, i: (bi, n_groups + g, 0, 0)),
                  pl.BlockSpec((NSA_GATE_PAD, tq), lambda bi, g, i: (g, bi * q_tiles + i)),
                  pl.BlockSpec((LANES, nc), lambda bi, g, i: (0, 0)),
                  pl.BlockSpec((s, LANES), lambda bi, g, i: (0, 0))],
        out_specs=pl.BlockSpec((1, tq, gw), lambda bi, g, i: (bi, i, g)),
        out_shape=jax.ShapeDtypeStruct((b, s, n_groups * gw), BF16),
        scratch_shapes=[pltpu.VMEM((HEAD_DIM, rows), F32),
                        pltpu.VMEM((2 * NSA_LANE_SPLIT, 1, rows // NSA_LANE_SPLIT), F32),
                        pltpu.VMEM((2 * NSA_LANE_SPLIT, 1, rows // NSA_LANE_SPLIT), F32),
                        pltpu.VMEM((2 * NSA_LANE_SPLIT, HEAD_DIM, rows // NSA_LANE_SPLIT), F32)],
        compiler_params=_params("parallel", "parallel", "arbitrary"),
    )(proj, proj, proj, proj, proj, cmp_kv, cmp_kv, gates_t, jnp.asarray(cover_t, BF16),
      jnp.asarray(expand_t, BF16))


def _even_tile_mode(tn, d_model):
    seg_tiles = (d_model // 2) // tn
    return lambda j: ROPE_SCALED - (j // seg_tiles) % 3


def even_mixer(hn, w_in, w_out, e, lam_params, subln_g, lam_init, tables, batch, seq):
    d_model = hn.shape[1]
    seg = d_model // 2
    tn = min(1024, seg)
    proj = proj_rope(hn, w_in, e, 6 * seg, tables, seq, _even_tile_mode(tn, d_model), tn=tn)
    proj = proj.reshape(batch, seq, -1)
    o_diff = diff_attention(proj, lam_params, subln_g, lam_init, seg // DIFF_V_DIM, 0, seg, 2 * seg)
    o_moba = moba_attention(proj, seg // HEAD_DIM, 3 * seg, 4 * seg, 5 * seg)
    rows = batch * seq
    return matmul_cat(o_diff.reshape(rows, seg), o_moba.reshape(rows, seg), w_out, e, F32)


def _odd_tile_mode(tn, d_model):
    slab_tiles = (d_model // 8) // tn
    q_tiles = d_model // tn
    return lambda j: jnp.where(j < q_tiles, ROPE_SCALED, ROPE - ((j - q_tiles) // slab_tiles) % 2)


def odd_mixer(hn, w_in, w_out, o_idx, cmp_pe, cmp_w1, cmp_w2, tables, batch, seq):
    d_model = hn.shape[1]
    g = d_model // (NSA_GROUP * HEAD_DIM)
    kvd = g * HEAD_DIM
    n_main = d_model + 6 * kvd
    tn = min(512, kvd)
    w_in_t = jnp.swapaxes(w_in, 1, 2)
    proj = proj_rope(hn, w_in_t, o_idx, n_main, tables, seq, _odd_tile_mode(tn, d_model), tn=tn,
                     w_rows_are_outputs=True)
    proj = proj.reshape(batch, seq, n_main)
    n_gate = 3 * NSA_GROUP
    w_gate_t = w_in_t[o_idx, n_main:, :].reshape(g, n_gate, d_model)
    w_gate_t = jnp.pad(w_gate_t, ((0, 0), (0, NSA_GATE_PAD - n_gate), (0, 0))).reshape(g * NSA_GATE_PAD, d_model)
    gates_t = matmul_nt(w_gate_t, hn)
    nc = seq // NSA_CMP_STRIDE
    t16 = proj[:, :, d_model:d_model + 2 * kvd].reshape(batch, seq, 2 * g, HEAD_DIM)
    t16 = t16.transpose(0, 2, 1, 3).reshape(batch, 2 * g, nc, NSA_CMP_STRIDE * HEAD_DIM)
    cmp_kv = nsa_compress(t16, cmp_pe.reshape(2, 1, NSA_CMP_LEN * HEAD_DIM), cmp_w1, cmp_w2)
    o = nsa_attention(proj, gates_t, cmp_kv, g, 0, d_model + 2 * kvd, d_model + 3 * kvd, d_model + 4 * kvd,
                      d_model + 5 * kvd)
    return matmul(o.reshape(batch * seq, d_model), w_out, o_idx, F32)


def ffn(hn, w_gate, w_up, w_down, layer, rounded, next_weights):
    if rounded is not None:
        w_gate, w_up = rounded
    a, w_down_bf16, next_rounded = swiglu_up(hn, w_gate, w_up, w_down, layer, next_weights)
    return matmul_ksplit(a, w_down_bf16, 2), next_rounded


def kernel(x, norm_g, ffn_a_gate, ffn_a_up, ffn_a_down, ffn_b_gate, ffn_b_up, ffn_b_down, ev_w_in, ev_w_out, ev_lambda, ev_subln_g, od_w_in, od_w_out, od_cmp_pe, od_cmp_w1, od_cmp_w2):
    batch, seq, d_model = x.shape
    depth = norm_g.shape[0]
    tables = rope_tables(seq)
    h = x.reshape(batch * seq, d_model)
    hn = rmsnorm(h, norm_g[0, 0])
    rounded = None
    for layer in range(depth):
        g = norm_g[layer]
        m, rounded = ffn(hn, ffn_a_gate, ffn_a_up, ffn_a_down, layer, rounded, (ffn_b_gate, ffn_b_up, layer))
        h, hn = residual_norm(h, m, g[1], g[2], 0.5)
        if layer % 2 == 0:
            e = layer // 2
            lam_init = 0.8 - 0.6 * math.exp(-0.3 * layer)
            m = even_mixer(hn, ev_w_in, ev_w_out, e, ev_lambda[e], ev_subln_g[e], lam_init, tables, batch, seq)
        else:
            o = layer // 2
            m = odd_mixer(hn, od_w_in, od_w_out, o, od_cmp_pe[o], od_cmp_w1[o], od_cmp_w2[o], tables, batch, seq)
        h, hn = residual_norm(h, m, g[3], g[4], 1.0)
        upcoming = (ffn_a_gate, ffn_a_up, layer + 1) if layer + 1 < depth else None
        m, rounded = ffn(hn, ffn_b_gate, ffn_b_up, ffn_b_down, layer, rounded, upcoming)
        g_next = norm_g[layer + 1, 0] if layer + 1 < depth else None
        h, hn = residual_norm(h, m, g[5], g_next, 0.5)
    return h.reshape(batch, seq, d_model)
```
